```python
import jax, jax.numpy as jnp
from jax import lax
import numpy as np

D_MODEL = 4096
BATCH = 1
SEQ = 8192
DEPTH = 1
DEC_BATCH = 2
DEC_SEQ = 4096
PAST_LEN = 128

N_MEM = 256
CHUNK = 128
GMLP_WIDTH = D_MODEL // 2
GMLP_GROUPS = 16
GMLP_GROUP_CH = GMLP_WIDTH // GMLP_GROUPS
CONV_WIDTH = D_MODEL // 4
CONV_K = 3
XATTN_HEADS = 4
XATTN_WIDTH = D_MODEL // 4
XATTN_HEAD_DIM = XATTN_WIDTH // XATTN_HEADS
N_BRANCH = 3
OFF_U = 0
OFF_V = OFF_U + GMLP_WIDTH
OFF_CB = OFF_V + GMLP_WIDTH
OFF_CC = OFF_CB + CONV_WIDTH
OFF_CH = OFF_CC + CONV_WIDTH
OFF_Q = OFF_CH + CONV_WIDTH
OFF_G = OFF_Q + XATTN_WIDTH
IN_COLS = OFF_G + N_BRANCH * D_MODEL
N_EXPERTS = 16
EXPERT_FF = D_MODEL
CAPACITY_FACTOR = 2
EPS = 1e-6

kernel_name = "hybrid_gmlp_shortconv_memxattn_ec_moe_encoder"


def rmsnorm(x, g):
    x32 = x.astype(jnp.float32)
    y = x32 * lax.rsqrt(jnp.mean(x32 * x32, axis=-1, keepdims=True) + EPS)
    return y.astype(x.dtype) * g


def token_mixer(h, mem, w_in, gmlp_norm_g, w_spatial, b_spatial, conv_w, mem_norm_g,
                w_mem_kv, w_branch_a, w_branch_b, w_branch_x, w_out):
    bsz, s, _ = h.shape
    proj = jnp.einsum('bsd,dk->bsk', h, w_in)

    u = jax.nn.gelu(proj[..., OFF_U:OFF_V])
    v = rmsnorm(jax.nn.gelu(proj[..., OFF_V:OFF_CB]), gmlp_norm_g)
    vc = v.reshape(bsz, s // CHUNK, CHUNK, GMLP_GROUPS, GMLP_GROUP_CH)
    mixed = jnp.einsum('gij,bnjgc->bnigc', w_spatial, vc) + b_spatial.T[None, None, :, :, None]
    y_a = u * mixed.reshape(bsz, s, GMLP_WIDTH)

    gate_b = proj[..., OFF_CB:OFF_CC]
    gate_c = proj[..., OFF_CC:OFF_CH]
    val = proj[..., OFF_CH:OFF_Q]
    z = gate_c * val
    zp = jnp.pad(z, ((0, 0), (1, 1), (0, 0)))
    conv = zp[:, :-2] * conv_w[0] + zp[:, 1:-1] * conv_w[1] + zp[:, 2:] * conv_w[2]
    y_b = gate_b * conv

    q = proj[..., OFF_Q:OFF_G].reshape(bsz, s, XATTN_HEADS, XATTN_HEAD_DIM)
    m = rmsnorm(mem, mem_norm_g)
    kv = jnp.einsum('bmd,dk->bmk', m, w_mem_kv)
    k = kv[..., :XATTN_WIDTH].reshape(bsz, N_MEM, XATTN_HEADS, XATTN_HEAD_DIM)
    vm = kv[..., XATTN_WIDTH:].reshape(bsz, N_MEM, XATTN_HEADS, XATTN_HEAD_DIM)
    scores = jnp.einsum('bshd,bmhd->bhsm', q, k).astype(jnp.float32) * (XATTN_HEAD_DIM ** -0.5)
    p = jax.nn.softmax(scores, axis=-1).astype(vm.dtype)
    y_x = jnp.einsum('bhsm,bmhd->bshd', p, vm).reshape(bsz, s, XATTN_WIDTH)

    g_a = jax.nn.sigmoid(proj[..., OFF_G:OFF_G + D_MODEL])
    g_b = jax.nn.sigmoid(proj[..., OFF_G + D_MODEL:OFF_G + 2 * D_MODEL])
    g_x = jax.nn.sigmoid(proj[..., OFF_G + 2 * D_MODEL:IN_COLS])
    merged = (g_a * jnp.einsum('bsk,kd->bsd', y_a, w_branch_a)
              + g_b * jnp.einsum('bsk,kd->bsd', y_b, w_branch_b)
              + g_x * jnp.einsum('bsk,kd->bsd', y_x, w_branch_x))
    return jnp.einsum('bsd,de->bse', merged, w_out)


def expert_choice_ffn(h, w_router, w_gate, w_up, w_down):
    bsz, s, d = h.shape
    n = bsz * s
    cap = CAPACITY_FACTOR * n // N_EXPERTS
    t = h.reshape(n, d)
    aff = jax.nn.softmax(jnp.einsum('nd,de->ne', t, w_router).astype(jnp.float32), axis=-1)
    gw, idx = lax.top_k(aff.T, cap)
    xe = t[idx]
    hid = jax.nn.silu(jnp.einsum('ecd,edf->ecf', xe, w_gate)) * jnp.einsum('ecd,edf->ecf', xe, w_up)
    ye = jnp.einsum('ecf,efd->ecd', hid, w_down) * gw[..., None].astype(h.dtype)
    out = jnp.zeros_like(t).at[idx.reshape(-1)].add(ye.reshape(-1, d))
    return out.reshape(bsz, s, d)


def encode(x, mem, norm_mix_g, w_in, gmlp_norm_g, w_spatial, b_spatial, conv_w, mem_norm_g,
           w_mem_kv, w_branch_a, w_branch_b, w_branch_x, w_out, norm_ffn_g, w_router,
           w_gate, w_up, w_down, final_norm_g):
    for l in range(DEPTH):
        h = rmsnorm(x, norm_mix_g[l])
        x = x + token_mixer(h, mem, w_in[l], gmlp_norm_g[l], w_spatial[l], b_spatial[l], conv_w[l],
                            mem_norm_g[l], w_mem_kv[l], w_branch_a[l], w_branch_b[l],
                            w_branch_x[l], w_out[l])
        h = rmsnorm(x, norm_ffn_g[l])
        x = x + expert_choice_ffn(h, w_router[l], w_gate[l], w_up[l], w_down[l])
    return rmsnorm(x, final_norm_g)


def setup_inputs(seed: int = 0) -> dict:
    key = jax.random.key(seed)
    ks = jax.random.split(key, 26)
    f32 = jnp.float32

    def nrm(k, shape, scale=1.0):
        return jax.random.normal(k, shape, f32) * scale

    def gain(k, shape):
        return 1.0 + 0.05 * jax.random.normal(k, shape, f32)

    L = DEPTH
    return {
        "x_prompt": nrm(ks[0], (BATCH, SEQ, D_MODEL)),
        "x_sample": nrm(ks[1], (DEC_BATCH, DEC_SEQ, D_MODEL)),
        "mem_prompt": nrm(ks[2], (BATCH, N_MEM, D_MODEL)),
        "mem_sample": nrm(ks[3], (DEC_BATCH, N_MEM, D_MODEL)),
        "norm_mix_g": gain(ks[4], (L, D_MODEL)),
        "w_in": nrm(ks[5], (L, D_MODEL, IN_COLS), D_MODEL ** -0.5),
        "gmlp_norm_g": gain(ks[6], (L, GMLP_WIDTH)),
        "w_spatial": nrm(ks[7], (L, GMLP_GROUPS, CHUNK, CHUNK), CHUNK ** -0.5),
        "b_spatial": 1.0 + 0.1 * jax.random.normal(ks[8], (L, GMLP_GROUPS, CHUNK), f32),
        "conv_w": nrm(ks[9], (L, CONV_K, CONV_WIDTH), CONV_K ** -0.5),
        "mem_norm_g": gain(ks[10], (L, D_MODEL)),
        "w_mem_kv": nrm(ks[11], (L, D_MODEL, 2 * XATTN_WIDTH), D_MODEL ** -0.5),
        "w_branch_a": nrm(ks[12], (L, GMLP_WIDTH, D_MODEL), GMLP_WIDTH ** -0.5),
        "w_branch_b": nrm(ks[13], (L, CONV_WIDTH, D_MODEL), CONV_WIDTH ** -0.5),
        "w_branch_x": nrm(ks[14], (L, XATTN_WIDTH, D_MODEL), XATTN_WIDTH ** -0.5),
        "w_out": nrm(ks[15], (L, D_MODEL, D_MODEL), D_MODEL ** -0.5),
        "norm_ffn_g": gain(ks[16], (L, D_MODEL)),
        "w_router": nrm(ks[17], (L, D_MODEL, N_EXPERTS), D_MODEL ** -0.5),
        "w_gate": nrm(ks[18], (L, N_EXPERTS, D_MODEL, EXPERT_FF), D_MODEL ** -0.5),
        "w_up": nrm(ks[19], (L, N_EXPERTS, D_MODEL, EXPERT_FF), D_MODEL ** -0.5),
        "w_down": nrm(ks[20], (L, N_EXPERTS, EXPERT_FF, D_MODEL), EXPERT_FF ** -0.5),
        "final_norm_g": gain(ks[21], (D_MODEL,)),
    }


def reference(x_prompt, x_sample, mem_prompt, mem_sample, norm_mix_g, w_in, gmlp_norm_g,
              w_spatial, b_spatial, conv_w, mem_norm_g, w_mem_kv, w_branch_a, w_branch_b,
              w_branch_x, w_out, norm_ffn_g, w_router, w_gate, w_up, w_down, final_norm_g):
    y_prompt = encode(x_prompt, mem_prompt, norm_mix_g, w_in, gmlp_norm_g, w_spatial, b_spatial,
                      conv_w, mem_norm_g, w_mem_kv, w_branch_a, w_branch_b, w_branch_x, w_out,
                      norm_ffn_g, w_router, w_gate, w_up, w_down, final_norm_g)
    y_sample = encode(x_sample, mem_sample, norm_mix_g, w_in, gmlp_norm_g, w_spatial, b_spatial,
                      conv_w, mem_norm_g, w_mem_kv, w_branch_a, w_branch_b, w_branch_x, w_out,
                      norm_ffn_g, w_router, w_gate, w_up, w_down, final_norm_g)
    return (y_prompt, y_sample)
```

```python
import functools

import jax
import jax.numpy as jnp
from jax import lax
from jax.experimental import pallas as pl
from jax.experimental.pallas import tpu as pltpu

EPS = 1e-6
XATTN_HEADS = 4
CAPACITY_FACTOR = 2
BF16_ROW_TILE = 16
VMEM_LIMIT_BYTES = 56 * 1024 * 1024

F32 = jnp.float32
BF16 = jnp.bfloat16


def _params(*semantics):
    return pltpu.CompilerParams(dimension_semantics=semantics,
                                vmem_limit_bytes=VMEM_LIMIT_BYTES)


def _gelu_tanh(x):
    return 0.5 * x * (1.0 + jnp.tanh(0.7978845608028654 * (x + 0.044715 * (x * x * x))))


def _sigmoid(x):
    return 1.0 / (1.0 + jnp.exp(-x))


def _rms_scale(x):
    return x * lax.rsqrt(jnp.mean(x * x, axis=-1, keepdims=True) + EPS)


def _rmsnorm_cast_kernel(x_ref, g_ref, o_ref):
    o_ref[...] = (_rms_scale(x_ref[...]) * g_ref[...]).astype(o_ref.dtype)


def _rmsnorm_cast(x, g, tm):
    n, d = x.shape
    return pl.pallas_call(
        _rmsnorm_cast_kernel,
        grid=(n // tm,),
        in_specs=[pl.BlockSpec((tm, d), lambda i: (i, 0)),
                  pl.BlockSpec((1, d), lambda i: (0, 0))],
        out_specs=pl.BlockSpec((tm, d), lambda i: (i, 0)),
        out_shape=jax.ShapeDtypeStruct((n, d), BF16),
        compiler_params=_params("parallel"),
        name="rmsnorm_cast",
    )(x, g.reshape(1, d))


def _proj_kernel(h_ref, w_ref, o_ref, *, gelu_tiles, sigmoid_from):
    j = pl.program_id(1)
    acc = jnp.dot(h_ref[...], w_ref[...], preferred_element_type=F32)

    @pl.when(j < gelu_tiles)
    def _():
        o_ref[...] = _gelu_tanh(acc).astype(o_ref.dtype)

    @pl.when(jnp.logical_and(j >= gelu_tiles, j < sigmoid_from))
    def _():
        o_ref[...] = acc.astype(o_ref.dtype)

    @pl.when(j >= sigmoid_from)
    def _():
        o_ref[...] = _sigmoid(acc).astype(o_ref.dtype)


def _proj(h, w, tm, tn, gelu_cols, sigmoid_from_col, out_dtype):
    n, d = h.shape
    cols = w.shape[1]
    kern = functools.partial(_proj_kernel, gelu_tiles=gelu_cols // tn,
                             sigmoid_from=sigmoid_from_col // tn)
    return pl.pallas_call(
        kern,
        grid=(n // tm, cols // tn),
        in_specs=[pl.BlockSpec((tm, d), lambda i, j: (i, 0)),
                  pl.BlockSpec((d, tn), lambda i, j: (0, j))],
        out_specs=pl.BlockSpec((tm, tn), lambda i, j: (i, j)),
        out_shape=jax.ShapeDtypeStruct((n, cols), out_dtype),
        compiler_params=_params("parallel", "arbitrary"),
        name="in_proj",
    )(h, w)


def _branch_kernel(u_ref, v_ref, cb_ref, cc_ref, ch_ref, q_ref, ccp_ref, chp_ref, ccn_ref, chn_ref,
                   kv_ref, wsp_ref, bsp_ref, gg_ref, cw_ref, y_ref, *, seq, chunk, groups, heads):
    i = pl.program_id(0)
    ts = u_ref.shape[0]
    gw = u_ref.shape[1]
    gc = gw // groups
    cw = cb_ref.shape[1]
    xw = q_ref.shape[1]
    hd = xw // heads

    vn = (_rms_scale(v_ref[...]) * gg_ref[...]).astype(BF16)
    for c in range(ts // chunk):
        rows = slice(c * chunk, (c + 1) * chunk)
        for g in range(groups):
            cols = slice(g * gc, (g + 1) * gc)
            mixed = jnp.dot(wsp_ref[g], vn[rows, cols], preferred_element_type=F32) + bsp_ref[g]
            y_ref[rows, cols] = (u_ref[rows, cols] * mixed).astype(y_ref.dtype)

    pos = (i * ts) % seq
    z = cc_ref[...] * ch_ref[...]
    sub = ccp_ref.shape[0]
    z_before = jnp.where(pos == 0, 0.0, ccp_ref[sub - 1:sub, :] * chp_ref[sub - 1:sub, :])
    z_after = jnp.where(pos + ts == seq, 0.0, ccn_ref[0:1, :] * chn_ref[0:1, :])
    row = lax.broadcasted_iota(jnp.int32, z.shape, 0)
    z_prev = jnp.where(row == 0, z_before, pltpu.roll(z, 1, 0))
    z_next = jnp.where(row == ts - 1, z_after, pltpu.roll(z, ts - 1, 0))
    conv = z_prev * cw_ref[0:1, :] + z * cw_ref[1:2, :] + z_next * cw_ref[2:3, :]
    y_ref[:, gw:gw + cw] = (cb_ref[...] * conv).astype(y_ref.dtype)

    scale = hd ** -0.5
    for h in range(heads):
        cols = slice(h * hd, (h + 1) * hd)
        qh = q_ref[:, cols].astype(BF16)
        kh = kv_ref[:, cols]
        vh = kv_ref[:, xw + h * hd: xw + (h + 1) * hd]
        s = lax.dot_general(qh, kh, (((1,), (1,)), ((), ())), preferred_element_type=F32) * scale
        e = jnp.exp(s - jnp.max(s, axis=-1, keepdims=True))
        p = (e / jnp.sum(e, axis=-1, keepdims=True)).astype(BF16)
        yh = jnp.dot(p, vh, preferred_element_type=F32)
        y_ref[:, gw + cw + h * hd: gw + cw + (h + 1) * hd] = yh.astype(y_ref.dtype)


def _branches(proj, kv, wsp, bsp, gg, cw, *, seq, ts, gmlp_w, conv_w, xattn_w, n_mem):
    n = proj.shape[0]
    chunk = wsp.shape[1]
    groups = wsp.shape[0]
    sub = 8
    off_cb = 2 * gmlp_w
    cb_blk = off_cb // conv_w
    q_blk = (off_cb + 3 * conv_w) // xattn_w
    last = n // sub - 1
    tsb = ts // sub

    def prev_map(col):
        return lambda i: (jnp.maximum(i * tsb - 1, 0), col)

    def next_map(col):
        return lambda i: (jnp.minimum((i + 1) * tsb, last), col)

    kern = functools.partial(_branch_kernel, seq=seq, chunk=chunk, groups=groups, heads=XATTN_HEADS)
    return pl.pallas_call(
        kern,
        grid=(n // ts,),
        in_specs=[
            pl.BlockSpec((ts, gmlp_w), lambda i: (i, 0)),
            pl.BlockSpec((ts, gmlp_w), lambda i: (i, 1)),
            pl.BlockSpec((ts, conv_w), lambda i: (i, cb_blk)),
            pl.BlockSpec((ts, conv_w), lambda i: (i, cb_blk + 1)),
            pl.BlockSpec((ts, conv_w), lambda i: (i, cb_blk + 2)),
            pl.BlockSpec((ts, xattn_w), lambda i: (i, q_blk)),
            pl.BlockSpec((sub, conv_w), prev_map(cb_blk + 1)),
            pl.BlockSpec((sub, conv_w), prev_map(cb_blk + 2)),
            pl.BlockSpec((sub, conv_w), next_map(cb_blk + 1)),
            pl.BlockSpec((sub, conv_w), next_map(cb_blk + 2)),
            pl.BlockSpec((n_mem, 2 * xattn_w), lambda i: ((i * ts) // seq, 0)),
            pl.BlockSpec(wsp.shape, lambda i: (0, 0, 0)),
            pl.BlockSpec(bsp.shape, lambda i: (0, 0, 0)),
            pl.BlockSpec((1, gmlp_w), lambda i: (0, 0)),
            pl.BlockSpec(cw.shape, lambda i: (0, 0)),
        ],
        out_specs=pl.BlockSpec((ts, gmlp_w + conv_w + xattn_w), lambda i: (i, 0)),
        out_shape=jax.ShapeDtypeStruct((n, gmlp_w + conv_w + xattn_w), BF16),
        compiler_params=_params("parallel"),
        name="branches",
    )(proj, proj, proj, proj, proj, proj, proj, proj, proj, proj, kv, wsp, bsp, gg, cw)


def _merge_kernel(y_ref, w_ref, ga_ref, gb_ref, gx_ref, o_ref, *, gmlp_w, conv_w):
    a = jnp.dot(y_ref[:, :gmlp_w], w_ref[:gmlp_w, :], preferred_element_type=F32)
    b = jnp.dot(y_ref[:, gmlp_w:gmlp_w + conv_w], w_ref[gmlp_w:gmlp_w + conv_w, :],
                preferred_element_type=F32)
    x = jnp.dot(y_ref[:, gmlp_w + conv_w:], w_ref[gmlp_w + conv_w:, :], preferred_element_type=F32)
    o_ref[...] = (ga_ref[...] * a + gb_ref[...] * b + gx_ref[...] * x).astype(o_ref.dtype)


def _merge(ycat, w_br, proj, *, tm, tn, off_g, d, gmlp_w, conv_w):
    n, k = ycat.shape
    g0 = off_g // tn
    gstep = d // tn
    kern = functools.partial(_merge_kernel, gmlp_w=gmlp_w, conv_w=conv_w)
    return pl.pallas_call(
        kern,
        grid=(n // tm, d // tn),
        in_specs=[pl.BlockSpec((tm, k), lambda i, j: (i, 0)),
                  pl.BlockSpec((k, tn), lambda i, j: (0, j)),
                  pl.BlockSpec((tm, tn), lambda i, j: (i, g0 + j)),
                  pl.BlockSpec((tm, tn), lambda i, j: (i, g0 + gstep + j)),
                  pl.BlockSpec((tm, tn), lambda i, j: (i, g0 + 2 * gstep + j))],
        out_specs=pl.BlockSpec((tm, tn), lambda i, j: (i, j)),
        out_shape=jax.ShapeDtypeStruct((n, d), BF16),
        compiler_params=_params("parallel", "arbitrary"),
        name="merge",
    )(ycat, w_br, proj, proj, proj)


def _outproj_kernel(m_ref, w_ref, x_ref, o_ref):
    o_ref[...] = x_ref[...] + jnp.dot(m_ref[...], w_ref[...], preferred_element_type=F32)


def _outproj(merged, w_out, x, tm, tn):
    n, d = x.shape
    return pl.pallas_call(
        _outproj_kernel,
        grid=(n // tm, d // tn),
        in_specs=[pl.BlockSpec((tm, d), lambda i, j: (i, 0)),
                  pl.BlockSpec((d, tn), lambda i, j: (0, j)),
                  pl.BlockSpec((tm, tn), lambda i, j: (i, j))],
        out_specs=pl.BlockSpec((tm, tn), lambda i, j: (i, j)),
        out_shape=jax.ShapeDtypeStruct((n, d), F32),
        compiler_params=_params("parallel", "arbitrary"),
        name="out_proj",
    )(merged, w_out, x)


def _router_kernel(x_ref, g_ref, wr_ref, wrt_ref, h_ref, aff_ref, afft_ref):
    h = (_rms_scale(x_ref[...]) * g_ref[...]).astype(BF16)
    h_ref[...] = h
    lt = lax.dot_general(wrt_ref[...], h, (((1,), (1,)), ((), ())), preferred_element_type=F32)
    et = jnp.exp(lt - jnp.max(lt, axis=0, keepdims=True))
    afft_ref[...] = et / jnp.sum(et, axis=0, keepdims=True)
    l = jnp.dot(h, wr_ref[...], preferred_element_type=F32)
    e = jnp.exp(l - jnp.max(l, axis=-1, keepdims=True))
    aff_ref[...] = e / jnp.sum(e, axis=-1, keepdims=True)


def _router(x2, g, wr, tm):
    n, d = x2.shape
    ne = wr.shape[1]
    return pl.pallas_call(
        _router_kernel,
        grid=(n // tm,),
        in_specs=[pl.BlockSpec((tm, d), lambda i: (i, 0)),
                  pl.BlockSpec((1, d), lambda i: (0, 0)),
                  pl.BlockSpec((d, ne), lambda i: (0, 0)),
                  pl.BlockSpec((ne, d), lambda i: (0, 0))],
        out_specs=[pl.BlockSpec((tm, d), lambda i: (i, 0)),
                   pl.BlockSpec((tm, ne), lambda i: (i, 0)),
                   pl.BlockSpec((ne, tm), lambda i: (0, i))],
        out_shape=[jax.ShapeDtypeStruct((n, d), BF16),
                   jax.ShapeDtypeStruct((n, ne), F32),
                   jax.ShapeDtypeStruct((ne, n), F32)],
        compiler_params=_params("parallel"),
        name="ffn_norm_router",
    )(x2, g.reshape(1, d), wr, wr.T)


def _select_kernel(aff_ref, slot_ref, base_ref, *, cap, tb):
    ne, n = aff_ref.shape
    bits = pltpu.bitcast(aff_ref[...], jnp.int32)
    capf = jnp.float32(cap)

    def search(k, t):
        cand = t | lax.shift_left(jnp.int32(1), 30 - k)
        cnt = jnp.sum(jnp.where(bits >= cand, 1.0, 0.0), axis=1, keepdims=True)
        return jnp.where(cnt >= capf, cand, t)

    thr = lax.fori_loop(0, 31, search, jnp.zeros((ne, 1), jnp.int32))
    n_gt = jnp.sum(jnp.where(bits > thr, 1.0, 0.0), axis=1, keepdims=True)
    need = capf - n_gt

    r = lax.broadcasted_iota(jnp.int32, (tb, tb), 0)
    c = lax.broadcasted_iota(jnp.int32, (tb, tb), 1)
    tri = jnp.where(r < c, 1.0, 0.0).astype(BF16)
    lane = lax.broadcasted_iota(jnp.int32, base_ref.shape, 1)
    eq_seen = jnp.zeros((ne, 1), F32)
    sel_seen = jnp.zeros((ne, 1), F32)
    base = jnp.zeros(base_ref.shape, F32)
    for b in range(n // tb):
        blk = pltpu.bitcast(aff_ref[:, b * tb:(b + 1) * tb], jnp.int32)
        eq = jnp.where(blk == thr, 1.0, 0.0)
        eq_rank = jnp.dot(eq.astype(BF16), tri, preferred_element_type=F32) + eq_seen
        sel = jnp.where(blk > thr, 1.0, jnp.where(eq_rank < need, eq, 0.0))
        pos = jnp.dot(sel.astype(BF16), tri, preferred_element_type=F32) + sel_seen
        slot_ref[:, b * tb:(b + 1) * tb] = jnp.where(sel > 0.0, pos, -1.0).astype(jnp.int32)
        base = jnp.where(lane == b, sel_seen, base)
        eq_seen = eq_seen + jnp.sum(eq, axis=1, keepdims=True)
        sel_seen = sel_seen + jnp.sum(sel, axis=1, keepdims=True)
    base_ref[...] = base.astype(jnp.int32)


def _select(afft, cap, tb):
    ne, n = afft.shape
    lanes = 128
    assert n // tb <= lanes
    kern = functools.partial(_select_kernel, cap=cap, tb=tb)
    return pl.pallas_call(
        kern,
        out_shape=[jax.ShapeDtypeStruct((ne, n), jnp.int32),
                   jax.ShapeDtypeStruct((ne, lanes), jnp.int32)],
        compiler_params=pltpu.CompilerParams(vmem_limit_bytes=VMEM_LIMIT_BYTES),
        name="expert_choice_select",
    )(afft)


def _gather_kernel(base_ref, slot_ref, h_ref, xe_ref, *, cap, win, nblk):
    e = pl.program_id(0)
    b = pl.program_id(1)

    @pl.when(b == 0)
    def _():
        xe_ref[...] = jnp.zeros_like(xe_ref)

    tb = h_ref.shape[0]
    base = base_ref[e * nblk + b]
    start = jnp.minimum((base // BF16_ROW_TILE) * BF16_ROW_TILE, cap - win)
    start = pl.multiple_of(start, BF16_ROW_TILE)
    rows = lax.broadcasted_iota(jnp.int32, (win, tb), 0) + start
    onehot = jnp.where(rows == slot_ref[...], 1.0, 0.0).astype(BF16)
    picked = jnp.dot(onehot, h_ref[...], preferred_element_type=F32)
    cur = xe_ref[pl.ds(start, win), :].astype(F32)
    xe_ref[pl.ds(start, win), :] = (cur + picked).astype(xe_ref.dtype)


def _gather(base_flat, slot3, h, *, cap, tb):
    ne = slot3.shape[0]
    n, d = h.shape
    nblk = n // tb
    win = tb + BF16_ROW_TILE
    assert cap >= win and (cap - win) % BF16_ROW_TILE == 0
    kern = functools.partial(_gather_kernel, cap=cap, win=win, nblk=nblk)
    return pl.pallas_call(
        kern,
        grid_spec=pltpu.PrefetchScalarGridSpec(
            num_scalar_prefetch=1,
            grid=(ne, nblk),
            in_specs=[pl.BlockSpec((None, 1, tb), lambda e, b, base: (e, 0, b)),
                      pl.BlockSpec((tb, d), lambda e, b, base: (b, 0))],
            out_specs=pl.BlockSpec((None, cap, d), lambda e, b, base: (e, 0, 0)),
        ),
        out_shape=jax.ShapeDtypeStruct((ne, cap, d), BF16),
        compiler_params=_params("parallel", "arbitrary"),
        name="expert_gather",
    )(base_flat, slot3, h)


def _ffn_kernel(xe_ref, wg_ref, wu_ref, wd_ref, ye_ref, hid_ref, *, nt_up, tf):
    j = pl.program_id(1)

    @pl.when(j < nt_up)
    def _():
        x = xe_ref[...]
        g = jnp.dot(x, wg_ref[...], preferred_element_type=F32)
        u = jnp.dot(x, wu_ref[...], preferred_element_type=F32)
        off = pl.multiple_of(j * tf, tf)
        hid_ref[:, pl.ds(off, tf)] = ((g * _sigmoid(g)) * u).astype(hid_ref.dtype)

    @pl.when(j >= nt_up)
    def _():
        ye_ref[...] = jnp.dot(hid_ref[...], wd_ref[...],
                              preferred_element_type=F32).astype(ye_ref.dtype)


def _ffn(xe, wg, wu, wd, tf, tn):
    ne, cap, d = xe.shape
    f = wg.shape[2]
    nt_up = f // tf
    nt_dn = d // tn
    kern = functools.partial(_ffn_kernel, nt_up=nt_up, tf=tf)
    return pl.pallas_call(
        kern,
        grid=(ne, nt_up + nt_dn),
        in_specs=[pl.BlockSpec((None, cap, d), lambda e, j: (e, 0, 0)),
                  pl.BlockSpec((None, d, tf), lambda e, j: (e, 0, jnp.minimum(j, nt_up - 1))),
                  pl.BlockSpec((None, d, tf), lambda e, j: (e, 0, jnp.minimum(j, nt_up - 1))),
                  pl.BlockSpec((None, f, tn), lambda e, j: (e, 0, jnp.maximum(j - nt_up, 0)))],
        out_specs=pl.BlockSpec((None, cap, tn), lambda e, j: (e, 0, jnp.maximum(j - nt_up, 0))),
        out_shape=jax.ShapeDtypeStruct((ne, cap, d), BF16),
        scratch_shapes=[pltpu.VMEM((cap, f), BF16)],
        compiler_params=_params("parallel", "arbitrary"),
        name="expert_ffn",
    )(xe, wg, wu, wd)


def _combine_kernel(base_ref, slot_ref, aff_ref, x_ref, ya_ref, yb_ref, g_ref, o_ref, *, cap, nblk):
    b = pl.program_id(0)
    e = pl.program_id(1)
    ne = pl.num_programs(1)

    @pl.when(e == 0)
    def _():
        o_ref[...] = x_ref[...]

    tb = x_ref.shape[0]
    blk = ya_ref.shape[0]
    start = jnp.minimum(base_ref[e * nblk + b] // blk, cap // blk - 1) * blk
    lane = lax.broadcasted_iota(jnp.int32, slot_ref.shape, 1)
    slot = jnp.sum(jnp.where(lane == e, slot_ref[...], 0), axis=1, keepdims=True)
    gate = jnp.sum(jnp.where(lane == e, aff_ref[...], 0.0), axis=1, keepdims=True)
    cols = lax.broadcasted_iota(jnp.int32, (tb, blk), 1) + start
    oh_a = jnp.where(cols == slot, 1.0, 0.0).astype(BF16)
    oh_b = jnp.where(cols + blk == slot, 1.0, 0.0).astype(BF16)
    picked = (jnp.dot(oh_a, ya_ref[...], preferred_element_type=F32)
              + jnp.dot(oh_b, yb_ref[...], preferred_element_type=F32))
    o_ref[...] += gate * picked

    @pl.when(e == ne - 1)
    def _():
        o_ref[...] = _rms_scale(o_ref[...]) * g_ref[...]


def _combine(base_flat, slot_tok, aff_tok, x2, ye, g_final, *, cap, tb):
    n, d = x2.shape
    ne = ye.shape[0]
    nblk = n // tb
    blk = tb
    last = cap // blk - 1

    def ya_map(b, e, base):
        return (e, jnp.minimum(base[e * nblk + b] // blk, last), 0)

    def yb_map(b, e, base):
        return (e, jnp.minimum(jnp.minimum(base[e * nblk + b] // blk, last) + 1, last), 0)

    kern = functools.partial(_combine_kernel, cap=cap, nblk=nblk)
    return pl.pallas_call(
        kern,
        grid_spec=pltpu.PrefetchScalarGridSpec(
            num_scalar_prefetch=1,
            grid=(nblk, ne),
            in_specs=[pl.BlockSpec((tb, ne), lambda b, e, base: (b, 0)),
                      pl.BlockSpec((tb, ne), lambda b, e, base: (b, 0)),
                      pl.BlockSpec((tb, d), lambda b, e, base: (b, 0)),
                      pl.BlockSpec((None, blk, d), ya_map),
                      pl.BlockSpec((None, blk, d), yb_map),
                      pl.BlockSpec((1, d), lambda b, e, base: (0, 0))],
            out_specs=pl.BlockSpec((tb, d), lambda b, e, base: (b, 0)),
        ),
        out_shape=jax.ShapeDtypeStruct((n, d), F32),
        compiler_params=_params("parallel", "arbitrary"),
        name="expert_combine_final_norm",
    )(base_flat, slot_tok, aff_tok, x2, ye, ye, g_final.reshape(1, d))


def _encode(x3, mem3, p):
    bsz, seq, d = x3.shape
    n = bsz * seq
    n_mem = mem3.shape[1]
    x = x3.reshape(n, d)
    mem = mem3.reshape(bsz * n_mem, d)

    gmlp_w = p["gmlp_norm_g"].shape[0]
    conv_w = p["conv_w"].shape[1]
    xattn_w = p["w_mem_kv"].shape[1] // 2
    off_cb = 2 * gmlp_w
    off_g = off_cb + 3 * conv_w + xattn_w
    ne = p["w_router"].shape[1]
    cap = CAPACITY_FACTOR * n // ne

    tm = min(1024, n)
    tn = min(512, d)
    ts = min(256, seq)
    tb = 256 if cap >= 256 + BF16_ROW_TILE else 128

    h = _rmsnorm_cast(x, p["norm_mix_g"], ts)
    proj = _proj(h, p["w_in"], tm, tn, off_cb, off_g, F32)
    m = _rmsnorm_cast(mem, p["mem_norm_g"], n_mem)
    kv = _proj(m, p["w_mem_kv"], n_mem, min(tn, 2 * xattn_w), 0, 2 * xattn_w, BF16)
    ycat = _branches(proj, kv, p["w_spatial"], p["b_spatial"], p["gmlp_norm_g"].reshape(1, gmlp_w),
                     p["conv_w"], seq=seq, ts=ts, gmlp_w=gmlp_w, conv_w=conv_w, xattn_w=xattn_w,
                     n_mem=n_mem)
    merged = _merge(ycat, p["w_branch"], proj, tm=tm, tn=tn, off_g=off_g, d=d,
                    gmlp_w=gmlp_w, conv_w=conv_w)
    x2 = _outproj(merged, p["w_out"], x, tm, tn)

    h2, aff_tok, afft = _router(x2, p["norm_ffn_g"], p["w_router"], ts)
    slot, base = _select(afft, cap, tb)
    nblk = n // tb
    base_flat = base[:, :nblk].reshape(ne * nblk)
    xe = _gather(base_flat, slot.reshape(ne, 1, n), h2, cap=cap, tb=tb)
    ye = _ffn(xe, p["w_gate"], p["w_up"], p["w_down"], min(256, d), tn)
    y = _combine(base_flat, slot.T, aff_tok, x2, ye, p["final_norm_g"], cap=cap, tb=tb)
    return y.reshape(bsz, seq, d)


def kernel(x_prompt, x_sample, mem_prompt, mem_sample, norm_mix_g, w_in, gmlp_norm_g, w_spatial,
           b_spatial, conv_w, mem_norm_g, w_mem_kv, w_branch_a, w_branch_b, w_branch_x, w_out,
           norm_ffn_g, w_router, w_gate, w_up, w_down, final_norm_g):
    assert w_in.shape[0] == 1, "one layer"
    groups, chunk = b_spatial.shape[1], b_spatial.shape[2]
    gc = gmlp_norm_g.shape[1] // groups
    p = {
        "norm_mix_g": norm_mix_g[0],
        "w_in": w_in[0].astype(BF16),
        "gmlp_norm_g": gmlp_norm_g[0],
        "w_spatial": w_spatial[0].astype(BF16),
        "b_spatial": jnp.broadcast_to(b_spatial[0][:, :, None], (groups, chunk, gc)),
        "conv_w": conv_w[0],
        "mem_norm_g": mem_norm_g[0],
        "w_mem_kv": w_mem_kv[0].astype(BF16),
        "w_branch": jnp.concatenate([w_branch_a[0], w_branch_b[0], w_branch_x[0]],
                                    axis=0).astype(BF16),
        "w_out": w_out[0].astype(BF16),
        "norm_ffn_g": norm_ffn_g[0],
        "w_router": w_router[0].astype(BF16),
        "w_gate": w_gate[0].astype(BF16),
        "w_up": w_up[0].astype(BF16),
        "w_down": w_down[0].astype(BF16),
        "final_norm_g": final_norm_g,
    }
    return (_encode(x_prompt, mem_prompt, p), _encode(x_sample, mem_sample, p))
```

```python
import functools

import jax
import jax.numpy as jnp
from jax import lax
from jax.experimental import pallas as pl
from jax.experimental.pallas import tpu as pltpu

EPS = 1e-6
XATTN_HEADS = 4
CAPACITY_FACTOR = 2
BF16_ROW_TILE = 16
GATHER_CHUNK_ROWS = 80
COMBINE_WINDOW_ROWS = 128
VMEM_LIMIT_BYTES = 56 * 1024 * 1024

F32 = jnp.float32
BF16 = jnp.bfloat16


def _params(*semantics):
    return pltpu.CompilerParams(dimension_semantics=semantics,
                                vmem_limit_bytes=VMEM_LIMIT_BYTES)


def _gelu_tanh(x):
    return 0.5 * x * (1.0 + jnp.tanh(0.7978845608028654 * (x + 0.044715 * (x * x * x))))


def _sigmoid(x):
    return 1.0 / (1.0 + jnp.exp(-x))


def _rms_scale(x):
    return x * lax.rsqrt(jnp.mean(x * x, axis=-1, keepdims=True) + EPS)


def _rmsnorm_cast_kernel(x_ref, g_ref, o_ref):
    o_ref[...] = (_rms_scale(x_ref[...]) * g_ref[...]).astype(o_ref.dtype)


def _rmsnorm_cast(x, g, tm):
    n, d = x.shape
    return pl.pallas_call(
        _rmsnorm_cast_kernel,
        grid=(n // tm,),
        in_specs=[pl.BlockSpec((tm, d), lambda i: (i, 0)),
                  pl.BlockSpec((1, d), lambda i: (0, 0))],
        out_specs=pl.BlockSpec((tm, d), lambda i: (i, 0)),
        out_shape=jax.ShapeDtypeStruct((n, d), BF16),
        compiler_params=_params("parallel"),
        name="rmsnorm_cast",
    )(x, g.reshape(1, d))


def _proj_kernel(h_ref, w_ref, o_ref, *, gelu_tiles, sigmoid_from):
    j = pl.program_id(1)
    acc = jnp.dot(h_ref[...], w_ref[...], preferred_element_type=F32)

    @pl.when(j < gelu_tiles)
    def _():
        o_ref[...] = _gelu_tanh(acc).astype(o_ref.dtype)

    @pl.when(jnp.logical_and(j >= gelu_tiles, j < sigmoid_from))
    def _():
        o_ref[...] = acc.astype(o_ref.dtype)

    @pl.when(j >= sigmoid_from)
    def _():
        o_ref[...] = _sigmoid(acc).astype(o_ref.dtype)


def _proj(h, w, tm, tn, gelu_cols, sigmoid_from_col, out_dtype):
    n, d = h.shape
    cols = w.shape[1]
    kern = functools.partial(_proj_kernel, gelu_tiles=gelu_cols // tn,
                             sigmoid_from=sigmoid_from_col // tn)
    return pl.pallas_call(
        kern,
        grid=(n // tm, cols // tn),
        in_specs=[pl.BlockSpec((tm, d), lambda i, j: (i, 0)),
                  pl.BlockSpec((d, tn), lambda i, j: (0, j))],
        out_specs=pl.BlockSpec((tm, tn), lambda i, j: (i, j)),
        out_shape=jax.ShapeDtypeStruct((n, cols), out_dtype),
        compiler_params=_params("parallel", "arbitrary"),
        name="in_proj",
    )(h, w)


def _branch_kernel(u_ref, v_ref, cb_ref, cc_ref, ch_ref, q_ref, ccp_ref, chp_ref, ccn_ref, chn_ref,
                   kv_ref, wsp_ref, bsp_ref, gg_ref, cw_ref, y_ref, *, seq, chunk, groups, heads):
    i = pl.program_id(0)
    ts = u_ref.shape[0]
    gw = u_ref.shape[1]
    gc = gw // groups
    cw = cb_ref.shape[1]
    xw = q_ref.shape[1]
    hd = xw // heads

    vn = (_rms_scale(v_ref[...]) * gg_ref[...]).astype(BF16)
    for c in range(ts // chunk):
        rows = slice(c * chunk, (c + 1) * chunk)
        for g in range(groups):
            cols = slice(g * gc, (g + 1) * gc)
            mixed = jnp.dot(wsp_ref[g], vn[rows, cols], preferred_element_type=F32) + bsp_ref[g]
            y_ref[rows, cols] = (u_ref[rows, cols] * mixed).astype(y_ref.dtype)

    pos = (i * ts) % seq
    z = cc_ref[...] * ch_ref[...]
    sub = ccp_ref.shape[0]
    z_before = jnp.where(pos == 0, 0.0, ccp_ref[sub - 1:sub, :] * chp_ref[sub - 1:sub, :])
    z_after = jnp.where(pos + ts == seq, 0.0, ccn_ref[0:1, :] * chn_ref[0:1, :])
    row = lax.broadcasted_iota(jnp.int32, z.shape, 0)
    z_prev = jnp.where(row == 0, z_before, pltpu.roll(z, 1, 0))
    z_next = jnp.where(row == ts - 1, z_after, pltpu.roll(z, ts - 1, 0))
    conv = z_prev * cw_ref[0:1, :] + z * cw_ref[1:2, :] + z_next * cw_ref[2:3, :]
    y_ref[:, gw:gw + cw] = (cb_ref[...] * conv).astype(y_ref.dtype)

    scale = hd ** -0.5
    for h in range(heads):
        cols = slice(h * hd, (h + 1) * hd)
        qh = q_ref[:, cols].astype(BF16)
        kh = kv_ref[:, cols]
        vh = kv_ref[:, xw + h * hd: xw + (h + 1) * hd]
        s = lax.dot_general(qh, kh, (((1,), (1,)), ((), ())), preferred_element_type=F32) * scale
        e = jnp.exp(s - jnp.max(s, axis=-1, keepdims=True))
        p = (e / jnp.sum(e, axis=-1, keepdims=True)).astype(BF16)
        yh = jnp.dot(p, vh, preferred_element_type=F32)
        y_ref[:, gw + cw + h * hd: gw + cw + (h + 1) * hd] = yh.astype(y_ref.dtype)


def _branches(proj, kv, wsp, bsp, gg, cw, *, seq, ts, gmlp_w, conv_w, xattn_w, n_mem):
    n = proj.shape[0]
    chunk = wsp.shape[1]
    groups = wsp.shape[0]
    sub = 8
    off_cb = 2 * gmlp_w
    cb_blk = off_cb // conv_w
    q_blk = (off_cb + 3 * conv_w) // xattn_w
    last = n // sub - 1
    tsb = ts // sub

    def prev_map(col):
        return lambda i: (jnp.maximum(i * tsb - 1, 0), col)

    def next_map(col):
        return lambda i: (jnp.minimum((i + 1) * tsb, last), col)

    kern = functools.partial(_branch_kernel, seq=seq, chunk=chunk, groups=groups, heads=XATTN_HEADS)
    return pl.pallas_call(
        kern,
        grid=(n // ts,),
        in_specs=[
            pl.BlockSpec((ts, gmlp_w), lambda i: (i, 0)),
            pl.BlockSpec((ts, gmlp_w), lambda i: (i, 1)),
            pl.BlockSpec((ts, conv_w), lambda i: (i, cb_blk)),
            pl.BlockSpec((ts, conv_w), lambda i: (i, cb_blk + 1)),
            pl.BlockSpec((ts, conv_w), lambda i: (i, cb_blk + 2)),
            pl.BlockSpec((ts, xattn_w), lambda i: (i, q_blk)),
            pl.BlockSpec((sub, conv_w), prev_map(cb_blk + 1)),
            pl.BlockSpec((sub, conv_w), prev_map(cb_blk + 2)),
            pl.BlockSpec((sub, conv_w), next_map(cb_blk + 1)),
            pl.BlockSpec((sub, conv_w), next_map(cb_blk + 2)),
            pl.BlockSpec((n_mem, 2 * xattn_w), lambda i: ((i * ts) // seq, 0)),
            pl.BlockSpec(wsp.shape, lambda i: (0, 0, 0)),
            pl.BlockSpec(bsp.shape, lambda i: (0, 0, 0)),
            pl.BlockSpec((1, gmlp_w), lambda i: (0, 0)),
            pl.BlockSpec(cw.shape, lambda i: (0, 0)),
        ],
        out_specs=pl.BlockSpec((ts, gmlp_w + conv_w + xattn_w), lambda i: (i, 0)),
        out_shape=jax.ShapeDtypeStruct((n, gmlp_w + conv_w + xattn_w), BF16),
        compiler_params=_params("parallel"),
        name="branches",
    )(proj, proj, proj, proj, proj, proj, proj, proj, proj, proj, kv, wsp, bsp, gg, cw)


def _merge_kernel(y_ref, w_ref, ga_ref, gb_ref, gx_ref, o_ref, *, gmlp_w, conv_w):
    a = jnp.dot(y_ref[:, :gmlp_w], w_ref[:gmlp_w, :], preferred_element_type=F32)
    b = jnp.dot(y_ref[:, gmlp_w:gmlp_w + conv_w], w_ref[gmlp_w:gmlp_w + conv_w, :],
                preferred_element_type=F32)
    x = jnp.dot(y_ref[:, gmlp_w + conv_w:], w_ref[gmlp_w + conv_w:, :], preferred_element_type=F32)
    o_ref[...] = (ga_ref[...] * a + gb_ref[...] * b + gx_ref[...] * x).astype(o_ref.dtype)


def _merge(ycat, w_br, proj, *, tm, tn, off_g, d, gmlp_w, conv_w):
    n, k = ycat.shape
    g0 = off_g // tn
    gstep = d // tn
    kern = functools.partial(_merge_kernel, gmlp_w=gmlp_w, conv_w=conv_w)
    return pl.pallas_call(
        kern,
        grid=(n // tm, d // tn),
        in_specs=[pl.BlockSpec((tm, k), lambda i, j: (i, 0)),
                  pl.BlockSpec((k, tn), lambda i, j: (0, j)),
                  pl.BlockSpec((tm, tn), lambda i, j: (i, g0 + j)),
                  pl.BlockSpec((tm, tn), lambda i, j: (i, g0 + gstep + j)),
                  pl.BlockSpec((tm, tn), lambda i, j: (i, g0 + 2 * gstep + j))],
        out_specs=pl.BlockSpec((tm, tn), lambda i, j: (i, j)),
        out_shape=jax.ShapeDtypeStruct((n, d), BF16),
        compiler_params=_params("parallel", "arbitrary"),
        name="merge",
    )(ycat, w_br, proj, proj, proj)


def _outproj_kernel(m_ref, w_ref, x_ref, o_ref):
    o_ref[...] = x_ref[...] + jnp.dot(m_ref[...], w_ref[...], preferred_element_type=F32)


def _outproj(merged, w_out, x, tm, tn):
    n, d = x.shape
    return pl.pallas_call(
        _outproj_kernel,
        grid=(n // tm, d // tn),
        in_specs=[pl.BlockSpec((tm, d), lambda i, j: (i, 0)),
                  pl.BlockSpec((d, tn), lambda i, j: (0, j)),
                  pl.BlockSpec((tm, tn), lambda i, j: (i, j))],
        out_specs=pl.BlockSpec((tm, tn), lambda i, j: (i, j)),
        out_shape=jax.ShapeDtypeStruct((n, d), F32),
        compiler_params=_params("parallel", "arbitrary"),
        name="out_proj",
    )(merged, w_out, x)


def _router_kernel(x_ref, g_ref, wr_ref, wrt_ref, h_ref, aff_ref, afft_ref):
    h = (_rms_scale(x_ref[...]) * g_ref[...]).astype(BF16)
    h_ref[...] = h
    lt = lax.dot_general(wrt_ref[...], h, (((1,), (1,)), ((), ())), preferred_element_type=F32)
    et = jnp.exp(lt - jnp.max(lt, axis=0, keepdims=True))
    afft_ref[...] = et / jnp.sum(et, axis=0, keepdims=True)
    l = jnp.dot(h, wr_ref[...], preferred_element_type=F32)
    e = jnp.exp(l - jnp.max(l, axis=-1, keepdims=True))
    aff_ref[...] = e / jnp.sum(e, axis=-1, keepdims=True)


def _router(x2, g, wr, tm):
    n, d = x2.shape
    ne = wr.shape[1]
    return pl.pallas_call(
        _router_kernel,
        grid=(n // tm,),
        in_specs=[pl.BlockSpec((tm, d), lambda i: (i, 0)),
                  pl.BlockSpec((1, d), lambda i: (0, 0)),
                  pl.BlockSpec((d, ne), lambda i: (0, 0)),
                  pl.BlockSpec((ne, d), lambda i: (0, 0))],
        out_specs=[pl.BlockSpec((tm, d), lambda i: (i, 0)),
                   pl.BlockSpec((tm, ne), lambda i: (i, 0)),
                   pl.BlockSpec((ne, tm), lambda i: (0, i))],
        out_shape=[jax.ShapeDtypeStruct((n, d), BF16),
                   jax.ShapeDtypeStruct((n, ne), F32),
                   jax.ShapeDtypeStruct((ne, n), F32)],
        compiler_params=_params("parallel"),
        name="ffn_norm_router",
    )(x2, g.reshape(1, d), wr, wr.T)


def _select_kernel(aff_ref, slot_ref, base_ref, *, cap, tb):
    ne, n = aff_ref.shape
    bits = pltpu.bitcast(aff_ref[...], jnp.int32)
    capf = jnp.float32(cap)

    def search(k, t):
        cand = t | lax.shift_left(jnp.int32(1), 30 - k)
        cnt = jnp.sum(jnp.where(bits >= cand, 1.0, 0.0), axis=1, keepdims=True)
        return jnp.where(cnt >= capf, cand, t)

    thr = lax.fori_loop(0, 31, search, jnp.zeros((ne, 1), jnp.int32))
    n_gt = jnp.sum(jnp.where(bits > thr, 1.0, 0.0), axis=1, keepdims=True)
    need = capf - n_gt

    r = lax.broadcasted_iota(jnp.int32, (tb, tb), 0)
    c = lax.broadcasted_iota(jnp.int32, (tb, tb), 1)
    tri = jnp.where(r < c, 1.0, 0.0).astype(BF16)
    lane = lax.broadcasted_iota(jnp.int32, base_ref.shape, 1)
    eq_seen = jnp.zeros((ne, 1), F32)
    sel_seen = jnp.zeros((ne, 1), F32)
    base = jnp.zeros(base_ref.shape, F32)
    for b in range(n // tb):
        blk = pltpu.bitcast(aff_ref[:, b * tb:(b + 1) * tb], jnp.int32)
        eq = jnp.where(blk == thr, 1.0, 0.0)
        eq_rank = jnp.dot(eq.astype(BF16), tri, preferred_element_type=F32) + eq_seen
        sel = jnp.where(blk > thr, 1.0, jnp.where(eq_rank < need, eq, 0.0))
        pos = jnp.dot(sel.astype(BF16), tri, preferred_element_type=F32) + sel_seen
        slot_ref[:, b * tb:(b + 1) * tb] = jnp.where(sel > 0.0, pos, -1.0).astype(jnp.int32)
        base = jnp.where(lane == b, sel_seen, base)
        eq_seen = eq_seen + jnp.sum(eq, axis=1, keepdims=True)
        sel_seen = sel_seen + jnp.sum(sel, axis=1, keepdims=True)
    base = jnp.where(lane == n // tb, sel_seen, base)
    base_ref[...] = base.astype(jnp.int32)


def _select(afft, cap, tb):
    ne, n = afft.shape
    lanes = 128
    assert n // tb < lanes
    kern = functools.partial(_select_kernel, cap=cap, tb=tb)
    return pl.pallas_call(
        kern,
        out_shape=[jax.ShapeDtypeStruct((ne, n), jnp.int32),
                   jax.ShapeDtypeStruct((ne, lanes), jnp.int32)],
        compiler_params=pltpu.CompilerParams(vmem_limit_bytes=VMEM_LIMIT_BYTES),
        name="expert_choice_select",
    )(afft)


def _gather_kernel(base_ref, slot_ref, aff_ref, h_ref, xe_ref, gw_ref, *, cap, nblk, chunk):
    eb = pl.program_id(0)
    b = pl.program_id(1)

    @pl.when(b == 0)
    def _():
        xe_ref[...] = jnp.zeros_like(xe_ref)
        gw_ref[...] = jnp.zeros_like(gw_ref)

    tb = h_ref.shape[0]
    for k in range(xe_ref.shape[0]):
        e = eb * xe_ref.shape[0] + k
        first = (base_ref[e * (nblk + 1) + b] // BF16_ROW_TILE) * BF16_ROW_TILE
        end = base_ref[e * (nblk + 1) + b + 1]
        slot = slot_ref[k]
        aff = aff_ref[k]

        def body(c, carry, k=k, first=first, slot=slot, aff=aff):
            lo = first + c * chunk
            st = pl.multiple_of(jnp.minimum(lo, cap - chunk), BF16_ROW_TILE)
            rows = lax.broadcasted_iota(jnp.int32, (chunk, tb), 0) + st
            hit = jnp.where(rows == slot, jnp.where(rows >= lo, 1.0, 0.0), 0.0)
            picked = jnp.dot(hit.astype(BF16), h_ref[...], preferred_element_type=F32)
            cur = xe_ref[k, pl.ds(st, chunk), :].astype(F32)
            xe_ref[k, pl.ds(st, chunk), :] = (cur + picked).astype(xe_ref.dtype)
            gw_ref[k, pl.ds(st, chunk), :] += jnp.sum(hit * aff, axis=1, keepdims=True)
            return carry

        lax.fori_loop(0, (end - first + chunk - 1) // chunk, body, 0)


def _gather(base_flat, slot3, aff3, h, *, cap, tb, chunk, experts_per_step):
    ne = slot3.shape[0]
    n, d = h.shape
    nblk = n // tb
    eps = experts_per_step
    assert cap >= chunk and (cap - chunk) % BF16_ROW_TILE == 0 and chunk % BF16_ROW_TILE == 0
    kern = functools.partial(_gather_kernel, cap=cap, nblk=nblk, chunk=chunk)
    return pl.pallas_call(
        kern,
        grid_spec=pltpu.PrefetchScalarGridSpec(
            num_scalar_prefetch=1,
            grid=(ne // eps, nblk),
            in_specs=[pl.BlockSpec((eps, 1, tb), lambda e, b, base: (e, 0, b)),
                      pl.BlockSpec((eps, 1, tb), lambda e, b, base: (e, 0, b)),
                      pl.BlockSpec((tb, d), lambda e, b, base: (b, 0))],
            out_specs=[pl.BlockSpec((eps, cap, d), lambda e, b, base: (e, 0, 0)),
                       pl.BlockSpec((eps, cap, 1), lambda e, b, base: (e, 0, 0))],
        ),
        out_shape=[jax.ShapeDtypeStruct((ne, cap, d), BF16),
                   jax.ShapeDtypeStruct((ne, cap, 1), F32)],
        compiler_params=_params("parallel", "arbitrary"),
        name="expert_gather",
    )(base_flat, slot3, aff3, h)


def _ffn_kernel(xe_ref, gw_ref, wg_ref, wu_ref, wd_ref, ye_ref, hid_ref, *, nt_up, tf):
    j = pl.program_id(1)

    @pl.when(j < nt_up)
    def _():
        x = xe_ref[...]
        g = jnp.dot(x, wg_ref[...].astype(BF16), preferred_element_type=F32)
        u = jnp.dot(x, wu_ref[...].astype(BF16), preferred_element_type=F32)
        off = pl.multiple_of(j * tf, tf)
        hid_ref[:, pl.ds(off, tf)] = ((g * _sigmoid(g)) * u).astype(hid_ref.dtype)

    @pl.when(j >= nt_up)
    def _():
        y = jnp.dot(hid_ref[...], wd_ref[...].astype(BF16), preferred_element_type=F32)
        ye_ref[...] = (y * gw_ref[...]).astype(ye_ref.dtype)


def _ffn(xe, gw, wg, wu, wd, tf, tn):
    ne, cap, d = xe.shape
    f = wg.shape[2]
    nt_up = f // tf
    nt_dn = d // tn
    kern = functools.partial(_ffn_kernel, nt_up=nt_up, tf=tf)
    return pl.pallas_call(
        kern,
        grid=(ne, nt_up + nt_dn),
        in_specs=[pl.BlockSpec((None, cap, d), lambda e, j: (e, 0, 0)),
                  pl.BlockSpec((None, cap, 1), lambda e, j: (e, 0, 0)),
                  pl.BlockSpec((None, d, tf), lambda e, j: (e, 0, jnp.minimum(j, nt_up - 1))),
                  pl.BlockSpec((None, d, tf), lambda e, j: (e, 0, jnp.minimum(j, nt_up - 1))),
                  pl.BlockSpec((None, f, tn), lambda e, j: (e, 0, jnp.maximum(j - nt_up, 0)))],
        out_specs=pl.BlockSpec((None, cap, tn), lambda e, j: (e, 0, jnp.maximum(j - nt_up, 0))),
        out_shape=jax.ShapeDtypeStruct((ne, cap, d), BF16),
        scratch_shapes=[pltpu.VMEM((cap, f), BF16)],
        compiler_params=_params("parallel", "arbitrary"),
        name="expert_ffn",
    )(xe, gw, wg, wu, wd)


def _expert_column(ref, e):
    lane = lax.broadcasted_iota(jnp.int32, ref.shape, 1)
    return jnp.sum(jnp.where(lane == e, ref[...], 0), axis=1, keepdims=True)


def _combine_any_kernel(base_ref, slot_ref, x_ref, ya_ref, yb_ref, g_ref, o_ref, *, cap, nblk):
    b = pl.program_id(0)
    e = pl.program_id(1)
    ne = pl.num_programs(1)

    @pl.when(e == 0)
    def _():
        o_ref[...] = x_ref[...]

    tb = x_ref.shape[0]
    blk = ya_ref.shape[0]
    start = jnp.minimum(base_ref[e * (nblk + 1) + b] // blk, cap // blk - 1) * blk
    slot = _expert_column(slot_ref, e)
    cols = lax.broadcasted_iota(jnp.int32, (tb, blk), 1) + start
    oh_a = jnp.where(cols == slot, 1.0, 0.0).astype(BF16)
    oh_b = jnp.where(cols + blk == slot, 1.0, 0.0).astype(BF16)
    o_ref[...] += (jnp.dot(oh_a, ya_ref[...], preferred_element_type=F32)
                   + jnp.dot(oh_b, yb_ref[...], preferred_element_type=F32))

    @pl.when(e == ne - 1)
    def _():
        o_ref[...] = _rms_scale(o_ref[...]) * g_ref[...]


def _combine_any(base_flat, slot_tok, x2, ye, g_final, *, cap, tb):
    n, d = x2.shape
    ne = ye.shape[0]
    nblk = n // tb
    blk = tb
    last = cap // blk - 1

    def ya_map(b, e, base):
        return (e, jnp.minimum(base[e * (nblk + 1) + b] // blk, last), 0)

    def yb_map(b, e, base):
        return (e, jnp.minimum(jnp.minimum(base[e * (nblk + 1) + b] // blk, last) + 1, last), 0)

    kern = functools.partial(_combine_any_kernel, cap=cap, nblk=nblk)
    return pl.pallas_call(
        kern,
        grid_spec=pltpu.PrefetchScalarGridSpec(
            num_scalar_prefetch=1,
            grid=(nblk, ne),
            in_specs=[pl.BlockSpec((tb, ne), lambda b, e, base: (b, 0)),
                      pl.BlockSpec((tb, d), lambda b, e, base: (b, 0)),
                      pl.BlockSpec((None, blk, d), ya_map),
                      pl.BlockSpec((None, blk, d), yb_map),
                      pl.BlockSpec((1, d), lambda b, e, base: (0, 0))],
            out_specs=pl.BlockSpec((tb, d), lambda b, e, base: (b, 0)),
        ),
        out_shape=jax.ShapeDtypeStruct((n, d), F32),
        compiler_params=_params("parallel", "arbitrary"),
        name="expert_combine_any",
    )(base_flat, slot_tok, x2, ye, ye, g_final.reshape(1, d))


def _window_start(base, e, b, *, nblk, cap, win):
    tile = jnp.minimum(base[e * (nblk + 1) + b] // BF16_ROW_TILE, (cap - win) // BF16_ROW_TILE)
    return tile * BF16_ROW_TILE


def _combine_windows_kernel(base_ref, slot_ref, x_ref, g_ref, *rest, cap, nblk, win):
    y_refs, o_ref = rest[:-1], rest[-1]
    b = pl.program_id(0)
    s = pl.program_id(1)
    per = len(y_refs)

    @pl.when(s == 0)
    def _():
        o_ref[...] = x_ref[...]

    tb = x_ref.shape[0]
    col = lax.broadcasted_iota(jnp.int32, (tb, win), 1)
    acc = None
    for k0 in range(0, per, 2):
        onehots = []
        for k in (k0, k0 + 1):
            e = s * per + k
            start = _window_start(base_ref, e, b, nblk=nblk, cap=cap, win=win)
            onehots.append(jnp.where(col + start == _expert_column(slot_ref, e), 1.0, 0.0).astype(BF16))
        y = jnp.concatenate([y_refs[k0][...], y_refs[k0 + 1][...]], axis=0)
        part = jnp.dot(jnp.concatenate(onehots, axis=1), y, preferred_element_type=F32)
        acc = part if acc is None else acc + part
    o_ref[...] += acc

    @pl.when(s == pl.num_programs(1) - 1)
    def _():
        o_ref[...] = _rms_scale(o_ref[...]) * g_ref[...]


def _combine_windows(base_flat, slot_tok, x2, ye, g_final, *, cap, tb, win, experts_per_step):
    n, d = x2.shape
    ne = ye.shape[0]
    nblk = n // tb
    per = experts_per_step
    assert per % 2 == 0 and ne % per == 0

    def y_map(k):
        def index(b, s, base):
            e = s * per + k
            return (e, _window_start(base, e, b, nblk=nblk, cap=cap, win=win), 0)
        return index

    kern = functools.partial(_combine_windows_kernel, cap=cap, nblk=nblk, win=win)
    return pl.pallas_call(
        kern,
        grid_spec=pltpu.PrefetchScalarGridSpec(
            num_scalar_prefetch=1,
            grid=(nblk, ne // per),
            in_specs=[pl.BlockSpec((tb, ne), lambda b, s, base: (b, 0)),
                      pl.BlockSpec((tb, d), lambda b, s, base: (b, 0)),
                      pl.BlockSpec((1, d), lambda b, s, base: (0, 0))]
                     + [pl.BlockSpec((None, pl.Element(win), pl.Element(d)), y_map(k))
                        for k in range(per)],
            out_specs=pl.BlockSpec((tb, d), lambda b, s, base: (b, 0)),
        ),
        out_shape=jax.ShapeDtypeStruct((n, d), F32),
        compiler_params=_params("parallel", "arbitrary"),
        name="expert_combine_windows",
    )(base_flat, slot_tok, x2, g_final.reshape(1, d), *([ye] * per))


def _combine(base, slot_tok, x2, ye, g_final, *, cap, tb, win):
    nblk = x2.shape[0] // tb
    base_flat = base[:, :nblk + 1].reshape(-1)
    first = (base[:, :nblk] // BF16_ROW_TILE) * BF16_ROW_TILE
    fits = jnp.all(base[:, 1:nblk + 1] - jnp.minimum(first, cap - win) <= win)
    return lax.cond(
        fits,
        lambda: _combine_windows(base_flat, slot_tok, x2, ye, g_final, cap=cap, tb=tb, win=win,
                                 experts_per_step=8),
        lambda: _combine_any(base_flat, slot_tok, x2, ye, g_final, cap=cap, tb=tb))


def _encode(x3, mem3, p):
    bsz, seq, d = x3.shape
    n = bsz * seq
    n_mem = mem3.shape[1]
    x = x3.reshape(n, d)
    mem = mem3.reshape(bsz * n_mem, d)

    gmlp_w = p["gmlp_norm_g"].shape[0]
    conv_w = p["conv_w"].shape[1]
    xattn_w = p["w_mem_kv"].shape[1] // 2
    off_cb = 2 * gmlp_w
    off_g = off_cb + 3 * conv_w + xattn_w
    ne = p["w_router"].shape[1]
    cap = CAPACITY_FACTOR * n // ne

    tm = min(1024, n)
    tn = min(512, d)
    ts = min(256, seq)
    tb = 256 if cap >= 256 + BF16_ROW_TILE else 128

    h = _rmsnorm_cast(x, p["norm_mix_g"], ts)
    proj = _proj(h, p["w_in"], tm, tn, off_cb, off_g, F32)
    m = _rmsnorm_cast(mem, p["mem_norm_g"], n_mem)
    kv = _proj(m, p["w_mem_kv"], n_mem, min(tn, 2 * xattn_w), 0, 2 * xattn_w, BF16)
    ycat = _branches(proj, kv, p["w_spatial"], p["b_spatial"], p["gmlp_norm_g"].reshape(1, gmlp_w),
                     p["conv_w"], seq=seq, ts=ts, gmlp_w=gmlp_w, conv_w=conv_w, xattn_w=xattn_w,
                     n_mem=n_mem)
    merged = _merge(ycat, p["w_branch"], proj, tm=tm, tn=tn, off_g=off_g, d=d,
                    gmlp_w=gmlp_w, conv_w=conv_w)
    x2 = _outproj(merged, p["w_out"], x, tm, tn)

    h2, aff_tok, afft = _router(x2, p["norm_ffn_g"], p["w_router"], ts)
    slot, base = _select(afft, cap, tb)
    nblk = n // tb
    base_flat = base[:, :nblk + 1].reshape(-1)
    xe, gw = _gather(base_flat, slot.reshape(ne, 1, n), afft.reshape(ne, 1, n), h2, cap=cap, tb=tb,
                     chunk=GATHER_CHUNK_ROWS, experts_per_step=2)
    ye = _ffn(xe, gw, p["w_gate"], p["w_up"], p["w_down"], min(256, d), min(256, d))
    y = _combine(base, slot.T, x2, ye, p["final_norm_g"], cap=cap, tb=tb,
                 win=min(COMBINE_WINDOW_ROWS, tb))
    return y.reshape(bsz, seq, d)


def kernel(x_prompt, x_sample, mem_prompt, mem_sample, norm_mix_g, w_in, gmlp_norm_g, w_spatial,
           b_spatial, conv_w, mem_norm_g, w_mem_kv, w_branch_a, w_branch_b, w_branch_x, w_out,
           norm_ffn_g, w_router, w_gate, w_up, w_down, final_norm_g):
    assert w_in.shape[0] == 1, "one layer"
    groups, chunk = b_spatial.shape[1], b_spatial.shape[2]
    gc = gmlp_norm_g.shape[1] // groups
    p = {
        "norm_mix_g": norm_mix_g[0],
        "w_in": w_in[0].astype(BF16),
        "gmlp_norm_g": gmlp_norm_g[0],
        "w_spatial": w_spatial[0].astype(BF16),
        "b_spatial": jnp.broadcast_to(b_spatial[0][:, :, None], (groups, chunk, gc)),
        "conv_w": conv_w[0],
        "mem_norm_g": mem_norm_g[0],
        "w_mem_kv": w_mem_kv[0].astype(BF16),
        "w_branch": jnp.concatenate([w_branch_a[0], w_branch_b[0], w_branch_x[0]],
                                    axis=0).astype(BF16),
        "w_out": w_out[0].astype(BF16),
        "norm_ffn_g": norm_ffn_g[0],
        "w_router": w_router[0].astype(BF16),
        "w_gate": w_gate.reshape(w_gate.shape[1:]),
        "w_up": w_up.reshape(w_up.shape[1:]),
        "w_down": w_down.reshape(w_down.shape[1:]),
        "final_norm_g": final_norm_g,
    }
    return (_encode(x_prompt, mem_prompt, p), _encode(x_sample, mem_sample, p))
```

```python
import functools

import jax
import jax.numpy as jnp
from jax import lax
from jax.experimental import pallas as pl
from jax.experimental.pallas import tpu as pltpu

EPS = 1e-6
XATTN_HEADS = 4
CAPACITY_FACTOR = 2
BF16_ROW_TILE = 16
GATHER_CHUNK_ROWS = 80
COMBINE_WINDOW_ROWS = 128
VMEM_LIMIT_BYTES = 56 * 1024 * 1024

F32 = jnp.float32
BF16 = jnp.bfloat16


def _params(*semantics):
    return pltpu.CompilerParams(dimension_semantics=semantics,
                                vmem_limit_bytes=VMEM_LIMIT_BYTES)


GELU_K = 0.7978845608028654
GELU_CUBIC = 0.044715
TANH_SATURATED = 30.0
MXU_COLS = 256


def _sigmoid(x):
    return 1.0 / (1.0 + jnp.exp(-x))


def _rms_scale(x):
    return x * lax.rsqrt(jnp.mean(x * x, axis=-1, keepdims=True) + EPS)


def _rmsnorm_cast_kernel(x_ref, g_ref, o_ref):
    o_ref[...] = (_rms_scale(x_ref[...]) * g_ref[...]).astype(o_ref.dtype)


def _rmsnorm_cast(x, g, tm):
    n, d = x.shape
    return pl.pallas_call(
        _rmsnorm_cast_kernel,
        grid=(n // tm,),
        in_specs=[pl.BlockSpec((tm, d), lambda i: (i, 0)),
                  pl.BlockSpec((1, d), lambda i: (0, 0))],
        out_specs=pl.BlockSpec((tm, d), lambda i: (i, 0)),
        out_shape=jax.ShapeDtypeStruct((n, d), BF16),
        compiler_params=_params("parallel"),
        name="rmsnorm_cast",
    )(x, g.reshape(1, d))


def _proj_kernel(h_ref, w_ref, o_ref, *, sub, gelu_tiles, sigmoid_from):
    j = pl.program_id(1)
    nsub = o_ref.shape[1] // sub
    h = h_ref[...]
    for s in range(nsub):
        c = j * nsub + s
        is_gelu = c < gelu_tiles
        is_sig = c >= sigmoid_from
        q1 = jnp.where(is_gelu, GELU_K, jnp.where(is_sig, 0.5, 0.0))
        q3 = jnp.where(is_gelu, GELU_K * GELU_CUBIC, 0.0)
        q0 = jnp.where(jnp.logical_or(is_gelu, is_sig), 0.0, TANH_SATURATED)
        p1 = jnp.where(is_sig, 0.0, 1.0)
        p0 = jnp.where(is_sig, 1.0, 0.0)
        cols = slice(s * sub, (s + 1) * sub)
        x = jnp.dot(h, w_ref[:, cols], preferred_element_type=F32)
        gate = 0.5 * (1.0 + jnp.tanh(x * (q1 + q3 * (x * x)) + q0))
        o_ref[:, cols] = ((x * p1 + p0) * gate).astype(o_ref.dtype)


def _proj(h, w, tm, tn, gelu_cols, sigmoid_from_col, out_dtype):
    n, d = h.shape
    cols = w.shape[1]
    sub = min(MXU_COLS, tn)
    kern = functools.partial(_proj_kernel, sub=sub, gelu_tiles=gelu_cols // sub,
                             sigmoid_from=sigmoid_from_col // sub)
    return pl.pallas_call(
        kern,
        grid=(n // tm, cols // tn),
        in_specs=[pl.BlockSpec((tm, d), lambda i, j: (i, 0)),
                  pl.BlockSpec((d, tn), lambda i, j: (0, j))],
        out_specs=pl.BlockSpec((tm, tn), lambda i, j: (i, j)),
        out_shape=jax.ShapeDtypeStruct((n, cols), out_dtype),
        compiler_params=_params("parallel", "arbitrary"),
        name="in_proj",
    )(h, w)


def _branch_kernel(u_ref, v_ref, cb_ref, cc_ref, ch_ref, q_ref, ccp_ref, chp_ref, ccn_ref, chn_ref,
                   kv_ref, wsp_ref, bsp_ref, gg_ref, cw_ref, y_ref, *, seq, chunk, groups, heads):
    i = pl.program_id(0)
    ts = u_ref.shape[0]
    gw = u_ref.shape[1]
    gc = gw // groups
    cw = cb_ref.shape[1]
    xw = q_ref.shape[1]
    hd = xw // heads

    vn = (_rms_scale(v_ref[...]) * gg_ref[...]).astype(BF16)
    for c in range(ts // chunk):
        rows = slice(c * chunk, (c + 1) * chunk)
        for g in range(groups):
            cols = slice(g * gc, (g + 1) * gc)
            mixed = jnp.dot(wsp_ref[g], vn[rows, cols], preferred_element_type=F32) + bsp_ref[g]
            y_ref[rows, cols] = (u_ref[rows, cols] * mixed).astype(y_ref.dtype)

    pos = (i * ts) % seq
    z = cc_ref[...] * ch_ref[...]
    sub = ccp_ref.shape[0]
    z_before = jnp.where(pos == 0, 0.0, ccp_ref[sub - 1:sub, :] * chp_ref[sub - 1:sub, :])
    z_after = jnp.where(pos + ts == seq, 0.0, ccn_ref[0:1, :] * chn_ref[0:1, :])
    row = lax.broadcasted_iota(jnp.int32, z.shape, 0)
    z_prev = jnp.where(row == 0, z_before, pltpu.roll(z, 1, 0))
    z_next = jnp.where(row == ts - 1, z_after, pltpu.roll(z, ts - 1, 0))
    conv = z_prev * cw_ref[0:1, :] + z * cw_ref[1:2, :] + z_next * cw_ref[2:3, :]
    y_ref[:, gw:gw + cw] = (cb_ref[...] * conv).astype(y_ref.dtype)

    scale = hd ** -0.5
    for h in range(heads):
        cols = slice(h * hd, (h + 1) * hd)
        qh = q_ref[:, cols].astype(BF16)
        kh = kv_ref[:, cols]
        vh = kv_ref[:, xw + h * hd: xw + (h + 1) * hd]
        s = lax.dot_general(qh, kh, (((1,), (1,)), ((), ())), preferred_element_type=F32) * scale
        e = jnp.exp(s - jnp.max(s, axis=-1, keepdims=True))
        p = (e / jnp.sum(e, axis=-1, keepdims=True)).astype(BF16)
        yh = jnp.dot(p, vh, preferred_element_type=F32)
        y_ref[:, gw + cw + h * hd: gw + cw + (h + 1) * hd] = yh.astype(y_ref.dtype)


def _branches(proj, kv, wsp, bsp, gg, cw, *, seq, ts, gmlp_w, conv_w, xattn_w, n_mem):
    n = proj.shape[0]
    chunk = wsp.shape[1]
    groups = wsp.shape[0]
    sub = 8
    off_cb = 2 * gmlp_w
    cb_blk = off_cb // conv_w
    q_blk = (off_cb + 3 * conv_w) // xattn_w
    last = n // sub - 1
    tsb = ts // sub

    def prev_map(col):
        return lambda i: (jnp.maximum(i * tsb - 1, 0), col)

    def next_map(col):
        return lambda i: (jnp.minimum((i + 1) * tsb, last), col)

    kern = functools.partial(_branch_kernel, seq=seq, chunk=chunk, groups=groups, heads=XATTN_HEADS)
    return pl.pallas_call(
        kern,
        grid=(n // ts,),
        in_specs=[
            pl.BlockSpec((ts, gmlp_w), lambda i: (i, 0)),
            pl.BlockSpec((ts, gmlp_w), lambda i: (i, 1)),
            pl.BlockSpec((ts, conv_w), lambda i: (i, cb_blk)),
            pl.BlockSpec((ts, conv_w), lambda i: (i, cb_blk + 1)),
            pl.BlockSpec((ts, conv_w), lambda i: (i, cb_blk + 2)),
            pl.BlockSpec((ts, xattn_w), lambda i: (i, q_blk)),
            pl.BlockSpec((sub, conv_w), prev_map(cb_blk + 1)),
            pl.BlockSpec((sub, conv_w), prev_map(cb_blk + 2)),
            pl.BlockSpec((sub, conv_w), next_map(cb_blk + 1)),
            pl.BlockSpec((sub, conv_w), next_map(cb_blk + 2)),
            pl.BlockSpec((n_mem, 2 * xattn_w), lambda i: ((i * ts) // seq, 0)),
            pl.BlockSpec(wsp.shape, lambda i: (0, 0, 0)),
            pl.BlockSpec(bsp.shape, lambda i: (0, 0, 0)),
            pl.BlockSpec((1, gmlp_w), lambda i: (0, 0)),
            pl.BlockSpec(cw.shape, lambda i: (0, 0)),
        ],
        out_specs=pl.BlockSpec((ts, gmlp_w + conv_w + xattn_w), lambda i: (i, 0)),
        out_shape=jax.ShapeDtypeStruct((n, gmlp_w + conv_w + xattn_w), BF16),
        compiler_params=_params("parallel"),
        name="branches",
    )(proj, proj, proj, proj, proj, proj, proj, proj, proj, proj, kv, wsp, bsp, gg, cw)


def _merge_kernel(y_ref, w_ref, ga_ref, gb_ref, gx_ref, o_ref, *, gmlp_w, conv_w, sub):
    ya = y_ref[:, :gmlp_w]
    yb = y_ref[:, gmlp_w:gmlp_w + conv_w]
    yx = y_ref[:, gmlp_w + conv_w:]
    for s in range(o_ref.shape[1] // sub):
        cols = slice(s * sub, (s + 1) * sub)
        a = jnp.dot(ya, w_ref[:gmlp_w, cols], preferred_element_type=F32)
        b = jnp.dot(yb, w_ref[gmlp_w:gmlp_w + conv_w, cols], preferred_element_type=F32)
        x = jnp.dot(yx, w_ref[gmlp_w + conv_w:, cols], preferred_element_type=F32)
        o_ref[:, cols] = (ga_ref[:, cols] * a + gb_ref[:, cols] * b
                          + gx_ref[:, cols] * x).astype(o_ref.dtype)


def _merge(ycat, w_br, proj, *, tm, tn, off_g, d, gmlp_w, conv_w):
    n, k = ycat.shape
    g0 = off_g // tn
    gstep = d // tn
    kern = functools.partial(_merge_kernel, gmlp_w=gmlp_w, conv_w=conv_w, sub=min(MXU_COLS, tn))
    return pl.pallas_call(
        kern,
        grid=(n // tm, d // tn),
        in_specs=[pl.BlockSpec((tm, k), lambda i, j: (i, 0)),
                  pl.BlockSpec((k, tn), lambda i, j: (0, j)),
                  pl.BlockSpec((tm, tn), lambda i, j: (i, g0 + j)),
                  pl.BlockSpec((tm, tn), lambda i, j: (i, g0 + gstep + j)),
                  pl.BlockSpec((tm, tn), lambda i, j: (i, g0 + 2 * gstep + j))],
        out_specs=pl.BlockSpec((tm, tn), lambda i, j: (i, j)),
        out_shape=jax.ShapeDtypeStruct((n, d), BF16),
        compiler_params=_params("parallel", "arbitrary"),
        name="merge",
    )(ycat, w_br, proj, proj, proj)


def _outproj_kernel(m_ref, w_ref, x_ref, o_ref, *, sub):
    m = m_ref[...]
    for s in range(o_ref.shape[1] // sub):
        cols = slice(s * sub, (s + 1) * sub)
        o_ref[:, cols] = x_ref[:, cols] + jnp.dot(m, w_ref[:, cols], preferred_element_type=F32)


def _outproj(merged, w_out, x, tm, tn):
    n, d = x.shape
    return pl.pallas_call(
        functools.partial(_outproj_kernel, sub=min(MXU_COLS, tn)),
        grid=(n // tm, d // tn),
        in_specs=[pl.BlockSpec((tm, d), lambda i, j: (i, 0)),
                  pl.BlockSpec((d, tn), lambda i, j: (0, j)),
                  pl.BlockSpec((tm, tn), lambda i, j: (i, j))],
        out_specs=pl.BlockSpec((tm, tn), lambda i, j: (i, j)),
        out_shape=jax.ShapeDtypeStruct((n, d), F32),
        compiler_params=_params("parallel", "arbitrary"),
        name="out_proj",
    )(merged, w_out, x)


def _router_kernel(x_ref, g_ref, wr_ref, wrt_ref, h_ref, aff_ref, afft_ref):
    h = (_rms_scale(x_ref[...]) * g_ref[...]).astype(BF16)
    h_ref[...] = h
    lt = lax.dot_general(wrt_ref[...], h, (((1,), (1,)), ((), ())), preferred_element_type=F32)
    et = jnp.exp(lt - jnp.max(lt, axis=0, keepdims=True))
    afft_ref[...] = et / jnp.sum(et, axis=0, keepdims=True)
    l = jnp.dot(h, wr_ref[...], preferred_element_type=F32)
    e = jnp.exp(l - jnp.max(l, axis=-1, keepdims=True))
    aff_ref[...] = e / jnp.sum(e, axis=-1, keepdims=True)


def _router(x2, g, wr, tm):
    n, d = x2.shape
    ne = wr.shape[1]
    return pl.pallas_call(
        _router_kernel,
        grid=(n // tm,),
        in_specs=[pl.BlockSpec((tm, d), lambda i: (i, 0)),
                  pl.BlockSpec((1, d), lambda i: (0, 0)),
                  pl.BlockSpec((d, ne), lambda i: (0, 0)),
                  pl.BlockSpec((ne, d), lambda i: (0, 0))],
        out_specs=[pl.BlockSpec((tm, d), lambda i: (i, 0)),
                   pl.BlockSpec((tm, ne), lambda i: (i, 0)),
                   pl.BlockSpec((ne, tm), lambda i: (0, i))],
        out_shape=[jax.ShapeDtypeStruct((n, d), BF16),
                   jax.ShapeDtypeStruct((n, ne), F32),
                   jax.ShapeDtypeStruct((ne, n), F32)],
        compiler_params=_params("parallel"),
        name="ffn_norm_router",
    )(x2, g.reshape(1, d), wr, wr.T)


def _select_kernel(aff_ref, slot_ref, base_ref, *, cap, tb):
    ne, n = aff_ref.shape
    bits = pltpu.bitcast(aff_ref[...], jnp.int32)
    capf = jnp.float32(cap)

    def search(k, t):
        cand = t | lax.shift_left(jnp.int32(1), 30 - k)
        cnt = jnp.sum(jnp.where(bits >= cand, 1.0, 0.0), axis=1, keepdims=True)
        return jnp.where(cnt >= capf, cand, t)

    thr = lax.fori_loop(0, 31, search, jnp.zeros((ne, 1), jnp.int32))
    n_gt = jnp.sum(jnp.where(bits > thr, 1.0, 0.0), axis=1, keepdims=True)
    need = capf - n_gt

    r = lax.broadcasted_iota(jnp.int32, (tb, tb), 0)
    c = lax.broadcasted_iota(jnp.int32, (tb, tb), 1)
    tri = jnp.where(r < c, 1.0, 0.0).astype(BF16)
    lane = lax.broadcasted_iota(jnp.int32, base_ref.shape, 1)
    eq_seen = jnp.zeros((ne, 1), F32)
    sel_seen = jnp.zeros((ne, 1), F32)
    base = jnp.zeros(base_ref.shape, F32)
    for b in range(n // tb):
        blk = pltpu.bitcast(aff_ref[:, b * tb:(b + 1) * tb], jnp.int32)
        eq = jnp.where(blk == thr, 1.0, 0.0)
        eq_rank = jnp.dot(eq.astype(BF16), tri, preferred_element_type=F32) + eq_seen
        sel = jnp.where(blk > thr, 1.0, jnp.where(eq_rank < need, eq, 0.0))
        pos = jnp.dot(sel.astype(BF16), tri, preferred_element_type=F32) + sel_seen
        slot_ref[:, b * tb:(b + 1) * tb] = jnp.where(sel > 0.0, pos, -1.0).astype(jnp.int32)
        base = jnp.where(lane == b, sel_seen, base)
        eq_seen = eq_seen + jnp.sum(eq, axis=1, keepdims=True)
        sel_seen = sel_seen + jnp.sum(sel, axis=1, keepdims=True)
    base = jnp.where(lane == n // tb, sel_seen, base)
    base_ref[...] = base.astype(jnp.int32)


def _select(afft, cap, tb):
    ne, n = afft.shape
    lanes = 128
    assert n // tb < lanes
    kern = functools.partial(_select_kernel, cap=cap, tb=tb)
    return pl.pallas_call(
        kern,
        out_shape=[jax.ShapeDtypeStruct((ne, n), jnp.int32),
                   jax.ShapeDtypeStruct((ne, lanes), jnp.int32)],
        compiler_params=pltpu.CompilerParams(vmem_limit_bytes=VMEM_LIMIT_BYTES),
        name="expert_choice_select",
    )(afft)


def _gather_kernel(base_ref, slot_ref, aff_ref, h_ref, xe_ref, gw_ref, *, cap, nblk, chunk):
    eb = pl.program_id(0)
    b = pl.program_id(1)

    @pl.when(b == 0)
    def _():
        xe_ref[...] = jnp.zeros_like(xe_ref)
        gw_ref[...] = jnp.zeros_like(gw_ref)

    tb = h_ref.shape[0]
    for k in range(xe_ref.shape[0]):
        e = eb * xe_ref.shape[0] + k
        first = (base_ref[e * (nblk + 1) + b] // BF16_ROW_TILE) * BF16_ROW_TILE
        end = base_ref[e * (nblk + 1) + b + 1]
        slot = slot_ref[k]
        aff = aff_ref[k]

        def body(c, carry, k=k, first=first, slot=slot, aff=aff):
            lo = first + c * chunk
            st = pl.multiple_of(jnp.minimum(lo, cap - chunk), BF16_ROW_TILE)
            rows = lax.broadcasted_iota(jnp.int32, (chunk, tb), 0) + st
            hit = jnp.where(rows == slot, jnp.where(rows >= lo, 1.0, 0.0), 0.0)
            picked = jnp.dot(hit.astype(BF16), h_ref[...], preferred_element_type=F32)
            cur = xe_ref[k, pl.ds(st, chunk), :].astype(F32)
            xe_ref[k, pl.ds(st, chunk), :] = (cur + picked).astype(xe_ref.dtype)
            gw_ref[k, pl.ds(st, chunk), :] += jnp.sum(hit * aff, axis=1, keepdims=True)
            return carry

        lax.fori_loop(0, (end - first + chunk - 1) // chunk, body, 0)


def _gather(base_flat, slot3, aff3, h, *, cap, tb, chunk, experts_per_step):
    ne = slot3.shape[0]
    n, d = h.shape
    nblk = n // tb
    eps = experts_per_step
    assert cap >= chunk and (cap - chunk) % BF16_ROW_TILE == 0 and chunk % BF16_ROW_TILE == 0
    kern = functools.partial(_gather_kernel, cap=cap, nblk=nblk, chunk=chunk)
    return pl.pallas_call(
        kern,
        grid_spec=pltpu.PrefetchScalarGridSpec(
            num_scalar_prefetch=1,
            grid=(ne // eps, nblk),
            in_specs=[pl.BlockSpec((eps, 1, tb), lambda e, b, base: (e, 0, b)),
                      pl.BlockSpec((eps, 1, tb), lambda e, b, base: (e, 0, b)),
                      pl.BlockSpec((tb, d), lambda e, b, base: (b, 0))],
            out_specs=[pl.BlockSpec((eps, cap, d), lambda e, b, base: (e, 0, 0)),
                       pl.BlockSpec((eps, cap, 1), lambda e, b, base: (e, 0, 0))],
        ),
        out_shape=[jax.ShapeDtypeStruct((ne, cap, d), BF16),
                   jax.ShapeDtypeStruct((ne, cap, 1), F32)],
        compiler_params=_params("parallel", "arbitrary"),
        name="expert_gather",
    )(base_flat, slot3, aff3, h)


def _ffn_kernel(xe_ref, gw_ref, wg_ref, wu_ref, wd_ref, ye_ref, hid_ref, *, nt_up, tf):
    j = pl.program_id(1)

    @pl.when(j < nt_up)
    def _():
        x = xe_ref[...]
        g = jnp.dot(x, wg_ref[...].astype(BF16), preferred_element_type=F32)
        u = jnp.dot(x, wu_ref[...].astype(BF16), preferred_element_type=F32)
        off = pl.multiple_of(j * tf, tf)
        hid_ref[:, pl.ds(off, tf)] = ((g * _sigmoid(g)) * u).astype(hid_ref.dtype)

    @pl.when(j >= nt_up)
    def _():
        y = jnp.dot(hid_ref[...], wd_ref[...].astype(BF16), preferred_element_type=F32)
        ye_ref[...] = (y * gw_ref[...]).astype(ye_ref.dtype)


def _ffn(xe, gw, wg, wu, wd, tf, tn):
    ne, cap, d = xe.shape
    f = wg.shape[2]
    nt_up = f // tf
    nt_dn = d // tn
    kern = functools.partial(_ffn_kernel, nt_up=nt_up, tf=tf)
    return pl.pallas_call(
        kern,
        grid=(ne, nt_up + nt_dn),
        in_specs=[pl.BlockSpec((None, cap, d), lambda e, j: (e, 0, 0)),
                  pl.BlockSpec((None, cap, 1), lambda e, j: (e, 0, 0)),
                  pl.BlockSpec((None, d, tf), lambda e, j: (e, 0, jnp.minimum(j, nt_up - 1))),
                  pl.BlockSpec((None, d, tf), lambda e, j: (e, 0, jnp.minimum(j, nt_up - 1))),
                  pl.BlockSpec((None, f, tn), lambda e, j: (e, 0, jnp.maximum(j - nt_up, 0)))],
        out_specs=pl.BlockSpec((None, cap, tn), lambda e, j: (e, 0, jnp.maximum(j - nt_up, 0))),
        out_shape=jax.ShapeDtypeStruct((ne, cap, d), BF16),
        scratch_shapes=[pltpu.VMEM((cap, f), BF16)],
        compiler_params=_params("parallel", "arbitrary"),
        name="expert_ffn",
    )(xe, gw, wg, wu, wd)


def _expert_column(ref, e):
    lane = lax.broadcasted_iota(jnp.int32, ref.shape, 1)
    return jnp.sum(jnp.where(lane == e, ref[...], 0), axis=1, keepdims=True)


def _combine_any_kernel(base_ref, slot_ref, x_ref, ya_ref, yb_ref, g_ref, o_ref, *, cap, nblk):
    b = pl.program_id(0)
    e = pl.program_id(1)
    ne = pl.num_programs(1)

    @pl.when(e == 0)
    def _():
        o_ref[...] = x_ref[...]

    tb = x_ref.shape[0]
    blk = ya_ref.shape[0]
    start = jnp.minimum(base_ref[e * (nblk + 1) + b] // blk, cap // blk - 1) * blk
    slot = _expert_column(slot_ref, e)
    cols = lax.broadcasted_iota(jnp.int32, (tb, blk), 1) + start
    oh_a = jnp.where(cols == slot, 1.0, 0.0).astype(BF16)
    oh_b = jnp.where(cols + blk == slot, 1.0, 0.0).astype(BF16)
    o_ref[...] += (jnp.dot(oh_a, ya_ref[...], preferred_element_type=F32)
                   + jnp.dot(oh_b, yb_ref[...], preferred_element_type=F32))

    @pl.when(e == ne - 1)
    def _():
        o_ref[...] = _rms_scale(o_ref[...]) * g_ref[...]


def _combine_any(base_flat, slot_tok, x2, ye, g_final, *, cap, tb):
    n, d = x2.shape
    ne = ye.shape[0]
    nblk = n // tb
    blk = tb
    last = cap // blk - 1

    def ya_map(b, e, base):
        return (e, jnp.minimum(base[e * (nblk + 1) + b] // blk, last), 0)

    def yb_map(b, e, base):
        return (e, jnp.minimum(jnp.minimum(base[e * (nblk + 1) + b] // blk, last) + 1, last), 0)

    kern = functools.partial(_combine_any_kernel, cap=cap, nblk=nblk)
    return pl.pallas_call(
        kern,
        grid_spec=pltpu.PrefetchScalarGridSpec(
            num_scalar_prefetch=1,
            grid=(nblk, ne),
            in_specs=[pl.BlockSpec((tb, ne), lambda b, e, base: (b, 0)),
                      pl.BlockSpec((tb, d), lambda b, e, base: (b, 0)),
                      pl.BlockSpec((None, blk, d), ya_map),
                      pl.BlockSpec((None, blk, d), yb_map),
                      pl.BlockSpec((1, d), lambda b, e, base: (0, 0))],
            out_specs=pl.BlockSpec((tb, d), lambda b, e, base: (b, 0)),
        ),
        out_shape=jax.ShapeDtypeStruct((n, d), F32),
        compiler_params=_params("parallel", "arbitrary"),
        name="expert_combine_any",
    )(base_flat, slot_tok, x2, ye, ye, g_final.reshape(1, d))


def _window_start(base, e, b, *, nblk, cap, win):
    tile = jnp.minimum(base[e * (nblk + 1) + b] // BF16_ROW_TILE, (cap - win) // BF16_ROW_TILE)
    return tile * BF16_ROW_TILE


def _combine_windows_kernel(base_ref, slot_ref, x_ref, g_ref, *rest, cap, nblk, win):
    y_refs, o_ref = rest[:-1], rest[-1]
    b = pl.program_id(0)
    s = pl.program_id(1)
    per = len(y_refs)

    @pl.when(s == 0)
    def _():
        o_ref[...] = x_ref[...]

    tb = x_ref.shape[0]
    col = lax.broadcasted_iota(jnp.int32, (tb, win), 1)
    onehots = []
    for k in range(per):
        e = s * per + k
        start = _window_start(base_ref, e, b, nblk=nblk, cap=cap, win=win)
        onehots.append(jnp.where(col + start == _expert_column(slot_ref, e), 1.0, 0.0).astype(BF16))
    y = jnp.concatenate([r[...] for r in y_refs], axis=0)
    o_ref[...] += jnp.dot(jnp.concatenate(onehots, axis=1), y, preferred_element_type=F32)

    @pl.when(s == pl.num_programs(1) - 1)
    def _():
        o_ref[...] = _rms_scale(o_ref[...]) * g_ref[...]


def _combine_windows(base_flat, slot_tok, x2, ye, g_final, *, cap, tb, win, experts_per_step):
    n, d = x2.shape
    ne = ye.shape[0]
    nblk = n // tb
    per = experts_per_step
    assert ne % per == 0

    def y_map(k):
        def index(b, s, base):
            e = s * per + k
            return (e, _window_start(base, e, b, nblk=nblk, cap=cap, win=win), 0)
        return index

    kern = functools.partial(_combine_windows_kernel, cap=cap, nblk=nblk, win=win)
    return pl.pallas_call(
        kern,
        grid_spec=pltpu.PrefetchScalarGridSpec(
            num_scalar_prefetch=1,
            grid=(nblk, ne // per),
            in_specs=[pl.BlockSpec((tb, ne), lambda b, s, base: (b, 0)),
                      pl.BlockSpec((tb, d), lambda b, s, base: (b, 0)),
                      pl.BlockSpec((1, d), lambda b, s, base: (0, 0))]
                     + [pl.BlockSpec((None, pl.Element(win), pl.Element(d)), y_map(k))
                        for k in range(per)],
            out_specs=pl.BlockSpec((tb, d), lambda b, s, base: (b, 0)),
        ),
        out_shape=jax.ShapeDtypeStruct((n, d), F32),
        compiler_params=_params("parallel", "arbitrary"),
        name="expert_combine_windows",
    )(base_flat, slot_tok, x2, g_final.reshape(1, d), *([ye] * per))


def _combine(base, slot_tok, x2, ye, g_final, *, cap, tb, win):
    nblk = x2.shape[0] // tb
    base_flat = base[:, :nblk + 1].reshape(-1)
    first = (base[:, :nblk] // BF16_ROW_TILE) * BF16_ROW_TILE
    fits = jnp.all(base[:, 1:nblk + 1] - jnp.minimum(first, cap - win) <= win)
    return lax.cond(
        fits,
        lambda: _combine_windows(base_flat, slot_tok, x2, ye, g_final, cap=cap, tb=tb, win=win,
                                 experts_per_step=8),
        lambda: _combine_any(base_flat, slot_tok, x2, ye, g_final, cap=cap, tb=tb))


def _encode(x3, mem3, p):
    bsz, seq, d = x3.shape
    n = bsz * seq
    n_mem = mem3.shape[1]
    x = x3.reshape(n, d)
    mem = mem3.reshape(bsz * n_mem, d)

    gmlp_w = p["gmlp_norm_g"].shape[0]
    conv_w = p["conv_w"].shape[1]
    xattn_w = p["w_mem_kv"].shape[1] // 2
    off_cb = 2 * gmlp_w
    off_g = off_cb + 3 * conv_w + xattn_w
    ne = p["w_router"].shape[1]
    cap = CAPACITY_FACTOR * n // ne

    def col_tile(cols):
        return next(t for t in (1024, 512, 256, 128) if cols % t == 0 and t <= cols)

    tm = min(1024, n)
    ts = min(256, seq)
    tb = 256 if cap >= 256 + BF16_ROW_TILE else 128

    h = _rmsnorm_cast(x, p["norm_mix_g"], ts)
    proj = _proj(h, p["w_in"], tm, col_tile(p["w_in"].shape[1]), off_cb, off_g, F32)
    m = _rmsnorm_cast(mem, p["mem_norm_g"], n_mem)
    kv = _proj(m, p["w_mem_kv"], n_mem, col_tile(2 * xattn_w), 0, 2 * xattn_w, BF16)
    ycat = _branches(proj, kv, p["w_spatial"], p["b_spatial"], p["gmlp_norm_g"].reshape(1, gmlp_w),
                     p["conv_w"], seq=seq, ts=ts, gmlp_w=gmlp_w, conv_w=conv_w, xattn_w=xattn_w,
                     n_mem=n_mem)
    merged = _merge(ycat, p["w_branch"], proj, tm=tm // 2, tn=col_tile(d), off_g=off_g, d=d,
                    gmlp_w=gmlp_w, conv_w=conv_w)
    x2 = _outproj(merged, p["w_out"], x, tm, col_tile(d))

    h2, aff_tok, afft = _router(x2, p["norm_ffn_g"], p["w_router"], ts)
    slot, base = _select(afft, cap, tb)
    nblk = n // tb
    base_flat = base[:, :nblk + 1].reshape(-1)
    xe, gw = _gather(base_flat, slot.reshape(ne, 1, n), afft.reshape(ne, 1, n), h2, cap=cap, tb=tb,
                     chunk=GATHER_CHUNK_ROWS, experts_per_step=2)
    ye = _ffn(xe, gw, p["w_gate"], p["w_up"], p["w_down"], min(256, d), min(256, d))
    y = _combine(base, slot.T, x2, ye, p["final_norm_g"], cap=cap, tb=tb,
                 win=min(COMBINE_WINDOW_ROWS, tb))
    return y.reshape(bsz, seq, d)


def kernel(x_prompt, x_sample, mem_prompt, mem_sample, norm_mix_g, w_in, gmlp_norm_g, w_spatial,
           b_spatial, conv_w, mem_norm_g, w_mem_kv, w_branch_a, w_branch_b, w_branch_x, w_out,
           norm_ffn_g, w_router, w_gate, w_up, w_down, final_norm_g):
    assert w_in.shape[0] == 1, "one layer"
    groups, chunk = b_spatial.shape[1], b_spatial.shape[2]
    gc = gmlp_norm_g.shape[1] // groups
    p = {
        "norm_mix_g": norm_mix_g[0],
        "w_in": w_in[0].astype(BF16),
        "gmlp_norm_g": gmlp_norm_g[0],
        "w_spatial": w_spatial[0].astype(BF16),
        "b_spatial": jnp.broadcast_to(b_spatial[0][:, :, None], (groups, chunk, gc)),
        "conv_w": conv_w[0],
        "mem_norm_g": mem_norm_g[0],
        "w_mem_kv": w_mem_kv[0].astype(BF16),
        "w_branch": jnp.concatenate([w_branch_a[0], w_branch_b[0], w_branch_x[0]],
                                    axis=0).astype(BF16),
        "w_out": w_out[0].astype(BF16),
        "norm_ffn_g": norm_ffn_g[0],
        "w_router": w_router[0].astype(BF16),
        "w_gate": w_gate.reshape(w_gate.shape[1:]),
        "w_up": w_up.reshape(w_up.shape[1:]),
        "w_down": w_down.reshape(w_down.shape[1:]),
        "final_norm_g": final_norm_g,
    }
    return (_encode(x_prompt, mem_prompt, p), _encode(x_sample, mem_sample, p))
```

```python
import functools

import jax
import jax.numpy as jnp
from jax import lax
from jax.experimental import pallas as pl
from jax.experimental.pallas import tpu as pltpu
from jax.experimental.pallas import tpu_sc as plsc

EPS = 1e-6
XATTN_HEADS = 4
CAPACITY_FACTOR = 2
BF16_ROW_TILE = 16
SUBLANES = 8
PICK_CHUNK_ROWS = 80
SC_ROW_SPLIT = 8
SC_GATHER_WINDOW = 128
COMBINE_WINDOW_ROWS = 128
VMEM_LIMIT_BYTES = 56 * 1024 * 1024

F32 = jnp.float32
BF16 = jnp.bfloat16


def _params(*semantics):
    return pltpu.CompilerParams(dimension_semantics=semantics,
                                vmem_limit_bytes=VMEM_LIMIT_BYTES)


GELU_K = 0.7978845608028654
GELU_CUBIC = 0.044715
TANH_SATURATED = 30.0
MXU_COLS = 256


def _sigmoid(x):
    return 1.0 / (1.0 + jnp.exp(-x))


def _rms_scale(x):
    return x * lax.rsqrt(jnp.mean(x * x, axis=-1, keepdims=True) + EPS)


def _rmsnorm_cast_kernel(x_ref, g_ref, o_ref):
    o_ref[...] = (_rms_scale(x_ref[...]) * g_ref[...]).astype(o_ref.dtype)


def _rmsnorm_cast(x, g, tm):
    n, d = x.shape
    return pl.pallas_call(
        _rmsnorm_cast_kernel,
        grid=(n // tm,),
        in_specs=[pl.BlockSpec((tm, d), lambda i: (i, 0)),
                  pl.BlockSpec((1, d), lambda i: (0, 0))],
        out_specs=pl.BlockSpec((tm, d), lambda i: (i, 0)),
        out_shape=jax.ShapeDtypeStruct((n, d), BF16),
        compiler_params=_params("parallel"),
        name="rmsnorm_cast",
    )(x, g.reshape(1, d))


def _proj_kernel(h_ref, w_ref, o_ref, *, sub, gelu_tiles, sigmoid_from):
    j = pl.program_id(1)
    nsub = o_ref.shape[1] // sub
    h = h_ref[...]
    for s in range(nsub):
        c = j * nsub + s
        is_gelu = c < gelu_tiles
        is_sig = c >= sigmoid_from
        q1 = jnp.where(is_gelu, GELU_K, jnp.where(is_sig, 0.5, 0.0))
        q3 = jnp.where(is_gelu, GELU_K * GELU_CUBIC, 0.0)
        q0 = jnp.where(jnp.logical_or(is_gelu, is_sig), 0.0, TANH_SATURATED)
        p1 = jnp.where(is_sig, 0.0, 1.0)
        p0 = jnp.where(is_sig, 1.0, 0.0)
        cols = slice(s * sub, (s + 1) * sub)
        x = jnp.dot(h, w_ref[:, cols], preferred_element_type=F32)
        gate = 0.5 * (1.0 + jnp.tanh(x * (q1 + q3 * (x * x)) + q0))
        o_ref[:, cols] = ((x * p1 + p0) * gate).astype(o_ref.dtype)


def _proj(h, w, tm, tn, gelu_cols, sigmoid_from_col, out_dtype):
    n, d = h.shape
    cols = w.shape[1]
    sub = min(MXU_COLS, tn)
    kern = functools.partial(_proj_kernel, sub=sub, gelu_tiles=gelu_cols // sub,
                             sigmoid_from=sigmoid_from_col // sub)
    return pl.pallas_call(
        kern,
        grid=(n // tm, cols // tn),
        in_specs=[pl.BlockSpec((tm, d), lambda i, j: (i, 0)),
                  pl.BlockSpec((d, tn), lambda i, j: (0, j))],
        out_specs=pl.BlockSpec((tm, tn), lambda i, j: (i, j)),
        out_shape=jax.ShapeDtypeStruct((n, cols), out_dtype),
        compiler_params=_params("parallel", "arbitrary"),
        name="in_proj",
    )(h, w)


def _branch_kernel(u_ref, v_ref, cb_ref, cc_ref, ch_ref, q_ref, ccp_ref, chp_ref, ccn_ref, chn_ref,
                   kv_ref, wsp_ref, bsp_ref, gg_ref, cw_ref, y_ref, *, seq, chunk, groups, heads):
    i = pl.program_id(0)
    ts = u_ref.shape[0]
    gw = u_ref.shape[1]
    gc = gw // groups
    cw = cb_ref.shape[1]
    xw = q_ref.shape[1]
    hd = xw // heads

    vn = (_rms_scale(v_ref[...]) * gg_ref[...]).astype(BF16)
    for c in range(ts // chunk):
        rows = slice(c * chunk, (c + 1) * chunk)
        for g in range(groups):
            cols = slice(g * gc, (g + 1) * gc)
            mixed = jnp.dot(wsp_ref[g], vn[rows, cols], preferred_element_type=F32) + bsp_ref[g]
            y_ref[rows, cols] = (u_ref[rows, cols] * mixed).astype(y_ref.dtype)

    pos = (i * ts) % seq
    z = cc_ref[...] * ch_ref[...]
    sub = ccp_ref.shape[0]
    z_before = jnp.where(pos == 0, 0.0, ccp_ref[sub - 1:sub, :] * chp_ref[sub - 1:sub, :])
    z_after = jnp.where(pos + ts == seq, 0.0, ccn_ref[0:1, :] * chn_ref[0:1, :])
    row = lax.broadcasted_iota(jnp.int32, z.shape, 0)
    z_prev = jnp.where(row == 0, z_before, pltpu.roll(z, 1, 0))
    z_next = jnp.where(row == ts - 1, z_after, pltpu.roll(z, ts - 1, 0))
    conv = z_prev * cw_ref[0:1, :] + z * cw_ref[1:2, :] + z_next * cw_ref[2:3, :]
    y_ref[:, gw:gw + cw] = (cb_ref[...] * conv).astype(y_ref.dtype)

    scale = hd ** -0.5
    for h in range(heads):
        cols = slice(h * hd, (h + 1) * hd)
        qh = q_ref[:, cols].astype(BF16)
        kh = kv_ref[:, cols]
        vh = kv_ref[:, xw + h * hd: xw + (h + 1) * hd]
        s = lax.dot_general(qh, kh, (((1,), (1,)), ((), ())), preferred_element_type=F32) * scale
        e = jnp.exp(s - jnp.max(s, axis=-1, keepdims=True))
        p = (e / jnp.sum(e, axis=-1, keepdims=True)).astype(BF16)
        yh = jnp.dot(p, vh, preferred_element_type=F32)
        y_ref[:, gw + cw + h * hd: gw + cw + (h + 1) * hd] = yh.astype(y_ref.dtype)


def _branches(proj, kv, wsp, bsp, gg, cw, *, seq, ts, gmlp_w, conv_w, xattn_w, n_mem):
    n = proj.shape[0]
    chunk = wsp.shape[1]
    groups = wsp.shape[0]
    sub = 8
    off_cb = 2 * gmlp_w
    cb_blk = off_cb // conv_w
    q_blk = (off_cb + 3 * conv_w) // xattn_w
    last = n // sub - 1
    tsb = ts // sub

    def prev_map(col):
        return lambda i: (jnp.maximum(i * tsb - 1, 0), col)

    def next_map(col):
        return lambda i: (jnp.minimum((i + 1) * tsb, last), col)

    kern = functools.partial(_branch_kernel, seq=seq, chunk=chunk, groups=groups, heads=XATTN_HEADS)
    return pl.pallas_call(
        kern,
        grid=(n // ts,),
        in_specs=[
            pl.BlockSpec((ts, gmlp_w), lambda i: (i, 0)),
            pl.BlockSpec((ts, gmlp_w), lambda i: (i, 1)),
            pl.BlockSpec((ts, conv_w), lambda i: (i, cb_blk)),
            pl.BlockSpec((ts, conv_w), lambda i: (i, cb_blk + 1)),
            pl.BlockSpec((ts, conv_w), lambda i: (i, cb_blk + 2)),
            pl.BlockSpec((ts, xattn_w), lambda i: (i, q_blk)),
            pl.BlockSpec((sub, conv_w), prev_map(cb_blk + 1)),
            pl.BlockSpec((sub, conv_w), prev_map(cb_blk + 2)),
            pl.BlockSpec((sub, conv_w), next_map(cb_blk + 1)),
            pl.BlockSpec((sub, conv_w), next_map(cb_blk + 2)),
            pl.BlockSpec((n_mem, 2 * xattn_w), lambda i: ((i * ts) // seq, 0)),
            pl.BlockSpec(wsp.shape, lambda i: (0, 0, 0)),
            pl.BlockSpec(bsp.shape, lambda i: (0, 0, 0)),
            pl.BlockSpec((1, gmlp_w), lambda i: (0, 0)),
            pl.BlockSpec(cw.shape, lambda i: (0, 0)),
        ],
        out_specs=pl.BlockSpec((ts, gmlp_w + conv_w + xattn_w), lambda i: (i, 0)),
        out_shape=jax.ShapeDtypeStruct((n, gmlp_w + conv_w + xattn_w), BF16),
        compiler_params=_params("parallel"),
        name="branches",
    )(proj, proj, proj, proj, proj, proj, proj, proj, proj, proj, kv, wsp, bsp, gg, cw)


def _merge_kernel(y_ref, w_ref, ga_ref, gb_ref, gx_ref, o_ref, *, gmlp_w, conv_w, sub):
    ya = y_ref[:, :gmlp_w]
    yb = y_ref[:, gmlp_w:gmlp_w + conv_w]
    yx = y_ref[:, gmlp_w + conv_w:]
    for s in range(o_ref.shape[1] // sub):
        cols = slice(s * sub, (s + 1) * sub)
        a = jnp.dot(ya, w_ref[:gmlp_w, cols], preferred_element_type=F32)
        b = jnp.dot(yb, w_ref[gmlp_w:gmlp_w + conv_w, cols], preferred_element_type=F32)
        x = jnp.dot(yx, w_ref[gmlp_w + conv_w:, cols], preferred_element_type=F32)
        o_ref[:, cols] = (ga_ref[:, cols] * a + gb_ref[:, cols] * b
                          + gx_ref[:, cols] * x).astype(o_ref.dtype)


def _merge(ycat, w_br, proj, *, tm, tn, off_g, d, gmlp_w, conv_w):
    n, k = ycat.shape
    g0 = off_g // tn
    gstep = d // tn
    kern = functools.partial(_merge_kernel, gmlp_w=gmlp_w, conv_w=conv_w, sub=min(MXU_COLS, tn))
    return pl.pallas_call(
        kern,
        grid=(n // tm, d // tn),
        in_specs=[pl.BlockSpec((tm, k), lambda i, j: (i, 0)),
                  pl.BlockSpec((k, tn), lambda i, j: (0, j)),
                  pl.BlockSpec((tm, tn), lambda i, j: (i, g0 + j)),
                  pl.BlockSpec((tm, tn), lambda i, j: (i, g0 + gstep + j)),
                  pl.BlockSpec((tm, tn), lambda i, j: (i, g0 + 2 * gstep + j))],
        out_specs=pl.BlockSpec((tm, tn), lambda i, j: (i, j)),
        out_shape=jax.ShapeDtypeStruct((n, d), BF16),
        compiler_params=_params("parallel", "arbitrary"),
        name="merge",
    )(ycat, w_br, proj, proj, proj)


def _outproj_kernel(m_ref, w_ref, x_ref, o_ref, *, sub):
    m = m_ref[...]
    for s in range(o_ref.shape[1] // sub):
        cols = slice(s * sub, (s + 1) * sub)
        o_ref[:, cols] = x_ref[:, cols] + jnp.dot(m, w_ref[:, cols], preferred_element_type=F32)


def _outproj(merged, w_out, x, tm, tn):
    n, d = x.shape
    return pl.pallas_call(
        functools.partial(_outproj_kernel, sub=min(MXU_COLS, tn)),
        grid=(n // tm, d // tn),
        in_specs=[pl.BlockSpec((tm, d), lambda i, j: (i, 0)),
                  pl.BlockSpec((d, tn), lambda i, j: (0, j)),
                  pl.BlockSpec((tm, tn), lambda i, j: (i, j))],
        out_specs=pl.BlockSpec((tm, tn), lambda i, j: (i, j)),
        out_shape=jax.ShapeDtypeStruct((n, d), F32),
        compiler_params=_params("parallel", "arbitrary"),
        name="out_proj",
    )(merged, w_out, x)


def _pack_bf16_pair(lo, hi):
    lo_bits = pltpu.bitcast(lo.astype(BF16).astype(F32), jnp.uint32)
    hi_bits = pltpu.bitcast(hi.astype(BF16).astype(F32), jnp.uint32)
    return lax.shift_right_logical(lo_bits, jnp.uint32(16)) | (hi_bits & jnp.uint32(0xFFFF0000))


def _unpack_bf16_pair(words):
    lo = pltpu.bitcast(lax.shift_left(words, jnp.uint32(16)), F32)
    hi = pltpu.bitcast(words & jnp.uint32(0xFFFF0000), F32)
    return lo.astype(BF16), hi.astype(BF16)


def _router_kernel(x_ref, g_ref, wrt_ref, hp_ref, afft_ref):
    hf = _rms_scale(x_ref[...]) * g_ref[...]
    half = hf.shape[1] // 2
    hp_ref[...] = _pack_bf16_pair(hf[:, :half], hf[:, half:])
    lt = lax.dot_general(wrt_ref[...], hf.astype(BF16), (((1,), (1,)), ((), ())),
                         preferred_element_type=F32)
    et = jnp.exp(lt - jnp.max(lt, axis=0, keepdims=True))
    afft_ref[...] = et / jnp.sum(et, axis=0, keepdims=True)


def _router(x2, g, wr, tm):
    n, d = x2.shape
    ne = wr.shape[1]
    return pl.pallas_call(
        _router_kernel,
        grid=(n // tm,),
        in_specs=[pl.BlockSpec((tm, d), lambda i: (i, 0)),
                  pl.BlockSpec((1, d), lambda i: (0, 0)),
                  pl.BlockSpec((ne, d), lambda i: (0, 0))],
        out_specs=[pl.BlockSpec((tm, d // 2), lambda i: (i, 0)),
                   pl.BlockSpec((ne, tm), lambda i: (0, i))],
        out_shape=[jax.ShapeDtypeStruct((n, d // 2), jnp.uint32),
                   jax.ShapeDtypeStruct((ne, n), F32)],
        compiler_params=_params("parallel"),
        name="ffn_norm_router",
    )(x2, g.reshape(1, d), wr.T)


def _select_kernel(aff_ref, slot_ref, base_ref, *, cap, tb):
    ne, n = aff_ref.shape
    bits = pltpu.bitcast(aff_ref[...], jnp.int32)
    capf = jnp.float32(cap)

    def search(k, t):
        cand = t | lax.shift_left(jnp.int32(1), 30 - k)
        cnt = jnp.sum(jnp.where(bits >= cand, 1.0, 0.0), axis=1, keepdims=True)
        return jnp.where(cnt >= capf, cand, t)

    thr = lax.fori_loop(0, 31, search, jnp.zeros((ne, 1), jnp.int32))
    n_gt = jnp.sum(jnp.where(bits > thr, 1.0, 0.0), axis=1, keepdims=True)
    need = capf - n_gt

    r = lax.broadcasted_iota(jnp.int32, (tb, tb), 0)
    c = lax.broadcasted_iota(jnp.int32, (tb, tb), 1)
    tri = jnp.where(r < c, 1.0, 0.0).astype(BF16)
    lane = lax.broadcasted_iota(jnp.int32, base_ref.shape, 1)
    eq_seen = jnp.zeros((ne, 1), F32)
    sel_seen = jnp.zeros((ne, 1), F32)
    base = jnp.zeros(base_ref.shape, F32)
    for b in range(n // tb):
        blk = pltpu.bitcast(aff_ref[:, b * tb:(b + 1) * tb], jnp.int32)
        eq = jnp.where(blk == thr, 1.0, 0.0)
        eq_rank = jnp.dot(eq.astype(BF16), tri, preferred_element_type=F32) + eq_seen
        sel = jnp.where(blk > thr, 1.0, jnp.where(eq_rank < need, eq, 0.0))
        pos = jnp.dot(sel.astype(BF16), tri, preferred_element_type=F32) + sel_seen
        slot_ref[:, b * tb:(b + 1) * tb] = jnp.where(sel > 0.0, pos, -1.0).astype(jnp.int32)
        base = jnp.where(lane == b, sel_seen, base)
        eq_seen = eq_seen + jnp.sum(eq, axis=1, keepdims=True)
        sel_seen = sel_seen + jnp.sum(sel, axis=1, keepdims=True)
    base = jnp.where(lane == n // tb, sel_seen, base)
    base_ref[...] = base.astype(jnp.int32)


def _select(afft, cap, tb):
    ne, n = afft.shape
    lanes = 128
    assert n // tb < lanes
    kern = functools.partial(_select_kernel, cap=cap, tb=tb)
    return pl.pallas_call(
        kern,
        out_shape=[jax.ShapeDtypeStruct((ne, n), jnp.int32),
                   jax.ShapeDtypeStruct((ne, lanes), jnp.int32)],
        compiler_params=pltpu.CompilerParams(vmem_limit_bytes=VMEM_LIMIT_BYTES),
        name="expert_choice_select",
    )(afft)


def _pick_list_kernel(base_ref, slot_ref, aff_ref, tok_ref, gw_ref, *, cap, nblk, chunk):
    eb = pl.program_id(0)
    b = pl.program_id(1)

    @pl.when(b == 0)
    def _():
        tok_ref[...] = jnp.zeros_like(tok_ref)
        gw_ref[...] = jnp.zeros_like(gw_ref)

    tb = slot_ref.shape[2]
    token = (lax.broadcasted_iota(jnp.int32, (1, tb), 1) + b * tb).astype(F32)
    for k in range(tok_ref.shape[0]):
        e = eb * tok_ref.shape[0] + k
        first = (base_ref[e * (nblk + 1) + b] // SUBLANES) * SUBLANES
        end = base_ref[e * (nblk + 1) + b + 1]
        slot = slot_ref[k]
        aff = aff_ref[k]

        def body(c, carry, k=k, first=first, slot=slot, aff=aff):
            lo = first + c * chunk
            st = pl.multiple_of(jnp.minimum(lo, cap - chunk), SUBLANES)
            rows = lax.broadcasted_iota(jnp.int32, (chunk, tb), 0) + st
            hit = jnp.where(rows == slot, jnp.where(rows >= lo, 1.0, 0.0), 0.0)
            tok_ref[k, pl.ds(st, chunk), :] += jnp.sum(hit * token, axis=1,
                                                       keepdims=True).astype(jnp.int32)
            gw_ref[k, pl.ds(st, chunk), :] += jnp.sum(hit * aff, axis=1, keepdims=True)
            return carry

        lax.fori_loop(0, (end - first + chunk - 1) // chunk, body, 0)


def _pick_list(base_flat, slot3, aff3, *, cap, tb, chunk, experts_per_step):
    ne, _, n = slot3.shape
    nblk = n // tb
    eps = experts_per_step
    assert cap >= chunk and (cap - chunk) % SUBLANES == 0 and chunk % SUBLANES == 0
    kern = functools.partial(_pick_list_kernel, cap=cap, nblk=nblk, chunk=chunk)
    return pl.pallas_call(
        kern,
        grid_spec=pltpu.PrefetchScalarGridSpec(
            num_scalar_prefetch=1,
            grid=(ne // eps, nblk),
            in_specs=[pl.BlockSpec((eps, 1, tb), lambda e, b, base: (e, 0, b)),
                      pl.BlockSpec((eps, 1, tb), lambda e, b, base: (e, 0, b))],
            out_specs=[pl.BlockSpec((eps, cap, 1), lambda e, b, base: (e, 0, 0)),
                       pl.BlockSpec((eps, cap, 1), lambda e, b, base: (e, 0, 0))],
        ),
        out_shape=[jax.ShapeDtypeStruct((ne, cap, 1), jnp.int32),
                   jax.ShapeDtypeStruct((ne, cap, 1), F32)],
        compiler_params=_params("parallel", "arbitrary"),
        name="expert_pick_list",
    )(base_flat, slot3, aff3)


def _row_gather_sparsecore(table, rows, *, window):
    num, width = rows.shape[0], table.shape[1]
    mesh = plsc.VectorSubcoreMesh(core_axis_name="c", subcore_axis_name="s")

    @pl.kernel(out_type=jax.ShapeDtypeStruct((num, width), table.dtype), mesh=mesh,
               name="expert_row_gather")
    def gather(table_hbm, rows_hbm, out_hbm):
        def body(rows_vmem, out_vmem):
            pltpu.sync_copy(table_hbm.at[rows_vmem.at[0]], out_vmem)

        pltpu.emit_pipeline(
            body,
            grid=(num // window,),
            in_specs=[pl.BlockSpec((1, window), index_map=lambda i: (0, i))],
            out_specs=[pl.BlockSpec((window, width), index_map=lambda i: (i, 0))],
            core_axis_name=("c", "s"),
            dimension_semantics=(pltpu.PARALLEL,),
        )(rows_hbm, out_hbm)

    return gather(table, rows.reshape(1, num))


def _gather(tok, hp, *, split, window):
    ne, cap, _ = tok.shape
    n, words = hp.shape
    piece = jnp.arange(split, dtype=jnp.int32)
    rows = (tok.reshape(ne * cap, 1) * split + piece[None, :]).reshape(ne * cap * split)
    out = _row_gather_sparsecore(hp.reshape(n * split, words // split), rows, window=window)
    return out.reshape(ne, cap, words)


def _ffn_kernel(xp_ref, gw_ref, wg_ref, wu_ref, wd_ref, ye_ref, xe_ref, hid_ref, *, nt_up, tf):
    j = pl.program_id(1)

    @pl.when(j == 0)
    def _():
        half = xp_ref.shape[1]
        lo, hi = _unpack_bf16_pair(xp_ref[...])
        xe_ref[:, :half] = lo
        xe_ref[:, half:] = hi

    @pl.when(j < nt_up)
    def _():
        x = xe_ref[...]
        g = jnp.dot(x, wg_ref[...].astype(BF16), preferred_element_type=F32)
        u = jnp.dot(x, wu_ref[...].astype(BF16), preferred_element_type=F32)
        off = pl.multiple_of(j * tf, tf)
        hid_ref[:, pl.ds(off, tf)] = ((g * _sigmoid(g)) * u).astype(hid_ref.dtype)

    @pl.when(j >= nt_up)
    def _():
        y = jnp.dot(hid_ref[...], wd_ref[...].astype(BF16), preferred_element_type=F32)
        ye_ref[...] = (y * gw_ref[...]).astype(ye_ref.dtype)


def _ffn(xp, gw, wg, wu, wd, tf, tn):
    ne, cap, half = xp.shape
    d, f = wg.shape[1], wg.shape[2]
    assert d == 2 * half
    nt_up = f // tf
    nt_dn = d // tn
    kern = functools.partial(_ffn_kernel, nt_up=nt_up, tf=tf)
    return pl.pallas_call(
        kern,
        grid=(ne, nt_up + nt_dn),
        in_specs=[pl.BlockSpec((None, cap, half), lambda e, j: (e, 0, 0),
                               pipeline_mode=pl.Buffered(1)),
                  pl.BlockSpec((None, cap, 1), lambda e, j: (e, 0, 0)),
                  pl.BlockSpec((None, d, tf), lambda e, j: (e, 0, jnp.minimum(j, nt_up - 1))),
                  pl.BlockSpec((None, d, tf), lambda e, j: (e, 0, jnp.minimum(j, nt_up - 1))),
                  pl.BlockSpec((None, f, tn), lambda e, j: (e, 0, jnp.maximum(j - nt_up, 0)))],
        out_specs=pl.BlockSpec((None, cap, tn), lambda e, j: (e, 0, jnp.maximum(j - nt_up, 0))),
        out_shape=jax.ShapeDtypeStruct((ne, cap, d), BF16),
        scratch_shapes=[pltpu.VMEM((cap, d), BF16), pltpu.VMEM((cap, f), BF16)],
        compiler_params=_params("parallel", "arbitrary"),
        name="expert_ffn",
    )(xp, gw, wg, wu, wd)


def _expert_column(ref, e):
    lane = lax.broadcasted_iota(jnp.int32, ref.shape, 1)
    return jnp.sum(jnp.where(lane == e, ref[...], 0), axis=1, keepdims=True)


def _combine_any_kernel(base_ref, slot_ref, x_ref, ya_ref, yb_ref, g_ref, o_ref, *, cap, nblk):
    b = pl.program_id(0)
    e = pl.program_id(1)
    ne = pl.num_programs(1)

    @pl.when(e == 0)
    def _():
        o_ref[...] = x_ref[...]

    tb = x_ref.shape[0]
    blk = ya_ref.shape[0]
    start = jnp.minimum(base_ref[e * (nblk + 1) + b] // blk, cap // blk - 1) * blk
    slot = _expert_column(slot_ref, e)
    cols = lax.broadcasted_iota(jnp.int32, (tb, blk), 1) + start
    oh_a = jnp.where(cols == slot, 1.0, 0.0).astype(BF16)
    oh_b = jnp.where(cols + blk == slot, 1.0, 0.0).astype(BF16)
    o_ref[...] += (jnp.dot(oh_a, ya_ref[...], preferred_element_type=F32)
                   + jnp.dot(oh_b, yb_ref[...], preferred_element_type=F32))

    @pl.when(e == ne - 1)
    def _():
        o_ref[...] = _rms_scale(o_ref[...]) * g_ref[...]


def _combine_any(base_flat, slot_tok, x2, ye, g_final, *, cap, tb):
    n, d = x2.shape
    ne = ye.shape[0]
    nblk = n // tb
    blk = tb
    last = cap // blk - 1

    def ya_map(b, e, base):
        return (e, jnp.minimum(base[e * (nblk + 1) + b] // blk, last), 0)

    def yb_map(b, e, base):
        return (e, jnp.minimum(jnp.minimum(base[e * (nblk + 1) + b] // blk, last) + 1, last), 0)

    kern = functools.partial(_combine_any_kernel, cap=cap, nblk=nblk)
    return pl.pallas_call(
        kern,
        grid_spec=pltpu.PrefetchScalarGridSpec(
            num_scalar_prefetch=1,
            grid=(nblk, ne),
            in_specs=[pl.BlockSpec((tb, ne), lambda b, e, base: (b, 0)),
                      pl.BlockSpec((tb, d), lambda b, e, base: (b, 0)),
                      pl.BlockSpec((None, blk, d), ya_map),
                      pl.BlockSpec((None, blk, d), yb_map),
                      pl.BlockSpec((1, d), lambda b, e, base: (0, 0))],
            out_specs=pl.BlockSpec((tb, d), lambda b, e, base: (b, 0)),
        ),
        out_shape=jax.ShapeDtypeStruct((n, d), F32),
        compiler_params=_params("parallel", "arbitrary"),
        name="expert_combine_any",
    )(base_flat, slot_tok, x2, ye, ye, g_final.reshape(1, d))


def _window_start(base, e, b, *, nblk, cap, win):
    tile = jnp.minimum(base[e * (nblk + 1) + b] // BF16_ROW_TILE, (cap - win) // BF16_ROW_TILE)
    return tile * BF16_ROW_TILE


def _combine_windows_kernel(base_ref, slot_ref, x_ref, g_ref, *rest, cap, nblk, win):
    y_refs, o_ref = rest[:-1], rest[-1]
    b = pl.program_id(0)
    s = pl.program_id(1)
    per = len(y_refs)

    @pl.when(s == 0)
    def _():
        o_ref[...] = x_ref[...]

    tb = x_ref.shape[0]
    col = lax.broadcasted_iota(jnp.int32, (tb, win), 1)
    onehots = []
    for k in range(per):
        e = s * per + k
        start = _window_start(base_ref, e, b, nblk=nblk, cap=cap, win=win)
        onehots.append(jnp.where(col + start == _expert_column(slot_ref, e), 1.0, 0.0).astype(BF16))
    y = jnp.concatenate([r[...] for r in y_refs], axis=0)
    o_ref[...] += jnp.dot(jnp.concatenate(onehots, axis=1), y, preferred_element_type=F32)

    @pl.when(s == pl.num_programs(1) - 1)
    def _():
        o_ref[...] = _rms_scale(o_ref[...]) * g_ref[...]


def _combine_windows(base_flat, slot_tok, x2, ye, g_final, *, cap, tb, win, experts_per_step):
    n, d = x2.shape
    ne = ye.shape[0]
    nblk = n // tb
    per = experts_per_step
    assert ne % per == 0

    def y_map(k):
        def index(b, s, base):
            e = s * per + k
            return (e, _window_start(base, e, b, nblk=nblk, cap=cap, win=win), 0)
        return index

    kern = functools.partial(_combine_windows_kernel, cap=cap, nblk=nblk, win=win)
    return pl.pallas_call(
        kern,
        grid_spec=pltpu.PrefetchScalarGridSpec(
            num_scalar_prefetch=1,
            grid=(nblk, ne // per),
            in_specs=[pl.BlockSpec((tb, ne), lambda b, s, base: (b, 0)),
                      pl.BlockSpec((tb, d), lambda b, s, base: (b, 0)),
                      pl.BlockSpec((1, d), lambda b, s, base: (0, 0))]
                     + [pl.BlockSpec((None, pl.Element(win), pl.Element(d)), y_map(k))
                        for k in range(per)],
            out_specs=pl.BlockSpec((tb, d), lambda b, s, base: (b, 0)),
        ),
        out_shape=jax.ShapeDtypeStruct((n, d), F32),
        compiler_params=_params("parallel", "arbitrary"),
        name="expert_combine_windows",
    )(base_flat, slot_tok, x2, g_final.reshape(1, d), *([ye] * per))


def _combine(base, slot_tok, x2, ye, g_final, *, cap, tb, win):
    nblk = x2.shape[0] // tb
    base_flat = base[:, :nblk + 1].reshape(-1)
    first = (base[:, :nblk] // BF16_ROW_TILE) * BF16_ROW_TILE
    fits = jnp.all(base[:, 1:nblk + 1] - jnp.minimum(first, cap - win) <= win)
    return lax.cond(
        fits,
        lambda: _combine_windows(base_flat, slot_tok, x2, ye, g_final, cap=cap, tb=tb, win=win,
                                 experts_per_step=8),
        lambda: _combine_any(base_flat, slot_tok, x2, ye, g_final, cap=cap, tb=tb))


def _encode(x3, mem3, p):
    bsz, seq, d = x3.shape
    n = bsz * seq
    n_mem = mem3.shape[1]
    x = x3.reshape(n, d)
    mem = mem3.reshape(bsz * n_mem, d)

    gmlp_w = p["gmlp_norm_g"].shape[0]
    conv_w = p["conv_w"].shape[1]
    xattn_w = p["w_mem_kv"].shape[1] // 2
    off_cb = 2 * gmlp_w
    off_g = off_cb + 3 * conv_w + xattn_w
    ne = p["w_router"].shape[1]
    cap = CAPACITY_FACTOR * n // ne

    def col_tile(cols):
        return next(t for t in (1024, 512, 256, 128) if cols % t == 0 and t <= cols)

    tm = min(1024, n)
    ts = min(256, seq)
    tb = 256 if cap >= 256 + BF16_ROW_TILE else 128

    h = _rmsnorm_cast(x, p["norm_mix_g"], ts)
    proj = _proj(h, p["w_in"], tm, col_tile(p["w_in"].shape[1]), off_cb, off_g, F32)
    m = _rmsnorm_cast(mem, p["mem_norm_g"], n_mem)
    kv = _proj(m, p["w_mem_kv"], n_mem, col_tile(2 * xattn_w), 0, 2 * xattn_w, BF16)
    ycat = _branches(proj, kv, p["w_spatial"], p["b_spatial"], p["gmlp_norm_g"].reshape(1, gmlp_w),
                     p["conv_w"], seq=seq, ts=ts, gmlp_w=gmlp_w, conv_w=conv_w, xattn_w=xattn_w,
                     n_mem=n_mem)
    merged = _merge(ycat, p["w_branch"], proj, tm=tm, tn=min(512, col_tile(d)), off_g=off_g, d=d,
                    gmlp_w=gmlp_w, conv_w=conv_w)
    x2 = _outproj(merged, p["w_out"], x, tm, col_tile(d))

    hp, afft = _router(x2, p["norm_ffn_g"], p["w_router"], ts)
    slot, base = _select(afft, cap, tb)
    nblk = n // tb
    base_flat = base[:, :nblk + 1].reshape(-1)
    tok, gw = _pick_list(base_flat, slot.reshape(ne, 1, n), afft.reshape(ne, 1, n), cap=cap, tb=tb,
                         chunk=PICK_CHUNK_ROWS, experts_per_step=min(8, ne))
    xp = _gather(tok, hp, split=min(SC_ROW_SPLIT, d // 256), window=SC_GATHER_WINDOW)
    ye = _ffn(xp, gw, p["w_gate"], p["w_up"], p["w_down"], min(256, d), min(256, d))
    y = _combine(base, slot.T, x2, ye, p["final_norm_g"], cap=cap, tb=tb,
                 win=min(COMBINE_WINDOW_ROWS, tb))
    return y.reshape(bsz, seq, d)


def kernel(x_prompt, x_sample, mem_prompt, mem_sample, norm_mix_g, w_in, gmlp_norm_g, w_spatial,
           b_spatial, conv_w, mem_norm_g, w_mem_kv, w_branch_a, w_branch_b, w_branch_x, w_out,
           norm_ffn_g, w_router, w_gate, w_up, w_down, final_norm_g):
    assert w_in.shape[0] == 1, "one layer"
    groups, chunk = b_spatial.shape[1], b_spatial.shape[2]
    gc = gmlp_norm_g.shape[1] // groups
    p = {
        "norm_mix_g": norm_mix_g[0],
        "w_in": w_in[0].astype(BF16),
        "gmlp_norm_g": gmlp_norm_g[0],
        "w_spatial": w_spatial[0].astype(BF16),
        "b_spatial": jnp.broadcast_to(b_spatial[0][:, :, None], (groups, chunk, gc)),
        "conv_w": conv_w[0],
        "mem_norm_g": mem_norm_g[0],
        "w_mem_kv": w_mem_kv[0].astype(BF16),
        "w_branch": jnp.concatenate([w_branch_a[0], w_branch_b[0], w_branch_x[0]],
                                    axis=0).astype(BF16),
        "w_out": w_out[0].astype(BF16),
        "norm_ffn_g": norm_ffn_g[0],
        "w_router": w_router[0].astype(BF16),
        "w_gate": w_gate.reshape(w_gate.shape[1:]),
        "w_up": w_up.reshape(w_up.shape[1:]),
        "w_down": w_down.reshape(w_down.shape[1:]),
        "final_norm_g": final_norm_g,
    }
    return (_encode(x_prompt, mem_prompt, p), _encode(x_sample, mem_sample, p))
```

```python
import functools

import jax
import jax.numpy as jnp
from jax import lax
from jax.experimental import pallas as pl
from jax.experimental.pallas import tpu as pltpu
from jax.experimental.pallas import tpu_sc as plsc

EPS = 1e-6
XATTN_HEADS = 4
CAPACITY_FACTOR = 2
BF16_ROW_TILE = 16
SUBLANES = 8
PICK_CHUNK_ROWS = 80
SC_ROW_PIECES = 8
SC_GATHER_WINDOW = 128
COMBINE_WINDOW_ROWS = 128
VMEM_LIMIT_BYTES = 56 * 1024 * 1024

F32 = jnp.float32
BF16 = jnp.bfloat16


def _params(*semantics):
    return pltpu.CompilerParams(dimension_semantics=semantics,
                                vmem_limit_bytes=VMEM_LIMIT_BYTES)


GELU_K = 0.7978845608028654
GELU_CUBIC = 0.044715
TANH_SATURATED = 30.0
MXU_COLS = 256


def _sigmoid(x):
    return 1.0 / (1.0 + jnp.exp(-x))


def _rms_scale(x):
    return x * lax.rsqrt(jnp.mean(x * x, axis=-1, keepdims=True) + EPS)


def _rmsnorm_cast_kernel(x_ref, g_ref, o_ref):
    o_ref[...] = (_rms_scale(x_ref[...]) * g_ref[...]).astype(o_ref.dtype)


def _rmsnorm_cast(x, g, tm):
    n, d = x.shape
    return pl.pallas_call(
        _rmsnorm_cast_kernel,
        grid=(n // tm,),
        in_specs=[pl.BlockSpec((tm, d), lambda i: (i, 0)),
                  pl.BlockSpec((1, d), lambda i: (0, 0))],
        out_specs=pl.BlockSpec((tm, d), lambda i: (i, 0)),
        out_shape=jax.ShapeDtypeStruct((n, d), BF16),
        compiler_params=_params("parallel"),
        name="rmsnorm_cast",
    )(x, g.reshape(1, d))


def _proj_kernel(h_ref, w_ref, o_ref, *, sub, gelu_tiles, sigmoid_from):
    j = pl.program_id(1)
    nsub = o_ref.shape[1] // sub
    h = h_ref[...]
    for s in range(nsub):
        c = j * nsub + s
        is_gelu = c < gelu_tiles
        is_sig = c >= sigmoid_from
        q1 = jnp.where(is_gelu, GELU_K, jnp.where(is_sig, 0.5, 0.0))
        q3 = jnp.where(is_gelu, GELU_K * GELU_CUBIC, 0.0)
        q0 = jnp.where(jnp.logical_or(is_gelu, is_sig), 0.0, TANH_SATURATED)
        p1 = jnp.where(is_sig, 0.0, 1.0)
        p0 = jnp.where(is_sig, 1.0, 0.0)
        cols = slice(s * sub, (s + 1) * sub)
        x = jnp.dot(h, w_ref[:, cols], preferred_element_type=F32)
        gate = 0.5 * (1.0 + jnp.tanh(x * (q1 + q3 * (x * x)) + q0))
        o_ref[:, cols] = ((x * p1 + p0) * gate).astype(o_ref.dtype)


def _proj(h, w, tm, tn, gelu_cols, sigmoid_from_col, out_dtype):
    n, d = h.shape
    cols = w.shape[1]
    sub = min(MXU_COLS, tn)
    kern = functools.partial(_proj_kernel, sub=sub, gelu_tiles=gelu_cols // sub,
                             sigmoid_from=sigmoid_from_col // sub)
    return pl.pallas_call(
        kern,
        grid=(n // tm, cols // tn),
        in_specs=[pl.BlockSpec((tm, d), lambda i, j: (i, 0)),
                  pl.BlockSpec((d, tn), lambda i, j: (0, j))],
        out_specs=pl.BlockSpec((tm, tn), lambda i, j: (i, j)),
        out_shape=jax.ShapeDtypeStruct((n, cols), out_dtype),
        compiler_params=_params("parallel", "arbitrary"),
        name="in_proj",
    )(h, w)


def _branch_kernel(u_ref, v_ref, cb_ref, cc_ref, ch_ref, q_ref, ccp_ref, chp_ref, ccn_ref, chn_ref,
                   kv_ref, wsp_ref, bsp_ref, gg_ref, cw_ref, y_ref, *, seq, chunk, groups, heads):
    i = pl.program_id(0)
    ts = u_ref.shape[0]
    gw = u_ref.shape[1]
    gc = gw // groups
    cw = cb_ref.shape[1]
    xw = q_ref.shape[1]
    hd = xw // heads

    vn = (_rms_scale(v_ref[...]) * gg_ref[...]).astype(BF16)
    for c in range(ts // chunk):
        rows = slice(c * chunk, (c + 1) * chunk)
        for g in range(groups):
            cols = slice(g * gc, (g + 1) * gc)
            mixed = jnp.dot(wsp_ref[g], vn[rows, cols], preferred_element_type=F32) + bsp_ref[g]
            y_ref[rows, cols] = (u_ref[rows, cols] * mixed).astype(y_ref.dtype)

    pos = (i * ts) % seq
    z = cc_ref[...] * ch_ref[...]
    sub = ccp_ref.shape[0]
    z_before = jnp.where(pos == 0, 0.0, ccp_ref[sub - 1:sub, :] * chp_ref[sub - 1:sub, :])
    z_after = jnp.where(pos + ts == seq, 0.0, ccn_ref[0:1, :] * chn_ref[0:1, :])
    row = lax.broadcasted_iota(jnp.int32, z.shape, 0)
    z_prev = jnp.where(row == 0, z_before, pltpu.roll(z, 1, 0))
    z_next = jnp.where(row == ts - 1, z_after, pltpu.roll(z, ts - 1, 0))
    conv = z_prev * cw_ref[0:1, :] + z * cw_ref[1:2, :] + z_next * cw_ref[2:3, :]
    y_ref[:, gw:gw + cw] = (cb_ref[...] * conv).astype(y_ref.dtype)

    scale = hd ** -0.5
    for h in range(heads):
        cols = slice(h * hd, (h + 1) * hd)
        qh = q_ref[:, cols].astype(BF16)
        kh = kv_ref[:, cols]
        vh = kv_ref[:, xw + h * hd: xw + (h + 1) * hd]
        s = lax.dot_general(qh, kh, (((1,), (1,)), ((), ())), preferred_element_type=F32) * scale
        e = jnp.exp(s - jnp.max(s, axis=-1, keepdims=True))
        p = (e / jnp.sum(e, axis=-1, keepdims=True)).astype(BF16)
        yh = jnp.dot(p, vh, preferred_element_type=F32)
        y_ref[:, gw + cw + h * hd: gw + cw + (h + 1) * hd] = yh.astype(y_ref.dtype)


def _branches(proj, kv, wsp, bsp, gg, cw, *, seq, ts, gmlp_w, conv_w, xattn_w, n_mem):
    n = proj.shape[0]
    chunk = wsp.shape[1]
    groups = wsp.shape[0]
    sub = 8
    off_cb = 2 * gmlp_w
    cb_blk = off_cb // conv_w
    q_blk = (off_cb + 3 * conv_w) // xattn_w
    last = n // sub - 1
    tsb = ts // sub

    def prev_map(col):
        return lambda i: (jnp.maximum(i * tsb - 1, 0), col)

    def next_map(col):
        return lambda i: (jnp.minimum((i + 1) * tsb, last), col)

    kern = functools.partial(_branch_kernel, seq=seq, chunk=chunk, groups=groups, heads=XATTN_HEADS)
    return pl.pallas_call(
        kern,
        grid=(n // ts,),
        in_specs=[
            pl.BlockSpec((ts, gmlp_w), lambda i: (i, 0)),
            pl.BlockSpec((ts, gmlp_w), lambda i: (i, 1)),
            pl.BlockSpec((ts, conv_w), lambda i: (i, cb_blk)),
            pl.BlockSpec((ts, conv_w), lambda i: (i, cb_blk + 1)),
            pl.BlockSpec((ts, conv_w), lambda i: (i, cb_blk + 2)),
            pl.BlockSpec((ts, xattn_w), lambda i: (i, q_blk)),
            pl.BlockSpec((sub, conv_w), prev_map(cb_blk + 1)),
            pl.BlockSpec((sub, conv_w), prev_map(cb_blk + 2)),
            pl.BlockSpec((sub, conv_w), next_map(cb_blk + 1)),
            pl.BlockSpec((sub, conv_w), next_map(cb_blk + 2)),
            pl.BlockSpec((n_mem, 2 * xattn_w), lambda i: ((i * ts) // seq, 0)),
            pl.BlockSpec(wsp.shape, lambda i: (0, 0, 0)),
            pl.BlockSpec(bsp.shape, lambda i: (0, 0, 0)),
            pl.BlockSpec((1, gmlp_w), lambda i: (0, 0)),
            pl.BlockSpec(cw.shape, lambda i: (0, 0)),
        ],
        out_specs=pl.BlockSpec((ts, gmlp_w + conv_w + xattn_w), lambda i: (i, 0)),
        out_shape=jax.ShapeDtypeStruct((n, gmlp_w + conv_w + xattn_w), BF16),
        compiler_params=_params("parallel"),
        name="branches",
    )(proj, proj, proj, proj, proj, proj, proj, proj, proj, proj, kv, wsp, bsp, gg, cw)


def _merge_kernel(y_ref, w_ref, ga_ref, gb_ref, gx_ref, o_ref, *, gmlp_w, conv_w, sub):
    ya = y_ref[:, :gmlp_w]
    yb = y_ref[:, gmlp_w:gmlp_w + conv_w]
    yx = y_ref[:, gmlp_w + conv_w:]
    for s in range(o_ref.shape[1] // sub):
        cols = slice(s * sub, (s + 1) * sub)
        a = jnp.dot(ya, w_ref[:gmlp_w, cols], preferred_element_type=F32)
        b = jnp.dot(yb, w_ref[gmlp_w:gmlp_w + conv_w, cols], preferred_element_type=F32)
        x = jnp.dot(yx, w_ref[gmlp_w + conv_w:, cols], preferred_element_type=F32)
        o_ref[:, cols] = (ga_ref[:, cols] * a + gb_ref[:, cols] * b
                          + gx_ref[:, cols] * x).astype(o_ref.dtype)


def _merge(ycat, w_br, proj, *, tm, tn, off_g, d, gmlp_w, conv_w):
    n, k = ycat.shape
    g0 = off_g // tn
    gstep = d // tn
    kern = functools.partial(_merge_kernel, gmlp_w=gmlp_w, conv_w=conv_w, sub=min(MXU_COLS, tn))
    return pl.pallas_call(
        kern,
        grid=(n // tm, d // tn),
        in_specs=[pl.BlockSpec((tm, k), lambda i, j: (i, 0)),
                  pl.BlockSpec((k, tn), lambda i, j: (0, j)),
                  pl.BlockSpec((tm, tn), lambda i, j: (i, g0 + j)),
                  pl.BlockSpec((tm, tn), lambda i, j: (i, g0 + gstep + j)),
                  pl.BlockSpec((tm, tn), lambda i, j: (i, g0 + 2 * gstep + j))],
        out_specs=pl.BlockSpec((tm, tn), lambda i, j: (i, j)),
        out_shape=jax.ShapeDtypeStruct((n, d), BF16),
        compiler_params=_params("parallel", "arbitrary"),
        name="merge",
    )(ycat, w_br, proj, proj, proj)


def _outproj_kernel(m_ref, w_ref, x_ref, o_ref, *, sub):
    m = m_ref[...]
    for s in range(o_ref.shape[1] // sub):
        cols = slice(s * sub, (s + 1) * sub)
        o_ref[:, cols] = x_ref[:, cols] + jnp.dot(m, w_ref[:, cols], preferred_element_type=F32)


def _outproj(merged, w_out, x, tm, tn):
    n, d = x.shape
    return pl.pallas_call(
        functools.partial(_outproj_kernel, sub=min(MXU_COLS, tn)),
        grid=(n // tm, d // tn),
        in_specs=[pl.BlockSpec((tm, d), lambda i, j: (i, 0)),
                  pl.BlockSpec((d, tn), lambda i, j: (0, j)),
                  pl.BlockSpec((tm, tn), lambda i, j: (i, j))],
        out_specs=pl.BlockSpec((tm, tn), lambda i, j: (i, j)),
        out_shape=jax.ShapeDtypeStruct((n, d), F32),
        compiler_params=_params("parallel", "arbitrary"),
        name="out_proj",
    )(merged, w_out, x)


def _pack_bf16_pair(lo, hi):
    lo_bits = pltpu.bitcast(lo.astype(BF16).astype(F32), jnp.uint32)
    hi_bits = pltpu.bitcast(hi.astype(BF16).astype(F32), jnp.uint32)
    return lax.shift_right_logical(lo_bits, jnp.uint32(16)) | (hi_bits & jnp.uint32(0xFFFF0000))


def _unpack_bf16_pair(words):
    lo = pltpu.bitcast(lax.shift_left(words, jnp.uint32(16)), F32)
    hi = pltpu.bitcast(words & jnp.uint32(0xFFFF0000), F32)
    return lo.astype(BF16), hi.astype(BF16)


def _router_kernel(x_ref, g_ref, wrt_ref, hp_ref, afft_ref):
    hf = _rms_scale(x_ref[...]) * g_ref[...]
    half = hf.shape[1] // 2
    words = _pack_bf16_pair(hf[:, :half], hf[:, half:])
    pw = hp_ref.shape[2]
    for c in range(hp_ref.shape[0]):
        hp_ref[c] = words[:, c * pw:(c + 1) * pw]
    lt = lax.dot_general(wrt_ref[...], hf.astype(BF16), (((1,), (1,)), ((), ())),
                         preferred_element_type=F32)
    et = jnp.exp(lt - jnp.max(lt, axis=0, keepdims=True))
    afft_ref[...] = et / jnp.sum(et, axis=0, keepdims=True)


def _router(x2, g, wr, tm, pieces):
    n, d = x2.shape
    ne = wr.shape[1]
    pw = d // 2 // pieces
    return pl.pallas_call(
        _router_kernel,
        grid=(n // tm,),
        in_specs=[pl.BlockSpec((tm, d), lambda i: (i, 0)),
                  pl.BlockSpec((1, d), lambda i: (0, 0)),
                  pl.BlockSpec((ne, d), lambda i: (0, 0))],
        out_specs=[pl.BlockSpec((pieces, tm, pw), lambda i: (0, i, 0)),
                   pl.BlockSpec((ne, tm), lambda i: (0, i))],
        out_shape=[jax.ShapeDtypeStruct((pieces, n, pw), jnp.uint32),
                   jax.ShapeDtypeStruct((ne, n), F32)],
        compiler_params=_params("parallel"),
        name="ffn_norm_router",
    )(x2, g.reshape(1, d), wr.T)


def _select_kernel(aff_ref, slot_ref, base_ref, *, cap, tb):
    ne, n = aff_ref.shape
    bits = pltpu.bitcast(aff_ref[...], jnp.int32)
    capf = jnp.float32(cap)

    def search(k, t):
        cand = t | lax.shift_left(jnp.int32(1), 30 - k)
        cnt = jnp.sum(jnp.where(bits >= cand, 1.0, 0.0), axis=1, keepdims=True)
        return jnp.where(cnt >= capf, cand, t)

    thr = lax.fori_loop(0, 31, search, jnp.zeros((ne, 1), jnp.int32))
    n_gt = jnp.sum(jnp.where(bits > thr, 1.0, 0.0), axis=1, keepdims=True)
    need = capf - n_gt

    r = lax.broadcasted_iota(jnp.int32, (tb, tb), 0)
    c = lax.broadcasted_iota(jnp.int32, (tb, tb), 1)
    tri = jnp.where(r < c, 1.0, 0.0).astype(BF16)
    lane = lax.broadcasted_iota(jnp.int32, base_ref.shape, 1)
    eq_seen = jnp.zeros((ne, 1), F32)
    sel_seen = jnp.zeros((ne, 1), F32)
    base = jnp.zeros(base_ref.shape, F32)
    for b in range(n // tb):
        blk = pltpu.bitcast(aff_ref[:, b * tb:(b + 1) * tb], jnp.int32)
        eq = jnp.where(blk == thr, 1.0, 0.0)
        eq_rank = jnp.dot(eq.astype(BF16), tri, preferred_element_type=F32) + eq_seen
        sel = jnp.where(blk > thr, 1.0, jnp.where(eq_rank < need, eq, 0.0))
        pos = jnp.dot(sel.astype(BF16), tri, preferred_element_type=F32) + sel_seen
        slot_ref[:, b * tb:(b + 1) * tb] = jnp.where(sel > 0.0, pos, -1.0).astype(jnp.int32)
        base = jnp.where(lane == b, sel_seen, base)
        eq_seen = eq_seen + jnp.sum(eq, axis=1, keepdims=True)
        sel_seen = sel_seen + jnp.sum(sel, axis=1, keepdims=True)
    base = jnp.where(lane == n // tb, sel_seen, base)
    base_ref[...] = base.astype(jnp.int32)


def _select(afft, cap, tb):
    ne, n = afft.shape
    lanes = 128
    assert n // tb < lanes
    kern = functools.partial(_select_kernel, cap=cap, tb=tb)
    return pl.pallas_call(
        kern,
        out_shape=[jax.ShapeDtypeStruct((ne, n), jnp.int32),
                   jax.ShapeDtypeStruct((ne, lanes), jnp.int32)],
        compiler_params=pltpu.CompilerParams(vmem_limit_bytes=VMEM_LIMIT_BYTES),
        name="expert_choice_select",
    )(afft)


def _pick_list_kernel(base_ref, slot_ref, aff_ref, tok_ref, gw_ref, *, cap, nblk, chunk):
    eb = pl.program_id(0)
    b = pl.program_id(1)

    @pl.when(b == 0)
    def _():
        tok_ref[...] = jnp.zeros_like(tok_ref)
        gw_ref[...] = jnp.zeros_like(gw_ref)

    tb = slot_ref.shape[2]
    token = (lax.broadcasted_iota(jnp.int32, (1, tb), 1) + b * tb).astype(F32)
    for k in range(tok_ref.shape[0]):
        e = eb * tok_ref.shape[0] + k
        first = (base_ref[e * (nblk + 1) + b] // SUBLANES) * SUBLANES
        end = base_ref[e * (nblk + 1) + b + 1]
        slot = slot_ref[k]
        aff = aff_ref[k]

        def body(c, carry, k=k, first=first, slot=slot, aff=aff):
            lo = first + c * chunk
            st = pl.multiple_of(jnp.minimum(lo, cap - chunk), SUBLANES)
            rows = lax.broadcasted_iota(jnp.int32, (chunk, tb), 0) + st
            hit = jnp.where(rows == slot, jnp.where(rows >= lo, 1.0, 0.0), 0.0)
            tok_ref[k, pl.ds(st, chunk), :] += jnp.sum(hit * token, axis=1,
                                                       keepdims=True).astype(jnp.int32)
            gw_ref[k, pl.ds(st, chunk), :] += jnp.sum(hit * aff, axis=1, keepdims=True)
            return carry

        lax.fori_loop(0, (end - first + chunk - 1) // chunk, body, 0)


def _pick_list(base_flat, slot3, aff3, *, cap, tb, chunk, experts_per_step):
    ne, _, n = slot3.shape
    nblk = n // tb
    eps = experts_per_step
    assert cap >= chunk and (cap - chunk) % SUBLANES == 0 and chunk % SUBLANES == 0
    kern = functools.partial(_pick_list_kernel, cap=cap, nblk=nblk, chunk=chunk)
    return pl.pallas_call(
        kern,
        grid_spec=pltpu.PrefetchScalarGridSpec(
            num_scalar_prefetch=1,
            grid=(ne // eps, nblk),
            in_specs=[pl.BlockSpec((eps, 1, tb), lambda e, b, base: (e, 0, b)),
                      pl.BlockSpec((eps, 1, tb), lambda e, b, base: (e, 0, b))],
            out_specs=[pl.BlockSpec((eps, cap, 1), lambda e, b, base: (e, 0, 0)),
                       pl.BlockSpec((eps, cap, 1), lambda e, b, base: (e, 0, 0))],
        ),
        out_shape=[jax.ShapeDtypeStruct((ne, cap, 1), jnp.int32),
                   jax.ShapeDtypeStruct((ne, cap, 1), F32)],
        compiler_params=_params("parallel", "arbitrary"),
        name="expert_pick_list",
    )(base_flat, slot3, aff3)


def _row_gather_sparsecore(table, rows, *, window):
    num, width = rows.shape[0], table.shape[1]
    mesh = plsc.VectorSubcoreMesh(core_axis_name="c", subcore_axis_name="s")

    @pl.kernel(out_type=jax.ShapeDtypeStruct((num, width), table.dtype), mesh=mesh,
               name="expert_row_gather")
    def gather(table_hbm, rows_hbm, out_hbm):
        def body(rows_vmem, out_vmem):
            pltpu.sync_copy(table_hbm.at[rows_vmem.at[0]], out_vmem)

        pltpu.emit_pipeline(
            body,
            grid=(num // window,),
            in_specs=[pl.BlockSpec((1, window), index_map=lambda i: (0, i))],
            out_specs=[pl.BlockSpec((window, width), index_map=lambda i: (i, 0))],
            core_axis_name=("c", "s"),
            dimension_semantics=(pltpu.PARALLEL,),
        )(rows_hbm, out_hbm)

    return gather(table, rows.reshape(1, num))


def _gather(tok, hp, *, window):
    ne, cap, _ = tok.shape
    pieces, n, pw = hp.shape
    rows = (jnp.arange(pieces, dtype=jnp.int32)[:, None] * n + tok.reshape(1, ne * cap)).reshape(-1)
    out = _row_gather_sparsecore(hp.reshape(pieces * n, pw), rows, window=window)
    return out.reshape(pieces, ne, cap, pw)


def _ffn_kernel(xp_ref, gw_ref, wg_ref, wu_ref, wd_ref, ye_ref, xe_ref, hid_ref, *, nt_up, tf):
    j = pl.program_id(1)

    @pl.when(j == 0)
    def _():
        pieces, _, pw = xp_ref.shape
        for c in range(pieces):
            lo, hi = _unpack_bf16_pair(xp_ref[c])
            xe_ref[:, c * pw:(c + 1) * pw] = lo
            xe_ref[:, (pieces + c) * pw:(pieces + c + 1) * pw] = hi

    @pl.when(j < nt_up)
    def _():
        x = xe_ref[...]
        g = jnp.dot(x, wg_ref[...].astype(BF16), preferred_element_type=F32)
        u = jnp.dot(x, wu_ref[...].astype(BF16), preferred_element_type=F32)
        off = pl.multiple_of(j * tf, tf)
        hid_ref[:, pl.ds(off, tf)] = ((g * _sigmoid(g)) * u).astype(hid_ref.dtype)

    @pl.when(j >= nt_up)
    def _():
        y = jnp.dot(hid_ref[...], wd_ref[...].astype(BF16), preferred_element_type=F32)
        ye_ref[...] = (y * gw_ref[...]).astype(ye_ref.dtype)


def _ffn(xp, gw, wg, wu, wd, tf, tn):
    pieces, ne, cap, pw = xp.shape
    d, f = wg.shape[1], wg.shape[2]
    assert d == 2 * pieces * pw
    nt_up = f // tf
    nt_dn = d // tn
    kern = functools.partial(_ffn_kernel, nt_up=nt_up, tf=tf)
    return pl.pallas_call(
        kern,
        grid=(ne, nt_up + nt_dn),
        in_specs=[pl.BlockSpec((pieces, None, cap, pw), lambda e, j: (0, e, 0, 0),
                               pipeline_mode=pl.Buffered(1)),
                  pl.BlockSpec((None, cap, 1), lambda e, j: (e, 0, 0)),
                  pl.BlockSpec((None, d, tf), lambda e, j: (e, 0, jnp.minimum(j, nt_up - 1))),
                  pl.BlockSpec((None, d, tf), lambda e, j: (e, 0, jnp.minimum(j, nt_up - 1))),
                  pl.BlockSpec((None, f, tn), lambda e, j: (e, 0, jnp.maximum(j - nt_up, 0)))],
        out_specs=pl.BlockSpec((None, cap, tn), lambda e, j: (e, 0, jnp.maximum(j - nt_up, 0))),
        out_shape=jax.ShapeDtypeStruct((ne, cap, d), BF16),
        scratch_shapes=[pltpu.VMEM((cap, d), BF16), pltpu.VMEM((cap, f), BF16)],
        compiler_params=_params("parallel", "arbitrary"),
        name="expert_ffn",
    )(xp, gw, wg, wu, wd)


def _expert_column(ref, e):
    lane = lax.broadcasted_iota(jnp.int32, ref.shape, 1)
    return jnp.sum(jnp.where(lane == e, ref[...], 0), axis=1, keepdims=True)


def _combine_any_kernel(base_ref, slot_ref, x_ref, ya_ref, yb_ref, g_ref, o_ref, *, cap, nblk):
    b = pl.program_id(0)
    e = pl.program_id(1)
    ne = pl.num_programs(1)

    @pl.when(e == 0)
    def _():
        o_ref[...] = x_ref[...]

    tb = x_ref.shape[0]
    blk = ya_ref.shape[0]
    start = jnp.minimum(base_ref[e * (nblk + 1) + b] // blk, cap // blk - 1) * blk
    slot = _expert_column(slot_ref, e)
    cols = lax.broadcasted_iota(jnp.int32, (tb, blk), 1) + start
    oh_a = jnp.where(cols == slot, 1.0, 0.0).astype(BF16)
    oh_b = jnp.where(cols + blk == slot, 1.0, 0.0).astype(BF16)
    o_ref[...] += (jnp.dot(oh_a, ya_ref[...], preferred_element_type=F32)
                   + jnp.dot(oh_b, yb_ref[...], preferred_element_type=F32))

    @pl.when(e == ne - 1)
    def _():
        o_ref[...] = _rms_scale(o_ref[...]) * g_ref[...]


def _combine_any(base_flat, slot_tok, x2, ye, g_final, *, cap, tb):
    n, d = x2.shape
    ne = ye.shape[0]
    nblk = n // tb
    blk = tb
    last = cap // blk - 1

    def ya_map(b, e, base):
        return (e, jnp.minimum(base[e * (nblk + 1) + b] // blk, last), 0)

    def yb_map(b, e, base):
        return (e, jnp.minimum(jnp.minimum(base[e * (nblk + 1) + b] // blk, last) + 1, last), 0)

    kern = functools.partial(_combine_any_kernel, cap=cap, nblk=nblk)
    return pl.pallas_call(
        kern,
        grid_spec=pltpu.PrefetchScalarGridSpec(
            num_scalar_prefetch=1,
            grid=(nblk, ne),
            in_specs=[pl.BlockSpec((tb, ne), lambda b, e, base: (b, 0)),
                      pl.BlockSpec((tb, d), lambda b, e, base: (b, 0)),
                      pl.BlockSpec((None, blk, d), ya_map),
                      pl.BlockSpec((None, blk, d), yb_map),
                      pl.BlockSpec((1, d), lambda b, e, base: (0, 0))],
            out_specs=pl.BlockSpec((tb, d), lambda b, e, base: (b, 0)),
        ),
        out_shape=jax.ShapeDtypeStruct((n, d), F32),
        compiler_params=_params("parallel", "arbitrary"),
        name="expert_combine_any",
    )(base_flat, slot_tok, x2, ye, ye, g_final.reshape(1, d))


def _window_start(base, e, b, *, nblk, cap, win):
    tile = jnp.minimum(base[e * (nblk + 1) + b] // BF16_ROW_TILE, (cap - win) // BF16_ROW_TILE)
    return tile * BF16_ROW_TILE


def _combine_windows_kernel(base_ref, slot_ref, x_ref, g_ref, *rest, cap, nblk, win):
    y_refs, o_ref = rest[:-1], rest[-1]
    b = pl.program_id(0)
    s = pl.program_id(1)
    per = len(y_refs)

    @pl.when(s == 0)
    def _():
        o_ref[...] = x_ref[...]

    tb = x_ref.shape[0]
    col = lax.broadcasted_iota(jnp.int32, (tb, win), 1)
    onehots = []
    for k in range(per):
        e = s * per + k
        start = _window_start(base_ref, e, b, nblk=nblk, cap=cap, win=win)
        onehots.append(jnp.where(col + start == _expert_column(slot_ref, e), 1.0, 0.0).astype(BF16))
    y = jnp.concatenate([r[...] for r in y_refs], axis=0)
    o_ref[...] += jnp.dot(jnp.concatenate(onehots, axis=1), y, preferred_element_type=F32)

    @pl.when(s == pl.num_programs(1) - 1)
    def _():
        o_ref[...] = _rms_scale(o_ref[...]) * g_ref[...]


def _combine_windows(base_flat, slot_tok, x2, ye, g_final, *, cap, tb, win, experts_per_step):
    n, d = x2.shape
    ne = ye.shape[0]
    nblk = n // tb
    per = experts_per_step
    assert ne % per == 0

    def y_map(k):
        def index(b, s, base):
            e = s * per + k
            return (e, _window_start(base, e, b, nblk=nblk, cap=cap, win=win), 0)
        return index

    kern = functools.partial(_combine_windows_kernel, cap=cap, nblk=nblk, win=win)
    return pl.pallas_call(
        kern,
        grid_spec=pltpu.PrefetchScalarGridSpec(
            num_scalar_prefetch=1,
            grid=(nblk, ne // per),
            in_specs=[pl.BlockSpec((tb, ne), lambda b, s, base: (b, 0)),
                      pl.BlockSpec((tb, d), lambda b, s, base: (b, 0)),
                      pl.BlockSpec((1, d), lambda b, s, base: (0, 0))]
                     + [pl.BlockSpec((None, pl.Element(win), pl.Element(d)), y_map(k))
                        for k in range(per)],
            out_specs=pl.BlockSpec((tb, d), lambda b, s, base: (b, 0)),
        ),
        out_shape=jax.ShapeDtypeStruct((n, d), F32),
        compiler_params=_params("parallel", "arbitrary"),
        name="expert_combine_windows",
    )(base_flat, slot_tok, x2, g_final.reshape(1, d), *([ye] * per))


def _combine(base, slot_tok, x2, ye, g_final, *, cap, tb, win):
    nblk = x2.shape[0] // tb
    base_flat = base[:, :nblk + 1].reshape(-1)
    first = (base[:, :nblk] // BF16_ROW_TILE) * BF16_ROW_TILE
    fits = jnp.all(base[:, 1:nblk + 1] - jnp.minimum(first, cap - win) <= win)
    return lax.cond(
        fits,
        lambda: _combine_windows(base_flat, slot_tok, x2, ye, g_final, cap=cap, tb=tb, win=win,
                                 experts_per_step=8),
        lambda: _combine_any(base_flat, slot_tok, x2, ye, g_final, cap=cap, tb=tb))


def _encode(x3, mem3, p):
    bsz, seq, d = x3.shape
    n = bsz * seq
    n_mem = mem3.shape[1]
    x = x3.reshape(n, d)
    mem = mem3.reshape(bsz * n_mem, d)

    gmlp_w = p["gmlp_norm_g"].shape[0]
    conv_w = p["conv_w"].shape[1]
    xattn_w = p["w_mem_kv"].shape[1] // 2
    off_cb = 2 * gmlp_w
    off_g = off_cb + 3 * conv_w + xattn_w
    ne = p["w_router"].shape[1]
    cap = CAPACITY_FACTOR * n // ne

    def col_tile(cols):
        return next(t for t in (1024, 512, 256, 128) if cols % t == 0 and t <= cols)

    tm = min(1024, n)
    ts = min(256, seq)
    tb = 256 if cap >= 256 + BF16_ROW_TILE else 128

    h = _rmsnorm_cast(x, p["norm_mix_g"], ts)
    proj = _proj(h, p["w_in"], tm, col_tile(p["w_in"].shape[1]), off_cb, off_g, F32)
    m = _rmsnorm_cast(mem, p["mem_norm_g"], n_mem)
    kv = _proj(m, p["w_mem_kv"], n_mem, col_tile(2 * xattn_w), 0, 2 * xattn_w, BF16)
    ycat = _branches(proj, kv, p["w_spatial"], p["b_spatial"], p["gmlp_norm_g"].reshape(1, gmlp_w),
                     p["conv_w"], seq=seq, ts=ts, gmlp_w=gmlp_w, conv_w=conv_w, xattn_w=xattn_w,
                     n_mem=n_mem)
    merged = _merge(ycat, p["w_branch"], proj, tm=tm, tn=min(512, col_tile(d)), off_g=off_g, d=d,
                    gmlp_w=gmlp_w, conv_w=conv_w)
    x2 = _outproj(merged, p["w_out"], x, tm, col_tile(d))

    hp, afft = _router(x2, p["norm_ffn_g"], p["w_router"], ts, min(SC_ROW_PIECES, d // 256))
    slot, base = _select(afft, cap, tb)
    nblk = n // tb
    base_flat = base[:, :nblk + 1].reshape(-1)
    tok, gw = _pick_list(base_flat, slot.reshape(ne, 1, n), afft.reshape(ne, 1, n), cap=cap, tb=tb,
                         chunk=PICK_CHUNK_ROWS, experts_per_step=min(8, ne))
    xp = _gather(tok, hp, window=SC_GATHER_WINDOW)
    ye = _ffn(xp, gw, p["w_gate"], p["w_up"], p["w_down"], min(256, d), min(256, d))
    y = _combine(base, slot.T, x2, ye, p["final_norm_g"], cap=cap, tb=tb,
                 win=min(COMBINE_WINDOW_ROWS, tb))
    return y.reshape(bsz, seq, d)


def kernel(x_prompt, x_sample, mem_prompt, mem_sample, norm_mix_g, w_in, gmlp_norm_g, w_spatial,
           b_spatial, conv_w, mem_norm_g, w_mem_kv, w_branch_a, w_branch_b, w_branch_x, w_out,
           norm_ffn_g, w_router, w_gate, w_up, w_down, final_norm_g):
    assert w_in.shape[0] == 1, "one layer"
    groups, chunk = b_spatial.shape[1], b_spatial.shape[2]
    gc = gmlp_norm_g.shape[1] // groups
    p = {
        "norm_mix_g": norm_mix_g[0],
        "w_in": w_in[0].astype(BF16),
        "gmlp_norm_g": gmlp_norm_g[0],
        "w_spatial": w_spatial[0].astype(BF16),
        "b_spatial": jnp.broadcast_to(b_spatial[0][:, :, None], (groups, chunk, gc)),
        "conv_w": conv_w[0],
        "mem_norm_g": mem_norm_g[0],
        "w_mem_kv": w_mem_kv[0].astype(BF16),
        "w_branch": jnp.concatenate([w_branch_a[0], w_branch_b[0], w_branch_x[0]],
                                    axis=0).astype(BF16),
        "w_out": w_out[0].astype(BF16),
        "norm_ffn_g": norm_ffn_g[0],
        "w_router": w_router[0].astype(BF16),
        "w_gate": w_gate.reshape(w_gate.shape[1:]),
        "w_up": w_up.reshape(w_up.shape[1:]),
        "w_down": w_down.reshape(w_down.shape[1:]),
        "final_norm_g": final_norm_g,
    }
    return (_encode(x_prompt, mem_prompt, p), _encode(x_sample, mem_sample, p))
```

```python
import functools

import jax
import jax.numpy as jnp
from jax import lax
from jax.experimental import pallas as pl
from jax.experimental.pallas import tpu as pltpu
from jax.experimental.pallas import tpu_sc as plsc

EPS = 1e-6
XATTN_HEADS = 4
CAPACITY_FACTOR = 2
BF16_ROW_TILE = 16
SUBLANES = 8
PICK_CHUNK_ROWS = 80
SC_ROW_PIECES = 8
SC_GATHER_WINDOW = 128
COMBINE_WINDOW_ROWS = 128
VMEM_LIMIT_BYTES = 56 * 1024 * 1024
FFN_VMEM_LIMIT_BYTES = 62 * 1024 * 1024
FFN_UP_COLS = 128

F32 = jnp.float32
BF16 = jnp.bfloat16


def _params(*semantics):
    return pltpu.CompilerParams(dimension_semantics=semantics,
                                vmem_limit_bytes=VMEM_LIMIT_BYTES)


GELU_K = 0.7978845608028654
GELU_CUBIC = 0.044715
TANH_SATURATED = 30.0
MXU_COLS = 256


def _sigmoid(x):
    return 1.0 / (1.0 + jnp.exp(-x))


def _rms_scale(x):
    return x * lax.rsqrt(jnp.mean(x * x, axis=-1, keepdims=True) + EPS)


def _rmsnorm_cast_kernel(x_ref, g_ref, o_ref):
    o_ref[...] = (_rms_scale(x_ref[...]) * g_ref[...]).astype(o_ref.dtype)


def _rmsnorm_cast(x, g, tm):
    n, d = x.shape
    return pl.pallas_call(
        _rmsnorm_cast_kernel,
        grid=(n // tm,),
        in_specs=[pl.BlockSpec((tm, d), lambda i: (i, 0)),
                  pl.BlockSpec((1, d), lambda i: (0, 0))],
        out_specs=pl.BlockSpec((tm, d), lambda i: (i, 0)),
        out_shape=jax.ShapeDtypeStruct((n, d), BF16),
        compiler_params=_params("parallel"),
        name="rmsnorm_cast",
    )(x, g.reshape(1, d))


def _proj_kernel(h_ref, w_ref, o_ref, *, sub, gelu_tiles, sigmoid_from):
    j = pl.program_id(1)
    nsub = o_ref.shape[1] // sub
    h = h_ref[...]
    for s in range(nsub):
        c = j * nsub + s
        is_gelu = c < gelu_tiles
        is_sig = c >= sigmoid_from
        q1 = jnp.where(is_gelu, GELU_K, jnp.where(is_sig, 0.5, 0.0))
        q3 = jnp.where(is_gelu, GELU_K * GELU_CUBIC, 0.0)
        q0 = jnp.where(jnp.logical_or(is_gelu, is_sig), 0.0, TANH_SATURATED)
        p1 = jnp.where(is_sig, 0.0, 1.0)
        p0 = jnp.where(is_sig, 1.0, 0.0)
        cols = slice(s * sub, (s + 1) * sub)
        x = jnp.dot(h, w_ref[:, cols], preferred_element_type=F32)
        gate = 0.5 * (1.0 + jnp.tanh(x * (q1 + q3 * (x * x)) + q0))
        o_ref[:, cols] = ((x * p1 + p0) * gate).astype(o_ref.dtype)


def _proj(h, w, tm, tn, gelu_cols, sigmoid_from_col, out_dtype):
    n, d = h.shape
    cols = w.shape[1]
    sub = min(MXU_COLS, tn)
    kern = functools.partial(_proj_kernel, sub=sub, gelu_tiles=gelu_cols // sub,
                             sigmoid_from=sigmoid_from_col // sub)
    return pl.pallas_call(
        kern,
        grid=(n // tm, cols // tn),
        in_specs=[pl.BlockSpec((tm, d), lambda i, j: (i, 0)),
                  pl.BlockSpec((d, tn), lambda i, j: (0, j))],
        out_specs=pl.BlockSpec((tm, tn), lambda i, j: (i, j)),
        out_shape=jax.ShapeDtypeStruct((n, cols), out_dtype),
        compiler_params=_params("parallel", "arbitrary"),
        name="in_proj",
    )(h, w)


def _branch_kernel(u_ref, v_ref, cb_ref, cc_ref, ch_ref, q_ref, ccp_ref, chp_ref, ccn_ref, chn_ref,
                   kv_ref, wsp_ref, bsp_ref, gg_ref, cw_ref, y_ref, *, seq, chunk, groups, heads):
    i = pl.program_id(0)
    ts = u_ref.shape[0]
    gw = u_ref.shape[1]
    gc = gw // groups
    cw = cb_ref.shape[1]
    xw = q_ref.shape[1]
    hd = xw // heads

    vn = (_rms_scale(v_ref[...]) * gg_ref[...]).astype(BF16)
    for c in range(ts // chunk):
        rows = slice(c * chunk, (c + 1) * chunk)
        for g in range(groups):
            cols = slice(g * gc, (g + 1) * gc)
            mixed = jnp.dot(wsp_ref[g], vn[rows, cols], preferred_element_type=F32) + bsp_ref[g]
            y_ref[rows, cols] = (u_ref[rows, cols] * mixed).astype(y_ref.dtype)

    pos = (i * ts) % seq
    z = cc_ref[...] * ch_ref[...]
    sub = ccp_ref.shape[0]
    z_before = jnp.where(pos == 0, 0.0, ccp_ref[sub - 1:sub, :] * chp_ref[sub - 1:sub, :])
    z_after = jnp.where(pos + ts == seq, 0.0, ccn_ref[0:1, :] * chn_ref[0:1, :])
    row = lax.broadcasted_iota(jnp.int32, z.shape, 0)
    z_prev = jnp.where(row == 0, z_before, pltpu.roll(z, 1, 0))
    z_next = jnp.where(row == ts - 1, z_after, pltpu.roll(z, ts - 1, 0))
    conv = z_prev * cw_ref[0:1, :] + z * cw_ref[1:2, :] + z_next * cw_ref[2:3, :]
    y_ref[:, gw:gw + cw] = (cb_ref[...] * conv).astype(y_ref.dtype)

    scale = hd ** -0.5
    for h in range(heads):
        cols = slice(h * hd, (h + 1) * hd)
        qh = q_ref[:, cols].astype(BF16)
        kh = kv_ref[:, cols]
        vh = kv_ref[:, xw + h * hd: xw + (h + 1) * hd]
        s = lax.dot_general(qh, kh, (((1,), (1,)), ((), ())), preferred_element_type=F32) * scale
        e = jnp.exp(s - jnp.max(s, axis=-1, keepdims=True))
        p = (e / jnp.sum(e, axis=-1, keepdims=True)).astype(BF16)
        yh = jnp.dot(p, vh, preferred_element_type=F32)
        y_ref[:, gw + cw + h * hd: gw + cw + (h + 1) * hd] = yh.astype(y_ref.dtype)


def _branches(proj, kv, wsp, bsp, gg, cw, *, seq, ts, gmlp_w, conv_w, xattn_w, n_mem):
    n = proj.shape[0]
    chunk = wsp.shape[1]
    groups = wsp.shape[0]
    sub = 8
    off_cb = 2 * gmlp_w
    cb_blk = off_cb // conv_w
    q_blk = (off_cb + 3 * conv_w) // xattn_w
    last = n // sub - 1
    tsb = ts // sub

    def prev_map(col):
        return lambda i: (jnp.maximum(i * tsb - 1, 0), col)

    def next_map(col):
        return lambda i: (jnp.minimum((i + 1) * tsb, last), col)

    kern = functools.partial(_branch_kernel, seq=seq, chunk=chunk, groups=groups, heads=XATTN_HEADS)
    return pl.pallas_call(
        kern,
        grid=(n // ts,),
        in_specs=[
            pl.BlockSpec((ts, gmlp_w), lambda i: (i, 0)),
            pl.BlockSpec((ts, gmlp_w), lambda i: (i, 1)),
            pl.BlockSpec((ts, conv_w), lambda i: (i, cb_blk)),
            pl.BlockSpec((ts, conv_w), lambda i: (i, cb_blk + 1)),
            pl.BlockSpec((ts, conv_w), lambda i: (i, cb_blk + 2)),
            pl.BlockSpec((ts, xattn_w), lambda i: (i, q_blk)),
            pl.BlockSpec((sub, conv_w), prev_map(cb_blk + 1)),
            pl.BlockSpec((sub, conv_w), prev_map(cb_blk + 2)),
            pl.BlockSpec((sub, conv_w), next_map(cb_blk + 1)),
            pl.BlockSpec((sub, conv_w), next_map(cb_blk + 2)),
            pl.BlockSpec((n_mem, 2 * xattn_w), lambda i: ((i * ts) // seq, 0)),
            pl.BlockSpec(wsp.shape, lambda i: (0, 0, 0)),
            pl.BlockSpec(bsp.shape, lambda i: (0, 0, 0)),
            pl.BlockSpec((1, gmlp_w), lambda i: (0, 0)),
            pl.BlockSpec(cw.shape, lambda i: (0, 0)),
        ],
        out_specs=pl.BlockSpec((ts, gmlp_w + conv_w + xattn_w), lambda i: (i, 0)),
        out_shape=jax.ShapeDtypeStruct((n, gmlp_w + conv_w + xattn_w), BF16),
        compiler_params=_params("parallel"),
        name="branches",
    )(proj, proj, proj, proj, proj, proj, proj, proj, proj, proj, kv, wsp, bsp, gg, cw)


def _merge_kernel(y_ref, w_ref, ga_ref, gb_ref, gx_ref, o_ref, *, gmlp_w, conv_w, sub):
    ya = y_ref[:, :gmlp_w]
    yb = y_ref[:, gmlp_w:gmlp_w + conv_w]
    yx = y_ref[:, gmlp_w + conv_w:]
    for s in range(o_ref.shape[1] // sub):
        cols = slice(s * sub, (s + 1) * sub)
        a = jnp.dot(ya, w_ref[:gmlp_w, cols], preferred_element_type=F32)
        b = jnp.dot(yb, w_ref[gmlp_w:gmlp_w + conv_w, cols], preferred_element_type=F32)
        x = jnp.dot(yx, w_ref[gmlp_w + conv_w:, cols], preferred_element_type=F32)
        o_ref[:, cols] = (ga_ref[:, cols] * a + gb_ref[:, cols] * b
                          + gx_ref[:, cols] * x).astype(o_ref.dtype)


def _merge(ycat, w_br, proj, *, tm, tn, off_g, d, gmlp_w, conv_w):
    n, k = ycat.shape
    g0 = off_g // tn
    gstep = d // tn
    kern = functools.partial(_merge_kernel, gmlp_w=gmlp_w, conv_w=conv_w, sub=min(MXU_COLS, tn))
    return pl.pallas_call(
        kern,
        grid=(n // tm, d // tn),
        in_specs=[pl.BlockSpec((tm, k), lambda i, j: (i, 0)),
                  pl.BlockSpec((k, tn), lambda i, j: (0, j)),
                  pl.BlockSpec((tm, tn), lambda i, j: (i, g0 + j)),
                  pl.BlockSpec((tm, tn), lambda i, j: (i, g0 + gstep + j)),
                  pl.BlockSpec((tm, tn), lambda i, j: (i, g0 + 2 * gstep + j))],
        out_specs=pl.BlockSpec((tm, tn), lambda i, j: (i, j)),
        out_shape=jax.ShapeDtypeStruct((n, d), BF16),
        compiler_params=_params("parallel", "arbitrary"),
        name="merge",
    )(ycat, w_br, proj, proj, proj)


def _outproj_kernel(m_ref, w_ref, x_ref, o_ref, *, sub):
    m = m_ref[...]
    for s in range(o_ref.shape[1] // sub):
        cols = slice(s * sub, (s + 1) * sub)
        o_ref[:, cols] = x_ref[:, cols] + jnp.dot(m, w_ref[:, cols], preferred_element_type=F32)


def _outproj(merged, w_out, x, tm, tn):
    n, d = x.shape
    return pl.pallas_call(
        functools.partial(_outproj_kernel, sub=min(MXU_COLS, tn)),
        grid=(n // tm, d // tn),
        in_specs=[pl.BlockSpec((tm, d), lambda i, j: (i, 0)),
                  pl.BlockSpec((d, tn), lambda i, j: (0, j)),
                  pl.BlockSpec((tm, tn), lambda i, j: (i, j))],
        out_specs=pl.BlockSpec((tm, tn), lambda i, j: (i, j)),
        out_shape=jax.ShapeDtypeStruct((n, d), F32),
        compiler_params=_params("parallel", "arbitrary"),
        name="out_proj",
    )(merged, w_out, x)


def _pack_bf16_pair(lo, hi):
    lo_bits = pltpu.bitcast(lo.astype(BF16).astype(F32), jnp.uint32)
    hi_bits = pltpu.bitcast(hi.astype(BF16).astype(F32), jnp.uint32)
    return lax.shift_right_logical(lo_bits, jnp.uint32(16)) | (hi_bits & jnp.uint32(0xFFFF0000))


def _unpack_bf16_pair(words):
    lo = pltpu.bitcast(lax.shift_left(words, jnp.uint32(16)), F32)
    hi = pltpu.bitcast(words & jnp.uint32(0xFFFF0000), F32)
    return lo.astype(BF16), hi.astype(BF16)


def _router_kernel(x_ref, g_ref, wrt_ref, hp_ref, afft_ref):
    hf = _rms_scale(x_ref[...]) * g_ref[...]
    half = hf.shape[1] // 2
    words = _pack_bf16_pair(hf[:, :half], hf[:, half:])
    pw = hp_ref.shape[2]
    for c in range(hp_ref.shape[0]):
        hp_ref[c] = words[:, c * pw:(c + 1) * pw]
    lt = lax.dot_general(wrt_ref[...], hf.astype(BF16), (((1,), (1,)), ((), ())),
                         preferred_element_type=F32)
    et = jnp.exp(lt - jnp.max(lt, axis=0, keepdims=True))
    afft_ref[...] = et / jnp.sum(et, axis=0, keepdims=True)


def _router(x2, g, wr, tm, pieces):
    n, d = x2.shape
    ne = wr.shape[1]
    pw = d // 2 // pieces
    return pl.pallas_call(
        _router_kernel,
        grid=(n // tm,),
        in_specs=[pl.BlockSpec((tm, d), lambda i: (i, 0)),
                  pl.BlockSpec((1, d), lambda i: (0, 0)),
                  pl.BlockSpec((ne, d), lambda i: (0, 0))],
        out_specs=[pl.BlockSpec((pieces, tm, pw), lambda i: (0, i, 0)),
                   pl.BlockSpec((ne, tm), lambda i: (0, i))],
        out_shape=[jax.ShapeDtypeStruct((pieces, n, pw), jnp.uint32),
                   jax.ShapeDtypeStruct((ne, n), F32)],
        compiler_params=_params("parallel"),
        name="ffn_norm_router",
    )(x2, g.reshape(1, d), wr.T)


def _select_kernel(aff_ref, slot_ref, base_ref, *, cap, tb):
    ne, n = aff_ref.shape
    bits = pltpu.bitcast(aff_ref[...], jnp.int32)
    capf = jnp.float32(cap)

    def search(k, t):
        cand = t | lax.shift_left(jnp.int32(1), 30 - k)
        cnt = jnp.sum(jnp.where(bits >= cand, 1.0, 0.0), axis=1, keepdims=True)
        return jnp.where(cnt >= capf, cand, t)

    thr = lax.fori_loop(0, 31, search, jnp.zeros((ne, 1), jnp.int32))
    n_gt = jnp.sum(jnp.where(bits > thr, 1.0, 0.0), axis=1, keepdims=True)
    need = capf - n_gt

    r = lax.broadcasted_iota(jnp.int32, (tb, tb), 0)
    c = lax.broadcasted_iota(jnp.int32, (tb, tb), 1)
    tri = jnp.where(r < c, 1.0, 0.0).astype(BF16)
    lane = lax.broadcasted_iota(jnp.int32, base_ref.shape, 1)
    eq_seen = jnp.zeros((ne, 1), F32)
    sel_seen = jnp.zeros((ne, 1), F32)
    base = jnp.zeros(base_ref.shape, F32)
    for b in range(n // tb):
        blk = pltpu.bitcast(aff_ref[:, b * tb:(b + 1) * tb], jnp.int32)
        eq = jnp.where(blk == thr, 1.0, 0.0)
        eq_rank = jnp.dot(eq.astype(BF16), tri, preferred_element_type=F32) + eq_seen
        sel = jnp.where(blk > thr, 1.0, jnp.where(eq_rank < need, eq, 0.0))
        pos = jnp.dot(sel.astype(BF16), tri, preferred_element_type=F32) + sel_seen
        slot_ref[:, b * tb:(b + 1) * tb] = jnp.where(sel > 0.0, pos, -1.0).astype(jnp.int32)
        base = jnp.where(lane == b, sel_seen, base)
        eq_seen = eq_seen + jnp.sum(eq, axis=1, keepdims=True)
        sel_seen = sel_seen + jnp.sum(sel, axis=1, keepdims=True)
    base = jnp.where(lane == n // tb, sel_seen, base)
    base_ref[...] = base.astype(jnp.int32)


def _select(afft, cap, tb):
    ne, n = afft.shape
    lanes = 128
    assert n // tb < lanes
    kern = functools.partial(_select_kernel, cap=cap, tb=tb)
    return pl.pallas_call(
        kern,
        out_shape=[jax.ShapeDtypeStruct((ne, n), jnp.int32),
                   jax.ShapeDtypeStruct((ne, lanes), jnp.int32)],
        compiler_params=pltpu.CompilerParams(vmem_limit_bytes=VMEM_LIMIT_BYTES),
        name="expert_choice_select",
    )(afft)


def _pick_list_kernel(base_ref, slot_ref, aff_ref, tok_ref, gw_ref, *, cap, nblk, chunk):
    eb = pl.program_id(0)
    b = pl.program_id(1)

    @pl.when(b == 0)
    def _():
        tok_ref[...] = jnp.zeros_like(tok_ref)
        gw_ref[...] = jnp.zeros_like(gw_ref)

    tb = slot_ref.shape[2]
    token = (lax.broadcasted_iota(jnp.int32, (1, tb), 1) + b * tb).astype(F32)
    for k in range(tok_ref.shape[0]):
        e = eb * tok_ref.shape[0] + k
        first = (base_ref[e * (nblk + 1) + b] // SUBLANES) * SUBLANES
        end = base_ref[e * (nblk + 1) + b + 1]
        slot = slot_ref[k]
        aff = aff_ref[k]

        def body(c, carry, k=k, first=first, slot=slot, aff=aff):
            lo = first + c * chunk
            st = pl.multiple_of(jnp.minimum(lo, cap - chunk), SUBLANES)
            rows = lax.broadcasted_iota(jnp.int32, (chunk, tb), 0) + st
            hit = jnp.where(rows == slot, jnp.where(rows >= lo, 1.0, 0.0), 0.0)
            tok_ref[k, pl.ds(st, chunk), :] += jnp.sum(hit * token, axis=1,
                                                       keepdims=True).astype(jnp.int32)
            gw_ref[k, pl.ds(st, chunk), :] += jnp.sum(hit * aff, axis=1, keepdims=True)
            return carry

        lax.fori_loop(0, (end - first + chunk - 1) // chunk, body, 0)


def _pick_list(base_flat, slot3, aff3, *, cap, tb, chunk, experts_per_step):
    ne, _, n = slot3.shape
    nblk = n // tb
    eps = experts_per_step
    assert cap >= chunk and (cap - chunk) % SUBLANES == 0 and chunk % SUBLANES == 0
    kern = functools.partial(_pick_list_kernel, cap=cap, nblk=nblk, chunk=chunk)
    return pl.pallas_call(
        kern,
        grid_spec=pltpu.PrefetchScalarGridSpec(
            num_scalar_prefetch=1,
            grid=(ne // eps, nblk),
            in_specs=[pl.BlockSpec((eps, 1, tb), lambda e, b, base: (e, 0, b)),
                      pl.BlockSpec((eps, 1, tb), lambda e, b, base: (e, 0, b))],
            out_specs=[pl.BlockSpec((eps, cap, 1), lambda e, b, base: (e, 0, 0)),
                       pl.BlockSpec((eps, cap, 1), lambda e, b, base: (e, 0, 0))],
        ),
        out_shape=[jax.ShapeDtypeStruct((ne, cap, 1), jnp.int32),
                   jax.ShapeDtypeStruct((ne, cap, 1), F32)],
        compiler_params=_params("parallel", "arbitrary"),
        name="expert_pick_list",
    )(base_flat, slot3, aff3)


def _row_gather_sparsecore(table, rows, *, window):
    num, width = rows.shape[0], table.shape[1]
    mesh = plsc.VectorSubcoreMesh(core_axis_name="c", subcore_axis_name="s")

    @pl.kernel(out_type=jax.ShapeDtypeStruct((num, width), table.dtype), mesh=mesh,
               name="expert_row_gather")
    def gather(table_hbm, rows_hbm, out_hbm):
        def body(rows_vmem, out_vmem):
            pltpu.sync_copy(table_hbm.at[rows_vmem.at[0]], out_vmem)

        pltpu.emit_pipeline(
            body,
            grid=(num // window,),
            in_specs=[pl.BlockSpec((1, window), index_map=lambda i: (0, i))],
            out_specs=[pl.BlockSpec((window, width), index_map=lambda i: (i, 0))],
            core_axis_name=("c", "s"),
            dimension_semantics=(pltpu.PARALLEL,),
        )(rows_hbm, out_hbm)

    return gather(table, rows.reshape(1, num))


def _gather(tok, hp, *, window):
    ne, cap, _ = tok.shape
    pieces, n, pw = hp.shape
    rows = (jnp.arange(pieces, dtype=jnp.int32)[:, None] * n + tok.reshape(1, ne * cap)).reshape(-1)
    out = _row_gather_sparsecore(hp.reshape(pieces * n, pw), rows, window=window)
    return out.reshape(pieces, ne, cap, pw)


def _ffn_kernel(*refs, groups, caps, nt_up, tf):
    xp_hbm = refs[:groups]
    gw_ref, wg_ref, wu_ref, wd_ref = refs[groups:groups + 4]
    ye_refs = refs[groups + 4:2 * groups + 4]
    xe_ref, hid_ref, acc_ref, sem = refs[2 * groups + 4:]
    e = pl.program_id(0)
    j = pl.program_id(1)
    pieces, _, _, pw = xp_hbm[0].shape
    starts = [sum(caps[:g]) for g in range(groups)]

    @pl.when(j == 0)
    def _():
        landing = hid_ref.bitcast(jnp.uint32)
        copies = []
        for g in range(groups):
            for c in range(pieces):
                k = g * pieces + c
                copies.append(pltpu.make_async_copy(
                    xp_hbm[g].at[c, e], landing.at[pl.ds(0, caps[g]), pl.ds(k * pw, pw)],
                    sem.at[g, c]))
        for cp in copies:
            cp.start()
        for cp in copies:
            cp.wait()
        for g in range(groups):
            dst = slice(starts[g], starts[g] + caps[g])
            for c in range(pieces):
                k = g * pieces + c
                lo, hi = _unpack_bf16_pair(landing[0:caps[g], k * pw:(k + 1) * pw])
                xe_ref[dst, c * pw:(c + 1) * pw] = lo
                xe_ref[dst, (pieces + c) * pw:(pieces + c + 1) * pw] = hi

    @pl.when(j < nt_up)
    def _():
        w = jnp.concatenate([wg_ref[...].astype(BF16), wu_ref[...].astype(BF16)], axis=1)
        gu = jnp.dot(xe_ref[...], w, preferred_element_type=F32)
        g, u = gu[:, :tf], gu[:, tf:]
        off = pl.multiple_of(j * tf, tf)
        hid_ref[:, pl.ds(off, tf)] = ((g * _sigmoid(g)) * u).astype(hid_ref.dtype)

    kh = (j - nt_up) % 2
    fh = wd_ref.shape[0]

    @pl.when(jnp.logical_and(j >= nt_up, kh == 0))
    def _():
        acc_ref[...] = jnp.dot(hid_ref[:, :fh], wd_ref[...].astype(BF16),
                               preferred_element_type=F32)

    @pl.when(jnp.logical_and(j >= nt_up, kh == 1))
    def _():
        y = acc_ref[...] + jnp.dot(hid_ref[:, fh:], wd_ref[...].astype(BF16),
                                   preferred_element_type=F32)
        y = y * gw_ref[...]
        for g in range(groups):
            ye_refs[g][...] = y[starts[g]:starts[g] + caps[g]].astype(ye_refs[g].dtype)


def _ffn(xps, gws, wg, wu, wd, tf, tn):
    groups = len(xps)
    pieces, ne, _, pw = xps[0].shape
    caps = tuple(xp.shape[2] for xp in xps)
    rows = sum(caps)
    d, f = wg.shape[1], wg.shape[2]
    assert d == 2 * pieces * pw and f % 2 == 0
    assert groups * pieces * pw <= f and 2 * max(caps) <= rows
    nt_up = f // tf
    nt_dn = d // tn
    kern = functools.partial(_ffn_kernel, groups=groups, caps=caps, nt_up=nt_up, tf=tf)

    def up_map(e, j):
        return (e, 0, jnp.minimum(j, nt_up - 1))

    def dn_step(j):
        return jnp.maximum(j - nt_up, 0)

    return pl.pallas_call(
        kern,
        grid=(ne, nt_up + 2 * nt_dn),
        in_specs=[pl.BlockSpec(memory_space=pl.ANY)] * groups
                 + [pl.BlockSpec((None, rows, 1), lambda e, j: (e, 0, 0),
                                 pipeline_mode=pl.Buffered(1)),
                    pl.BlockSpec((None, d, tf), up_map),
                    pl.BlockSpec((None, d, tf), up_map),
                    pl.BlockSpec((None, f // 2, tn), lambda e, j: (e, dn_step(j) % 2, dn_step(j) // 2))],
        out_specs=[pl.BlockSpec((None, c, tn), lambda e, j: (e, 0, dn_step(j) // 2)) for c in caps],
        out_shape=[jax.ShapeDtypeStruct((ne, c, d), BF16) for c in caps],
        scratch_shapes=[pltpu.VMEM((rows, d), BF16), pltpu.VMEM((rows, f), BF16),
                        pltpu.VMEM((rows, tn), F32),
                        pltpu.SemaphoreType.DMA((groups, pieces))],
        compiler_params=pltpu.CompilerParams(dimension_semantics=("arbitrary", "arbitrary"),
                                             vmem_limit_bytes=FFN_VMEM_LIMIT_BYTES),
        name="expert_ffn",
    )(*xps, jnp.concatenate(gws, axis=1), wg, wu, wd)


def _expert_column(ref, e):
    lane = lax.broadcasted_iota(jnp.int32, ref.shape, 1)
    return jnp.sum(jnp.where(lane == e, ref[...], 0), axis=1, keepdims=True)


def _combine_any_kernel(base_ref, slot_ref, x_ref, ya_ref, yb_ref, g_ref, o_ref, *, cap, nblk):
    b = pl.program_id(0)
    e = pl.program_id(1)
    ne = pl.num_programs(1)

    @pl.when(e == 0)
    def _():
        o_ref[...] = x_ref[...]

    tb = x_ref.shape[0]
    blk = ya_ref.shape[0]
    start = jnp.minimum(base_ref[e * (nblk + 1) + b] // blk, cap // blk - 1) * blk
    slot = _expert_column(slot_ref, e)
    cols = lax.broadcasted_iota(jnp.int32, (tb, blk), 1) + start
    oh_a = jnp.where(cols == slot, 1.0, 0.0).astype(BF16)
    oh_b = jnp.where(cols + blk == slot, 1.0, 0.0).astype(BF16)
    o_ref[...] += (jnp.dot(oh_a, ya_ref[...], preferred_element_type=F32)
                   + jnp.dot(oh_b, yb_ref[...], preferred_element_type=F32))

    @pl.when(e == ne - 1)
    def _():
        o_ref[...] = _rms_scale(o_ref[...]) * g_ref[...]


def _combine_any(base_flat, slot_tok, x2, ye, g_final, *, cap, tb):
    n, d = x2.shape
    ne = ye.shape[0]
    nblk = n // tb
    blk = tb
    last = cap // blk - 1

    def ya_map(b, e, base):
        return (e, jnp.minimum(base[e * (nblk + 1) + b] // blk, last), 0)

    def yb_map(b, e, base):
        return (e, jnp.minimum(jnp.minimum(base[e * (nblk + 1) + b] // blk, last) + 1, last), 0)

    kern = functools.partial(_combine_any_kernel, cap=cap, nblk=nblk)
    return pl.pallas_call(
        kern,
        grid_spec=pltpu.PrefetchScalarGridSpec(
            num_scalar_prefetch=1,
            grid=(nblk, ne),
            in_specs=[pl.BlockSpec((tb, ne), lambda b, e, base: (b, 0)),
                      pl.BlockSpec((tb, d), lambda b, e, base: (b, 0)),
                      pl.BlockSpec((None, blk, d), ya_map),
                      pl.BlockSpec((None, blk, d), yb_map),
                      pl.BlockSpec((1, d), lambda b, e, base: (0, 0))],
            out_specs=pl.BlockSpec((tb, d), lambda b, e, base: (b, 0)),
        ),
        out_shape=jax.ShapeDtypeStruct((n, d), F32),
        compiler_params=_params("parallel", "arbitrary"),
        name="expert_combine_any",
    )(base_flat, slot_tok, x2, ye, ye, g_final.reshape(1, d))


def _window_start(base, e, b, *, nblk, cap, win):
    tile = jnp.minimum(base[e * (nblk + 1) + b] // BF16_ROW_TILE, (cap - win) // BF16_ROW_TILE)
    return tile * BF16_ROW_TILE


def _combine_windows_kernel(base_ref, slot_ref, x_ref, g_ref, *rest, cap, nblk, win):
    y_refs, o_ref = rest[:-1], rest[-1]
    b = pl.program_id(0)
    s = pl.program_id(1)
    per = len(y_refs)

    @pl.when(s == 0)
    def _():
        o_ref[...] = x_ref[...]

    tb = x_ref.shape[0]
    col = lax.broadcasted_iota(jnp.int32, (tb, win), 1)
    onehots = []
    for k in range(per):
        e = s * per + k
        start = _window_start(base_ref, e, b, nblk=nblk, cap=cap, win=win)
        onehots.append(jnp.where(col + start == _expert_column(slot_ref, e), 1.0, 0.0).astype(BF16))
    y = jnp.concatenate([r[...] for r in y_refs], axis=0)
    o_ref[...] += jnp.dot(jnp.concatenate(onehots, axis=1), y, preferred_element_type=F32)

    @pl.when(s == pl.num_programs(1) - 1)
    def _():
        o_ref[...] = _rms_scale(o_ref[...]) * g_ref[...]


def _combine_windows(base_flat, slot_tok, x2, ye, g_final, *, cap, tb, win, experts_per_step):
    n, d = x2.shape
    ne = ye.shape[0]
    nblk = n // tb
    per = experts_per_step
    assert ne % per == 0

    def y_map(k):
        def index(b, s, base):
            e = s * per + k
            return (e, _window_start(base, e, b, nblk=nblk, cap=cap, win=win), 0)
        return index

    kern = functools.partial(_combine_windows_kernel, cap=cap, nblk=nblk, win=win)
    return pl.pallas_call(
        kern,
        grid_spec=pltpu.PrefetchScalarGridSpec(
            num_scalar_prefetch=1,
            grid=(nblk, ne // per),
            in_specs=[pl.BlockSpec((tb, ne), lambda b, s, base: (b, 0)),
                      pl.BlockSpec((tb, d), lambda b, s, base: (b, 0)),
                      pl.BlockSpec((1, d), lambda b, s, base: (0, 0))]
                     + [pl.BlockSpec((None, pl.Element(win), pl.Element(d)), y_map(k))
                        for k in range(per)],
            out_specs=pl.BlockSpec((tb, d), lambda b, s, base: (b, 0)),
        ),
        out_shape=jax.ShapeDtypeStruct((n, d), F32),
        compiler_params=_params("parallel", "arbitrary"),
        name="expert_combine_windows",
    )(base_flat, slot_tok, x2, g_final.reshape(1, d), *([ye] * per))


def _combine(base, slot_tok, x2, ye, g_final, *, cap, tb, win):
    nblk = x2.shape[0] // tb
    base_flat = base[:, :nblk + 1].reshape(-1)
    first = (base[:, :nblk] // BF16_ROW_TILE) * BF16_ROW_TILE
    fits = jnp.all(base[:, 1:nblk + 1] - jnp.minimum(first, cap - win) <= win)
    return lax.cond(
        fits,
        lambda: _combine_windows(base_flat, slot_tok, x2, ye, g_final, cap=cap, tb=tb, win=win,
                                 experts_per_step=8),
        lambda: _combine_any(base_flat, slot_tok, x2, ye, g_final, cap=cap, tb=tb))


def _mix_and_route(x3, mem3, p):
    bsz, seq, d = x3.shape
    n = bsz * seq
    n_mem = mem3.shape[1]
    x = x3.reshape(n, d)
    mem = mem3.reshape(bsz * n_mem, d)

    gmlp_w = p["gmlp_norm_g"].shape[0]
    conv_w = p["conv_w"].shape[1]
    xattn_w = p["w_mem_kv"].shape[1] // 2
    off_cb = 2 * gmlp_w
    off_g = off_cb + 3 * conv_w + xattn_w
    ne = p["w_router"].shape[1]
    cap = CAPACITY_FACTOR * n // ne

    def col_tile(cols):
        return next(t for t in (1024, 512, 256, 128) if cols % t == 0 and t <= cols)

    tm = min(1024, n)
    ts = min(256, seq)
    tb = 256 if cap >= 256 + BF16_ROW_TILE else 128

    h = _rmsnorm_cast(x, p["norm_mix_g"], ts)
    proj = _proj(h, p["w_in"], tm, col_tile(p["w_in"].shape[1]), off_cb, off_g, F32)
    m = _rmsnorm_cast(mem, p["mem_norm_g"], n_mem)
    kv = _proj(m, p["w_mem_kv"], n_mem, col_tile(2 * xattn_w), 0, 2 * xattn_w, BF16)
    ycat = _branches(proj, kv, p["w_spatial"], p["b_spatial"], p["gmlp_norm_g"].reshape(1, gmlp_w),
                     p["conv_w"], seq=seq, ts=ts, gmlp_w=gmlp_w, conv_w=conv_w, xattn_w=xattn_w,
                     n_mem=n_mem)
    merged = _merge(ycat, p["w_branch"], proj, tm=tm, tn=min(512, col_tile(d)), off_g=off_g, d=d,
                    gmlp_w=gmlp_w, conv_w=conv_w)
    x2 = _outproj(merged, p["w_out"], x, tm, col_tile(d))

    hp, afft = _router(x2, p["norm_ffn_g"], p["w_router"], ts, min(SC_ROW_PIECES, d // 256))
    slot, base = _select(afft, cap, tb)
    nblk = n // tb
    base_flat = base[:, :nblk + 1].reshape(-1)
    tok, gw = _pick_list(base_flat, slot.reshape(ne, 1, n), afft.reshape(ne, 1, n), cap=cap, tb=tb,
                         chunk=PICK_CHUNK_ROWS, experts_per_step=min(8, ne))
    xp = _gather(tok, hp, window=SC_GATHER_WINDOW)
    return {"xp": xp, "gw": gw, "x2": x2, "base": base, "slot_tok": slot.T, "cap": cap, "tb": tb}


def _combine_group(r, ye, g_final, shape):
    y = _combine(r["base"], r["slot_tok"], r["x2"], ye, g_final, cap=r["cap"], tb=r["tb"],
                 win=min(COMBINE_WINDOW_ROWS, r["tb"]))
    return y.reshape(shape)


def kernel(x_prompt, x_sample, mem_prompt, mem_sample, norm_mix_g, w_in, gmlp_norm_g, w_spatial,
           b_spatial, conv_w, mem_norm_g, w_mem_kv, w_branch_a, w_branch_b, w_branch_x, w_out,
           norm_ffn_g, w_router, w_gate, w_up, w_down, final_norm_g):
    assert w_in.shape[0] == 1, "one layer"
    groups, chunk = b_spatial.shape[1], b_spatial.shape[2]
    gc = gmlp_norm_g.shape[1] // groups
    p = {
        "norm_mix_g": norm_mix_g[0],
        "w_in": w_in[0].astype(BF16),
        "gmlp_norm_g": gmlp_norm_g[0],
        "w_spatial": w_spatial[0].astype(BF16),
        "b_spatial": jnp.broadcast_to(b_spatial[0][:, :, None], (groups, chunk, gc)),
        "conv_w": conv_w[0],
        "mem_norm_g": mem_norm_g[0],
        "w_mem_kv": w_mem_kv[0].astype(BF16),
        "w_branch": jnp.concatenate([w_branch_a[0], w_branch_b[0], w_branch_x[0]],
                                    axis=0).astype(BF16),
        "w_out": w_out[0].astype(BF16),
        "norm_ffn_g": norm_ffn_g[0],
        "w_router": w_router[0].astype(BF16),
        "w_gate": w_gate.reshape(w_gate.shape[1:]),
        "w_up": w_up.reshape(w_up.shape[1:]),
        "w_down": w_down.reshape(w_down.shape[1:]),
        "final_norm_g": final_norm_g,
    }
    routed = [_mix_and_route(x_prompt, mem_prompt, p), _mix_and_route(x_sample, mem_sample, p)]
    d = x_prompt.shape[-1]
    yes = _ffn([r["xp"] for r in routed], [r["gw"] for r in routed], p["w_gate"], p["w_up"],
               p["w_down"], min(FFN_UP_COLS, d), min(MXU_COLS, d))
    return tuple(_combine_group(r, ye, final_norm_g, x.shape)
                 for r, ye, x in zip(routed, yes, (x_prompt, x_sample)))
```

```python
import functools

import jax
import jax.numpy as jnp
from jax import lax
from jax.experimental import pallas as pl
from jax.experimental.pallas import tpu as pltpu
from jax.experimental.pallas import tpu_sc as plsc

EPS = 1e-6
XATTN_HEADS = 4
CAPACITY_FACTOR = 2
BF16_ROW_TILE = 16
SUBLANES = 8
PICK_CHUNK_ROWS = 80
SC_ROW_PIECES = 8
SC_GATHER_WINDOW = 128
COMBINE_WINDOW_ROWS = 128
VMEM_LIMIT_BYTES = 56 * 1024 * 1024
FFN_VMEM_LIMIT_BYTES = 60 * 1024 * 1024
FFN_UP_COLS = 512
FFN_DOWN_COLS = 512

F32 = jnp.float32
BF16 = jnp.bfloat16


def _params(*semantics):
    return pltpu.CompilerParams(dimension_semantics=semantics,
                                vmem_limit_bytes=VMEM_LIMIT_BYTES)


GELU_K = 0.7978845608028654
GELU_CUBIC = 0.044715
TANH_SATURATED = 30.0
MXU_COLS = 256


def _sigmoid(x):
    return 1.0 / (1.0 + jnp.exp(-x))


def _rms_scale(x):
    return x * lax.rsqrt(jnp.mean(x * x, axis=-1, keepdims=True) + EPS)


def _rmsnorm_cast_kernel(x_ref, g_ref, o_ref):
    o_ref[...] = (_rms_scale(x_ref[...]) * g_ref[...]).astype(o_ref.dtype)


def _rmsnorm_cast(x, g, tm):
    n, d = x.shape
    return pl.pallas_call(
        _rmsnorm_cast_kernel,
        grid=(n // tm,),
        in_specs=[pl.BlockSpec((tm, d), lambda i: (i, 0)),
                  pl.BlockSpec((1, d), lambda i: (0, 0))],
        out_specs=pl.BlockSpec((tm, d), lambda i: (i, 0)),
        out_shape=jax.ShapeDtypeStruct((n, d), BF16),
        compiler_params=_params("parallel"),
        name="rmsnorm_cast",
    )(x, g.reshape(1, d))


def _proj_kernel(h_ref, w_ref, o_ref, *, sub, gelu_tiles, sigmoid_from):
    j = pl.program_id(1)
    nsub = o_ref.shape[1] // sub
    h = h_ref[...]
    for s in range(nsub):
        c = j * nsub + s
        is_gelu = c < gelu_tiles
        is_sig = c >= sigmoid_from
        q1 = jnp.where(is_gelu, GELU_K, jnp.where(is_sig, 0.5, 0.0))
        q3 = jnp.where(is_gelu, GELU_K * GELU_CUBIC, 0.0)
        q0 = jnp.where(jnp.logical_or(is_gelu, is_sig), 0.0, TANH_SATURATED)
        p1 = jnp.where(is_sig, 0.0, 1.0)
        p0 = jnp.where(is_sig, 1.0, 0.0)
        cols = slice(s * sub, (s + 1) * sub)
        x = jnp.dot(h, w_ref[:, cols], preferred_element_type=F32)
        gate = 0.5 * (1.0 + jnp.tanh(x * (q1 + q3 * (x * x)) + q0))
        o_ref[:, cols] = ((x * p1 + p0) * gate).astype(o_ref.dtype)


def _proj(h, w, tm, tn, gelu_cols, sigmoid_from_col, out_dtype):
    n, d = h.shape
    cols = w.shape[1]
    sub = min(MXU_COLS, tn)
    kern = functools.partial(_proj_kernel, sub=sub, gelu_tiles=gelu_cols // sub,
                             sigmoid_from=sigmoid_from_col // sub)
    return pl.pallas_call(
        kern,
        grid=(n // tm, cols // tn),
        in_specs=[pl.BlockSpec((tm, d), lambda i, j: (i, 0)),
                  pl.BlockSpec((d, tn), lambda i, j: (0, j))],
        out_specs=pl.BlockSpec((tm, tn), lambda i, j: (i, j)),
        out_shape=jax.ShapeDtypeStruct((n, cols), out_dtype),
        compiler_params=_params("parallel", "arbitrary"),
        name="in_proj",
    )(h, w)


def _branch_kernel(u_ref, v_ref, cb_ref, cc_ref, ch_ref, q_ref, ccp_ref, chp_ref, ccn_ref, chn_ref,
                   kv_ref, wsp_ref, bsp_ref, gg_ref, cw_ref, y_ref, *, seq, chunk, groups, heads):
    i = pl.program_id(0)
    ts = u_ref.shape[0]
    gw = u_ref.shape[1]
    gc = gw // groups
    cw = cb_ref.shape[1]
    xw = q_ref.shape[1]
    hd = xw // heads

    vn = (_rms_scale(v_ref[...].astype(F32)) * gg_ref[...]).astype(BF16)
    for c in range(ts // chunk):
        rows = slice(c * chunk, (c + 1) * chunk)
        for g in range(groups):
            cols = slice(g * gc, (g + 1) * gc)
            mixed = jnp.dot(wsp_ref[g], vn[rows, cols], preferred_element_type=F32) + bsp_ref[g]
            y_ref[rows, cols] = (u_ref[rows, cols].astype(F32) * mixed).astype(y_ref.dtype)

    pos = (i * ts) % seq
    z = cc_ref[...].astype(F32) * ch_ref[...].astype(F32)
    sub = ccp_ref.shape[0]
    z_before = jnp.where(pos == 0, 0.0, ccp_ref[sub - 1:sub, :].astype(F32)
                         * chp_ref[sub - 1:sub, :].astype(F32))
    z_after = jnp.where(pos + ts == seq, 0.0, ccn_ref[0:1, :].astype(F32)
                        * chn_ref[0:1, :].astype(F32))
    row = lax.broadcasted_iota(jnp.int32, z.shape, 0)
    z_prev = jnp.where(row == 0, z_before, pltpu.roll(z, 1, 0))
    z_next = jnp.where(row == ts - 1, z_after, pltpu.roll(z, ts - 1, 0))
    conv = z_prev * cw_ref[0:1, :] + z * cw_ref[1:2, :] + z_next * cw_ref[2:3, :]
    y_ref[:, gw:gw + cw] = (cb_ref[...].astype(F32) * conv).astype(y_ref.dtype)

    scale = hd ** -0.5
    for h in range(heads):
        cols = slice(h * hd, (h + 1) * hd)
        qh = q_ref[:, cols]
        kh = kv_ref[:, cols]
        vh = kv_ref[:, xw + h * hd: xw + (h + 1) * hd]
        s = lax.dot_general(qh, kh, (((1,), (1,)), ((), ())), preferred_element_type=F32) * scale
        e = jnp.exp(s - jnp.max(s, axis=-1, keepdims=True))
        p = (e / jnp.sum(e, axis=-1, keepdims=True)).astype(BF16)
        yh = jnp.dot(p, vh, preferred_element_type=F32)
        y_ref[:, gw + cw + h * hd: gw + cw + (h + 1) * hd] = yh.astype(y_ref.dtype)


def _branches(proj, kv, wsp, bsp, gg, cw, *, seq, ts, gmlp_w, conv_w, xattn_w, n_mem):
    n = proj.shape[0]
    chunk = wsp.shape[1]
    groups = wsp.shape[0]
    sub = BF16_ROW_TILE
    off_cb = 2 * gmlp_w
    cb_blk = off_cb // conv_w
    q_blk = (off_cb + 3 * conv_w) // xattn_w
    last = n // sub - 1
    tsb = ts // sub

    def prev_map(col):
        return lambda i: (jnp.maximum(i * tsb - 1, 0), col)

    def next_map(col):
        return lambda i: (jnp.minimum((i + 1) * tsb, last), col)

    kern = functools.partial(_branch_kernel, seq=seq, chunk=chunk, groups=groups, heads=XATTN_HEADS)
    return pl.pallas_call(
        kern,
        grid=(n // ts,),
        in_specs=[
            pl.BlockSpec((ts, gmlp_w), lambda i: (i, 0)),
            pl.BlockSpec((ts, gmlp_w), lambda i: (i, 1)),
            pl.BlockSpec((ts, conv_w), lambda i: (i, cb_blk)),
            pl.BlockSpec((ts, conv_w), lambda i: (i, cb_blk + 1)),
            pl.BlockSpec((ts, conv_w), lambda i: (i, cb_blk + 2)),
            pl.BlockSpec((ts, xattn_w), lambda i: (i, q_blk)),
            pl.BlockSpec((sub, conv_w), prev_map(cb_blk + 1)),
            pl.BlockSpec((sub, conv_w), prev_map(cb_blk + 2)),
            pl.BlockSpec((sub, conv_w), next_map(cb_blk + 1)),
            pl.BlockSpec((sub, conv_w), next_map(cb_blk + 2)),
            pl.BlockSpec((n_mem, 2 * xattn_w), lambda i: ((i * ts) // seq, 0)),
            pl.BlockSpec(wsp.shape, lambda i: (0, 0, 0)),
            pl.BlockSpec(bsp.shape, lambda i: (0, 0, 0)),
            pl.BlockSpec((1, gmlp_w), lambda i: (0, 0)),
            pl.BlockSpec(cw.shape, lambda i: (0, 0)),
        ],
        out_specs=pl.BlockSpec((ts, gmlp_w + conv_w + xattn_w), lambda i: (i, 0)),
        out_shape=jax.ShapeDtypeStruct((n, gmlp_w + conv_w + xattn_w), BF16),
        compiler_params=_params("parallel"),
        name="branches",
    )(proj, proj, proj, proj, proj, proj, proj, proj, proj, proj, kv, wsp, bsp, gg, cw)


def _merge_kernel(y_ref, w_ref, ga_ref, gb_ref, gx_ref, o_ref, *, gmlp_w, conv_w, sub):
    ya = y_ref[:, :gmlp_w]
    yb = y_ref[:, gmlp_w:gmlp_w + conv_w]
    yx = y_ref[:, gmlp_w + conv_w:]
    for s in range(o_ref.shape[1] // sub):
        cols = slice(s * sub, (s + 1) * sub)
        a = jnp.dot(ya, w_ref[:gmlp_w, cols], preferred_element_type=F32)
        b = jnp.dot(yb, w_ref[gmlp_w:gmlp_w + conv_w, cols], preferred_element_type=F32)
        x = jnp.dot(yx, w_ref[gmlp_w + conv_w:, cols], preferred_element_type=F32)
        o_ref[:, cols] = (ga_ref[:, cols].astype(F32) * a + gb_ref[:, cols].astype(F32) * b
                          + gx_ref[:, cols].astype(F32) * x).astype(o_ref.dtype)


def _merge(ycat, w_br, proj, *, tm, tn, off_g, d, gmlp_w, conv_w):
    n, k = ycat.shape
    g0 = off_g // tn
    gstep = d // tn
    kern = functools.partial(_merge_kernel, gmlp_w=gmlp_w, conv_w=conv_w, sub=min(MXU_COLS, tn))
    return pl.pallas_call(
        kern,
        grid=(n // tm, d // tn),
        in_specs=[pl.BlockSpec((tm, k), lambda i, j: (i, 0)),
                  pl.BlockSpec((k, tn), lambda i, j: (0, j)),
                  pl.BlockSpec((tm, tn), lambda i, j: (i, g0 + j)),
                  pl.BlockSpec((tm, tn), lambda i, j: (i, g0 + gstep + j)),
                  pl.BlockSpec((tm, tn), lambda i, j: (i, g0 + 2 * gstep + j))],
        out_specs=pl.BlockSpec((tm, tn), lambda i, j: (i, j)),
        out_shape=jax.ShapeDtypeStruct((n, d), BF16),
        compiler_params=_params("parallel", "arbitrary"),
        name="merge",
    )(ycat, w_br, proj, proj, proj)


def _outproj_kernel(m_ref, w_ref, x_ref, o_ref, *, sub):
    m = m_ref[...]
    for s in range(o_ref.shape[1] // sub):
        cols = slice(s * sub, (s + 1) * sub)
        o_ref[:, cols] = x_ref[:, cols] + jnp.dot(m, w_ref[:, cols], preferred_element_type=F32)


def _outproj(merged, w_out, x, tm, tn):
    n, d = x.shape
    return pl.pallas_call(
        functools.partial(_outproj_kernel, sub=min(MXU_COLS, tn)),
        grid=(n // tm, d // tn),
        in_specs=[pl.BlockSpec((tm, d), lambda i, j: (i, 0)),
                  pl.BlockSpec((d, tn), lambda i, j: (0, j)),
                  pl.BlockSpec((tm, tn), lambda i, j: (i, j))],
        out_specs=pl.BlockSpec((tm, tn), lambda i, j: (i, j)),
        out_shape=jax.ShapeDtypeStruct((n, d), F32),
        compiler_params=_params("parallel", "arbitrary"),
        name="out_proj",
    )(merged, w_out, x)


def _pack_bf16_pair(lo, hi):
    lo_bits = pltpu.bitcast(lo.astype(BF16).astype(F32), jnp.uint32)
    hi_bits = pltpu.bitcast(hi.astype(BF16).astype(F32), jnp.uint32)
    return lax.shift_right_logical(lo_bits, jnp.uint32(16)) | (hi_bits & jnp.uint32(0xFFFF0000))


def _unpack_bf16_pair(words):
    lo = pltpu.bitcast(lax.shift_left(words, jnp.uint32(16)), F32)
    hi = pltpu.bitcast(words & jnp.uint32(0xFFFF0000), F32)
    return lo.astype(BF16), hi.astype(BF16)


def _router_kernel(x_ref, g_ref, wrt_ref, hp_ref, afft_ref):
    hf = _rms_scale(x_ref[...]) * g_ref[...]
    half = hf.shape[1] // 2
    words = _pack_bf16_pair(hf[:, :half], hf[:, half:])
    pw = hp_ref.shape[2]
    for c in range(hp_ref.shape[0]):
        hp_ref[c] = words[:, c * pw:(c + 1) * pw]
    lt = lax.dot_general(wrt_ref[...], hf.astype(BF16), (((1,), (1,)), ((), ())),
                         preferred_element_type=F32)
    et = jnp.exp(lt - jnp.max(lt, axis=0, keepdims=True))
    afft_ref[...] = et / jnp.sum(et, axis=0, keepdims=True)


def _router(x2, g, wr, tm, pieces):
    n, d = x2.shape
    ne = wr.shape[1]
    pw = d // 2 // pieces
    return pl.pallas_call(
        _router_kernel,
        grid=(n // tm,),
        in_specs=[pl.BlockSpec((tm, d), lambda i: (i, 0)),
                  pl.BlockSpec((1, d), lambda i: (0, 0)),
                  pl.BlockSpec((ne, d), lambda i: (0, 0))],
        out_specs=[pl.BlockSpec((pieces, tm, pw), lambda i: (0, i, 0)),
                   pl.BlockSpec((ne, tm), lambda i: (0, i))],
        out_shape=[jax.ShapeDtypeStruct((pieces, n, pw), jnp.uint32),
                   jax.ShapeDtypeStruct((ne, n), F32)],
        compiler_params=_params("parallel"),
        name="ffn_norm_router",
    )(x2, g.reshape(1, d), wr.T)


def _select_kernel(aff_ref, slot_ref, base_ref, *, cap, tb):
    ne, n = aff_ref.shape
    bits = pltpu.bitcast(aff_ref[...], jnp.int32)
    capf = jnp.float32(cap)

    def search(k, t):
        cand = t | lax.shift_left(jnp.int32(1), 30 - k)
        cnt = jnp.sum(jnp.where(bits >= cand, 1.0, 0.0), axis=1, keepdims=True)
        return jnp.where(cnt >= capf, cand, t)

    thr = lax.fori_loop(0, 31, search, jnp.zeros((ne, 1), jnp.int32))
    n_gt = jnp.sum(jnp.where(bits > thr, 1.0, 0.0), axis=1, keepdims=True)
    need = capf - n_gt

    r = lax.broadcasted_iota(jnp.int32, (tb, tb), 0)
    c = lax.broadcasted_iota(jnp.int32, (tb, tb), 1)
    tri = jnp.where(r < c, 1.0, 0.0).astype(BF16)
    lane = lax.broadcasted_iota(jnp.int32, base_ref.shape, 1)
    eq_seen = jnp.zeros((ne, 1), F32)
    sel_seen = jnp.zeros((ne, 1), F32)
    base = jnp.zeros(base_ref.shape, F32)
    for b in range(n // tb):
        blk = pltpu.bitcast(aff_ref[:, b * tb:(b + 1) * tb], jnp.int32)
        eq = jnp.where(blk == thr, 1.0, 0.0)
        eq_rank = jnp.dot(eq.astype(BF16), tri, preferred_element_type=F32) + eq_seen
        sel = jnp.where(blk > thr, 1.0, jnp.where(eq_rank < need, eq, 0.0))
        pos = jnp.dot(sel.astype(BF16), tri, preferred_element_type=F32) + sel_seen
        slot_ref[:, b * tb:(b + 1) * tb] = jnp.where(sel > 0.0, pos, -1.0).astype(jnp.int32)
        base = jnp.where(lane == b, sel_seen, base)
        eq_seen = eq_seen + jnp.sum(eq, axis=1, keepdims=True)
        sel_seen = sel_seen + jnp.sum(sel, axis=1, keepdims=True)
    base = jnp.where(lane == n // tb, sel_seen, base)
    base_ref[...] = base.astype(jnp.int32)


def _select(afft, cap, tb):
    ne, n = afft.shape
    lanes = 128
    assert n // tb < lanes
    kern = functools.partial(_select_kernel, cap=cap, tb=tb)
    return pl.pallas_call(
        kern,
        out_shape=[jax.ShapeDtypeStruct((ne, n), jnp.int32),
                   jax.ShapeDtypeStruct((ne, lanes), jnp.int32)],
        compiler_params=pltpu.CompilerParams(vmem_limit_bytes=VMEM_LIMIT_BYTES),
        name="expert_choice_select",
    )(afft)


def _pick_list_kernel(base_ref, slot_ref, aff_ref, tok_ref, gw_ref, *, cap, nblk, chunk):
    eb = pl.program_id(0)
    b = pl.program_id(1)

    @pl.when(b == 0)
    def _():
        tok_ref[...] = jnp.zeros_like(tok_ref)
        gw_ref[...] = jnp.zeros_like(gw_ref)

    tb = slot_ref.shape[2]
    token = (lax.broadcasted_iota(jnp.int32, (1, tb), 1) + b * tb).astype(F32)
    for k in range(tok_ref.shape[0]):
        e = eb * tok_ref.shape[0] + k
        first = (base_ref[e * (nblk + 1) + b] // SUBLANES) * SUBLANES
        end = base_ref[e * (nblk + 1) + b + 1]
        slot = slot_ref[k]
        aff = aff_ref[k]

        def body(c, carry, k=k, first=first, slot=slot, aff=aff):
            lo = first + c * chunk
            st = pl.multiple_of(jnp.minimum(lo, cap - chunk), SUBLANES)
            rows = lax.broadcasted_iota(jnp.int32, (chunk, tb), 0) + st
            hit = jnp.where(rows == slot, jnp.where(rows >= lo, 1.0, 0.0), 0.0)
            tok_ref[k, pl.ds(st, chunk), :] += jnp.sum(hit * token, axis=1,
                                                       keepdims=True).astype(jnp.int32)
            gw_ref[k, pl.ds(st, chunk), :] += jnp.sum(hit * aff, axis=1, keepdims=True)
            return carry

        lax.fori_loop(0, (end - first + chunk - 1) // chunk, body, 0)


def _pick_list(base_flat, slot3, aff3, *, cap, tb, chunk, experts_per_step):
    ne, _, n = slot3.shape
    nblk = n // tb
    eps = experts_per_step
    assert cap >= chunk and (cap - chunk) % SUBLANES == 0 and chunk % SUBLANES == 0
    kern = functools.partial(_pick_list_kernel, cap=cap, nblk=nblk, chunk=chunk)
    return pl.pallas_call(
        kern,
        grid_spec=pltpu.PrefetchScalarGridSpec(
            num_scalar_prefetch=1,
            grid=(ne // eps, nblk),
            in_specs=[pl.BlockSpec((eps, 1, tb), lambda e, b, base: (e, 0, b)),
                      pl.BlockSpec((eps, 1, tb), lambda e, b, base: (e, 0, b))],
            out_specs=[pl.BlockSpec((eps, cap, 1), lambda e, b, base: (e, 0, 0)),
                       pl.BlockSpec((eps, cap, 1), lambda e, b, base: (e, 0, 0))],
        ),
        out_shape=[jax.ShapeDtypeStruct((ne, cap, 1), jnp.int32),
                   jax.ShapeDtypeStruct((ne, cap, 1), F32)],
        compiler_params=_params("parallel", "arbitrary"),
        name="expert_pick_list",
    )(base_flat, slot3, aff3)


def _row_gather_sparsecore(table, rows, *, window):
    num, width = rows.shape[0], table.shape[1]
    mesh = plsc.VectorSubcoreMesh(core_axis_name="c", subcore_axis_name="s")

    @pl.kernel(out_type=jax.ShapeDtypeStruct((num, width), table.dtype), mesh=mesh,
               name="expert_row_gather")
    def gather(table_hbm, rows_hbm, out_hbm):
        def body(rows_vmem, out_vmem):
            pltpu.sync_copy(table_hbm.at[rows_vmem.at[0]], out_vmem)

        pltpu.emit_pipeline(
            body,
            grid=(num // window,),
            in_specs=[pl.BlockSpec((1, window), index_map=lambda i: (0, i))],
            out_specs=[pl.BlockSpec((window, width), index_map=lambda i: (i, 0))],
            core_axis_name=("c", "s"),
            dimension_semantics=(pltpu.PARALLEL,),
        )(rows_hbm, out_hbm)

    return gather(table, rows.reshape(1, num))


def _gather(tok, hp, *, window):
    ne, cap, _ = tok.shape
    pieces, n, pw = hp.shape
    rows = (jnp.arange(pieces, dtype=jnp.int32)[:, None] * n + tok.reshape(1, ne * cap)).reshape(-1)
    out = _row_gather_sparsecore(hp.reshape(pieces * n, pw), rows, window=window)
    return out.reshape(pieces, ne, cap, pw)


def _ffn_up_kernel(xp_ref, wg_ref, wu_ref, hid_ref, xe_ref, *, sub):
    @pl.when(pl.program_id(1) == 0)
    def _():
        pieces, _, pw = xp_ref.shape
        for c in range(pieces):
            lo, hi = _unpack_bf16_pair(xp_ref[c])
            xe_ref[:, c * pw:(c + 1) * pw] = lo
            xe_ref[:, (pieces + c) * pw:(pieces + c + 1) * pw] = hi

    x = xe_ref[...]
    for s in range(hid_ref.shape[1] // sub):
        cols = slice(s * sub, (s + 1) * sub)
        g = jnp.dot(x, wg_ref[:, cols].astype(BF16), preferred_element_type=F32)
        u = jnp.dot(x, wu_ref[:, cols].astype(BF16), preferred_element_type=F32)
        hid_ref[:, cols] = ((g * _sigmoid(g)) * u).astype(hid_ref.dtype)


def _ffn_down_kernel(hid_ref, gw_ref, wd_ref, ye_ref, *, sub):
    h = hid_ref[...]
    for s in range(ye_ref.shape[1] // sub):
        cols = slice(s * sub, (s + 1) * sub)
        y = jnp.dot(h, wd_ref[:, cols].astype(BF16), preferred_element_type=F32)
        ye_ref[:, cols] = (y * gw_ref[...]).astype(ye_ref.dtype)


def _ffn(xp, gw, wg, wu, wd, tf, tn):
    pieces, ne, cap, pw = xp.shape
    d, f = wg.shape[1], wg.shape[2]
    assert d == 2 * pieces * pw
    params = pltpu.CompilerParams(dimension_semantics=("parallel", "arbitrary"),
                                  vmem_limit_bytes=FFN_VMEM_LIMIT_BYTES)
    hid = pl.pallas_call(
        functools.partial(_ffn_up_kernel, sub=min(MXU_COLS, tf)),
        grid=(ne, f // tf),
        in_specs=[pl.BlockSpec((pieces, None, cap, pw), lambda e, j: (0, e, 0, 0),
                               pipeline_mode=pl.Buffered(1)),
                  pl.BlockSpec((None, d, tf), lambda e, j: (e, 0, j)),
                  pl.BlockSpec((None, d, tf), lambda e, j: (e, 0, j))],
        out_specs=pl.BlockSpec((None, cap, tf), lambda e, j: (e, 0, j)),
        out_shape=jax.ShapeDtypeStruct((ne, cap, f), BF16),
        scratch_shapes=[pltpu.VMEM((cap, d), BF16)],
        compiler_params=params,
        name="expert_ffn_up",
    )(xp, wg, wu)
    return pl.pallas_call(
        functools.partial(_ffn_down_kernel, sub=min(MXU_COLS, tn)),
        grid=(ne, d // tn),
        in_specs=[pl.BlockSpec((None, cap, f), lambda e, j: (e, 0, 0)),
                  pl.BlockSpec((None, cap, 1), lambda e, j: (e, 0, 0)),
                  pl.BlockSpec((None, f, tn), lambda e, j: (e, 0, j))],
        out_specs=pl.BlockSpec((None, cap, tn), lambda e, j: (e, 0, j)),
        out_shape=jax.ShapeDtypeStruct((ne, cap, d), BF16),
        compiler_params=params,
        name="expert_ffn_down",
    )(hid, gw, wd)


def _expert_column(ref, e):
    lane = lax.broadcasted_iota(jnp.int32, ref.shape, 1)
    return jnp.sum(jnp.where(lane == e, ref[...], 0), axis=1, keepdims=True)


def _combine_any_kernel(base_ref, slot_ref, x_ref, ya_ref, yb_ref, g_ref, o_ref, *, cap, nblk):
    b = pl.program_id(0)
    e = pl.program_id(1)
    ne = pl.num_programs(1)

    @pl.when(e == 0)
    def _():
        o_ref[...] = x_ref[...]

    tb = x_ref.shape[0]
    blk = ya_ref.shape[0]
    start = jnp.minimum(base_ref[e * (nblk + 1) + b] // blk, cap // blk - 1) * blk
    slot = _expert_column(slot_ref, e)
    cols = lax.broadcasted_iota(jnp.int32, (tb, blk), 1) + start
    oh_a = jnp.where(cols == slot, 1.0, 0.0).astype(BF16)
    oh_b = jnp.where(cols + blk == slot, 1.0, 0.0).astype(BF16)
    o_ref[...] += (jnp.dot(oh_a, ya_ref[...], preferred_element_type=F32)
                   + jnp.dot(oh_b, yb_ref[...], preferred_element_type=F32))

    @pl.when(e == ne - 1)
    def _():
        o_ref[...] = _rms_scale(o_ref[...]) * g_ref[...]


def _combine_any(base_flat, slot_tok, x2, ye, g_final, *, cap, tb):
    n, d = x2.shape
    ne = ye.shape[0]
    nblk = n // tb
    blk = tb
    last = cap // blk - 1

    def ya_map(b, e, base):
        return (e, jnp.minimum(base[e * (nblk + 1) + b] // blk, last), 0)

    def yb_map(b, e, base):
        return (e, jnp.minimum(jnp.minimum(base[e * (nblk + 1) + b] // blk, last) + 1, last), 0)

    kern = functools.partial(_combine_any_kernel, cap=cap, nblk=nblk)
    return pl.pallas_call(
        kern,
        grid_spec=pltpu.PrefetchScalarGridSpec(
            num_scalar_prefetch=1,
            grid=(nblk, ne),
            in_specs=[pl.BlockSpec((tb, ne), lambda b, e, base: (b, 0)),
                      pl.BlockSpec((tb, d), lambda b, e, base: (b, 0)),
                      pl.BlockSpec((None, blk, d), ya_map),
                      pl.BlockSpec((None, blk, d), yb_map),
                      pl.BlockSpec((1, d), lambda b, e, base: (0, 0))],
            out_specs=pl.BlockSpec((tb, d), lambda b, e, base: (b, 0)),
        ),
        out_shape=jax.ShapeDtypeStruct((n, d), F32),
        compiler_params=_params("parallel", "arbitrary"),
        name="expert_combine_any",
    )(base_flat, slot_tok, x2, ye, ye, g_final.reshape(1, d))


def _window_start(base, e, b, *, nblk, cap, win):
    tile = jnp.minimum(base[e * (nblk + 1) + b] // BF16_ROW_TILE, (cap - win) // BF16_ROW_TILE)
    return tile * BF16_ROW_TILE


def _combine_windows_kernel(base_ref, slot_ref, x_ref, g_ref, *rest, cap, nblk, win):
    y_refs, o_ref = rest[:-1], rest[-1]
    b = pl.program_id(0)
    s = pl.program_id(1)
    per = len(y_refs)

    @pl.when(s == 0)
    def _():
        o_ref[...] = x_ref[...]

    tb = x_ref.shape[0]
    col = lax.broadcasted_iota(jnp.int32, (tb, win), 1)
    onehots = []
    for k in range(per):
        e = s * per + k
        start = _window_start(base_ref, e, b, nblk=nblk, cap=cap, win=win)
        onehots.append(jnp.where(col + start == _expert_column(slot_ref, e), 1.0, 0.0).astype(BF16))
    y = jnp.concatenate([r[...] for r in y_refs], axis=0)
    o_ref[...] += jnp.dot(jnp.concatenate(onehots, axis=1), y, preferred_element_type=F32)

    @pl.when(s == pl.num_programs(1) - 1)
    def _():
        o_ref[...] = _rms_scale(o_ref[...]) * g_ref[...]


def _combine_windows(base_flat, slot_tok, x2, ye, g_final, *, cap, tb, win, experts_per_step):
    n, d = x2.shape
    ne = ye.shape[0]
    nblk = n // tb
    per = experts_per_step
    assert ne % per == 0

    def y_map(k):
        def index(b, s, base):
            e = s * per + k
            return (e, _window_start(base, e, b, nblk=nblk, cap=cap, win=win), 0)
        return index

    kern = functools.partial(_combine_windows_kernel, cap=cap, nblk=nblk, win=win)
    return pl.pallas_call(
        kern,
        grid_spec=pltpu.PrefetchScalarGridSpec(
            num_scalar_prefetch=1,
            grid=(nblk, ne // per),
            in_specs=[pl.BlockSpec((tb, ne), lambda b, s, base: (b, 0)),
                      pl.BlockSpec((tb, d), lambda b, s, base: (b, 0)),
                      pl.BlockSpec((1, d), lambda b, s, base: (0, 0))]
                     + [pl.BlockSpec((None, pl.Element(win), pl.Element(d)), y_map(k))
                        for k in range(per)],
            out_specs=pl.BlockSpec((tb, d), lambda b, s, base: (b, 0)),
        ),
        out_shape=jax.ShapeDtypeStruct((n, d), F32),
        compiler_params=_params("parallel", "arbitrary"),
        name="expert_combine_windows",
    )(base_flat, slot_tok, x2, g_final.reshape(1, d), *([ye] * per))


def _combine(base, slot_tok, x2, ye, g_final, *, cap, tb, win):
    nblk = x2.shape[0] // tb
    base_flat = base[:, :nblk + 1].reshape(-1)
    first = (base[:, :nblk] // BF16_ROW_TILE) * BF16_ROW_TILE
    fits = jnp.all(base[:, 1:nblk + 1] - jnp.minimum(first, cap - win) <= win)
    return lax.cond(
        fits,
        lambda: _combine_windows(base_flat, slot_tok, x2, ye, g_final, cap=cap, tb=tb, win=win,
                                 experts_per_step=8),
        lambda: _combine_any(base_flat, slot_tok, x2, ye, g_final, cap=cap, tb=tb))


def _encode(x3, mem3, p):
    bsz, seq, d = x3.shape
    n = bsz * seq
    n_mem = mem3.shape[1]
    x = x3.reshape(n, d)
    mem = mem3.reshape(bsz * n_mem, d)

    gmlp_w = p["gmlp_norm_g"].shape[0]
    conv_w = p["conv_w"].shape[1]
    xattn_w = p["w_mem_kv"].shape[1] // 2
    off_cb = 2 * gmlp_w
    off_g = off_cb + 3 * conv_w + xattn_w
    ne = p["w_router"].shape[1]
    cap = CAPACITY_FACTOR * n // ne

    def col_tile(cols):
        return next(t for t in (1024, 512, 256, 128) if cols % t == 0 and t <= cols)

    tm = min(1024, n)
    ts = min(256, seq)
    tb = 256 if cap >= 256 + BF16_ROW_TILE else 128

    h = _rmsnorm_cast(x, p["norm_mix_g"], ts)
    proj = _proj(h, p["w_in"], tm, col_tile(p["w_in"].shape[1]), off_cb, off_g, BF16)
    m = _rmsnorm_cast(mem, p["mem_norm_g"], n_mem)
    kv = _proj(m, p["w_mem_kv"], n_mem, col_tile(2 * xattn_w), 0, 2 * xattn_w, BF16)
    ycat = _branches(proj, kv, p["w_spatial"], p["b_spatial"], p["gmlp_norm_g"].reshape(1, gmlp_w),
                     p["conv_w"], seq=seq, ts=ts, gmlp_w=gmlp_w, conv_w=conv_w, xattn_w=xattn_w,
                     n_mem=n_mem)
    merged = _merge(ycat, p["w_branch"], proj, tm=tm, tn=min(512, col_tile(d)), off_g=off_g, d=d,
                    gmlp_w=gmlp_w, conv_w=conv_w)
    x2 = _outproj(merged, p["w_out"], x, tm, col_tile(d))

    hp, afft = _router(x2, p["norm_ffn_g"], p["w_router"], ts, min(SC_ROW_PIECES, d // 256))
    slot, base = _select(afft, cap, tb)
    nblk = n // tb
    base_flat = base[:, :nblk + 1].reshape(-1)
    tok, gw = _pick_list(base_flat, slot.reshape(ne, 1, n), afft.reshape(ne, 1, n), cap=cap, tb=tb,
                         chunk=PICK_CHUNK_ROWS, experts_per_step=min(8, ne))
    xp = _gather(tok, hp, window=SC_GATHER_WINDOW)
    ye = _ffn(xp, gw, p["w_gate"], p["w_up"], p["w_down"], min(FFN_UP_COLS, d), min(FFN_DOWN_COLS, d))
    y = _combine(base, slot.T, x2, ye, p["final_norm_g"], cap=cap, tb=tb,
                 win=min(COMBINE_WINDOW_ROWS, tb))
    return y.reshape(bsz, seq, d)


def kernel(x_prompt, x_sample, mem_prompt, mem_sample, norm_mix_g, w_in, gmlp_norm_g, w_spatial,
           b_spatial, conv_w, mem_norm_g, w_mem_kv, w_branch_a, w_branch_b, w_branch_x, w_out,
           norm_ffn_g, w_router, w_gate, w_up, w_down, final_norm_g):
    assert w_in.shape[0] == 1, "one layer"
    groups, chunk = b_spatial.shape[1], b_spatial.shape[2]
    gc = gmlp_norm_g.shape[1] // groups
    p = {
        "norm_mix_g": norm_mix_g[0],
        "w_in": w_in[0].astype(BF16),
        "gmlp_norm_g": gmlp_norm_g[0],
        "w_spatial": w_spatial[0].astype(BF16),
        "b_spatial": jnp.broadcast_to(b_spatial[0][:, :, None], (groups, chunk, gc)),
        "conv_w": conv_w[0],
        "mem_norm_g": mem_norm_g[0],
        "w_mem_kv": w_mem_kv[0].astype(BF16),
        "w_branch": jnp.concatenate([w_branch_a[0], w_branch_b[0], w_branch_x[0]],
                                    axis=0).astype(BF16),
        "w_out": w_out[0].astype(BF16),
        "norm_ffn_g": norm_ffn_g[0],
        "w_router": w_router[0].astype(BF16),
        "w_gate": w_gate.reshape(w_gate.shape[1:]),
        "w_up": w_up.reshape(w_up.shape[1:]),
        "w_down": w_down.reshape(w_down.shape[1:]),
        "final_norm_g": final_norm_g,
    }
    return (_encode(x_prompt, mem_prompt, p), _encode(x_sample, mem_sample, p))
```

```python
import functools

import jax
import jax.numpy as jnp
from jax import lax
from jax.experimental import pallas as pl
from jax.experimental.pallas import tpu as pltpu
from jax.experimental.pallas import tpu_sc as plsc

EPS = 1e-6
XATTN_HEADS = 4
CAPACITY_FACTOR = 2
BF16_ROW_TILE = 16
SUBLANES = 8
PICK_CHUNK_ROWS = 80
SC_ROW_PIECES = 8
SC_GATHER_WINDOW = 128
COMBINE_WINDOW_ROWS = 128
VMEM_LIMIT_BYTES = 56 * 1024 * 1024
FFN_VMEM_LIMIT_BYTES = 60 * 1024 * 1024
FFN_UP_COLS = 512
FFN_DOWN_COLS = 512

F32 = jnp.float32
BF16 = jnp.bfloat16


def _params(*semantics):
    return pltpu.CompilerParams(dimension_semantics=semantics,
                                vmem_limit_bytes=VMEM_LIMIT_BYTES)


GELU_K = 0.7978845608028654
GELU_CUBIC = 0.044715
LHS_SINGLE_BUFFER_BYTES = 16 * 1024 * 1024
MXU_COLS = 256


def _sigmoid(x):
    return 1.0 / (1.0 + jnp.exp(-x))


def _rms_scale(x):
    return x * lax.rsqrt(jnp.mean(x * x, axis=-1, keepdims=True) + EPS)


def _rmsnorm_cast_kernel(x_ref, g_ref, o_ref):
    o_ref[...] = (_rms_scale(x_ref[...]) * g_ref[...]).astype(o_ref.dtype)


def _rmsnorm_cast(x, g, tm):
    n, d = x.shape
    return pl.pallas_call(
        _rmsnorm_cast_kernel,
        grid=(n // tm,),
        in_specs=[pl.BlockSpec((tm, d), lambda i: (i, 0)),
                  pl.BlockSpec((1, d), lambda i: (0, 0))],
        out_specs=pl.BlockSpec((tm, d), lambda i: (i, 0)),
        out_shape=jax.ShapeDtypeStruct((n, d), BF16),
        compiler_params=_params("parallel"),
        name="rmsnorm_cast",
    )(x, g.reshape(1, d))


def _gelu_tanh(x):
    return (0.5 * x) * (1.0 + jnp.tanh(GELU_K * (x + GELU_CUBIC * (x * x * x))))


def _sigmoid_tanh(x):
    return 0.5 * (1.0 + jnp.tanh(0.5 * x))


def _proj_kernel(h_ref, w_ref, o_ref, *, sub, gelu_steps, sigmoid_from_step):
    j = pl.program_id(1)

    def tile(epilogue):
        h = h_ref[...]
        for s in range(o_ref.shape[1] // sub):
            cols = slice(s * sub, (s + 1) * sub)
            x = jnp.dot(h, w_ref[:, cols], preferred_element_type=F32)
            o_ref[:, cols] = epilogue(x).astype(o_ref.dtype)

    pl.when(j < gelu_steps)(lambda: tile(_gelu_tanh))
    pl.when(jnp.logical_and(j >= gelu_steps, j < sigmoid_from_step))(lambda: tile(lambda x: x))
    pl.when(j >= sigmoid_from_step)(lambda: tile(_sigmoid_tanh))


def _proj(h, w, tm, tn, gelu_cols, sigmoid_from_col, out_dtype):
    n, d = h.shape
    cols = w.shape[1]
    assert gelu_cols % tn == 0 and sigmoid_from_col % tn == 0
    kern = functools.partial(_proj_kernel, sub=min(MXU_COLS, tn), gelu_steps=gelu_cols // tn,
                             sigmoid_from_step=sigmoid_from_col // tn)
    lhs_mode = {"pipeline_mode": pl.Buffered(1)} if tm * d * h.dtype.itemsize >= LHS_SINGLE_BUFFER_BYTES else {}
    return pl.pallas_call(
        kern,
        grid=(n // tm, cols // tn),
        in_specs=[pl.BlockSpec((tm, d), lambda i, j: (i, 0), **lhs_mode),
                  pl.BlockSpec((d, tn), lambda i, j: (0, j))],
        out_specs=pl.BlockSpec((tm, tn), lambda i, j: (i, j)),
        out_shape=jax.ShapeDtypeStruct((n, cols), out_dtype),
        compiler_params=pltpu.CompilerParams(dimension_semantics=("parallel", "arbitrary"),
                                             vmem_limit_bytes=FFN_VMEM_LIMIT_BYTES),
        name="in_proj",
    )(h, w)


def _branch_kernel(u_ref, v_ref, cb_ref, cc_ref, ch_ref, q_ref, ccp_ref, chp_ref, ccn_ref, chn_ref,
                   kv_ref, wsp_ref, bsp_ref, gg_ref, cw_ref, y_ref, *, seq, chunk, groups, heads):
    i = pl.program_id(0)
    ts = u_ref.shape[0]
    gw = u_ref.shape[1]
    gc = gw // groups
    cw = cb_ref.shape[1]
    xw = q_ref.shape[1]
    hd = xw // heads

    vn = (_rms_scale(v_ref[...].astype(F32)) * gg_ref[...]).astype(BF16)
    for c in range(ts // chunk):
        rows = slice(c * chunk, (c + 1) * chunk)
        for g in range(groups):
            cols = slice(g * gc, (g + 1) * gc)
            mixed = jnp.dot(wsp_ref[g], vn[rows, cols], preferred_element_type=F32) + bsp_ref[g]
            y_ref[rows, cols] = (u_ref[rows, cols].astype(F32) * mixed).astype(y_ref.dtype)

    pos = (i * ts) % seq
    z = cc_ref[...].astype(F32) * ch_ref[...].astype(F32)
    sub = ccp_ref.shape[0]
    z_before = jnp.where(pos == 0, 0.0, ccp_ref[sub - 1:sub, :].astype(F32)
                         * chp_ref[sub - 1:sub, :].astype(F32))
    z_after = jnp.where(pos + ts == seq, 0.0, ccn_ref[0:1, :].astype(F32)
                        * chn_ref[0:1, :].astype(F32))
    row = lax.broadcasted_iota(jnp.int32, z.shape, 0)
    z_prev = jnp.where(row == 0, z_before, pltpu.roll(z, 1, 0))
    z_next = jnp.where(row == ts - 1, z_after, pltpu.roll(z, ts - 1, 0))
    conv = z_prev * cw_ref[0:1, :] + z * cw_ref[1:2, :] + z_next * cw_ref[2:3, :]
    y_ref[:, gw:gw + cw] = (cb_ref[...].astype(F32) * conv).astype(y_ref.dtype)

    scale = hd ** -0.5
    for h in range(heads):
        cols = slice(h * hd, (h + 1) * hd)
        qh = q_ref[:, cols]
        kh = kv_ref[:, cols]
        vh = kv_ref[:, xw + h * hd: xw + (h + 1) * hd]
        s = lax.dot_general(qh, kh, (((1,), (1,)), ((), ())), preferred_element_type=F32) * scale
        e = jnp.exp(s - jnp.max(s, axis=-1, keepdims=True))
        p = (e / jnp.sum(e, axis=-1, keepdims=True)).astype(BF16)
        yh = jnp.dot(p, vh, preferred_element_type=F32)
        y_ref[:, gw + cw + h * hd: gw + cw + (h + 1) * hd] = yh.astype(y_ref.dtype)


def _branches(proj, kv, wsp, bsp, gg, cw, *, seq, ts, gmlp_w, conv_w, xattn_w, n_mem):
    n = proj.shape[0]
    chunk = wsp.shape[1]
    groups = wsp.shape[0]
    sub = BF16_ROW_TILE
    off_cb = 2 * gmlp_w
    cb_blk = off_cb // conv_w
    q_blk = (off_cb + 3 * conv_w) // xattn_w
    last = n // sub - 1
    tsb = ts // sub

    def prev_map(col):
        return lambda i: (jnp.maximum(i * tsb - 1, 0), col)

    def next_map(col):
        return lambda i: (jnp.minimum((i + 1) * tsb, last), col)

    kern = functools.partial(_branch_kernel, seq=seq, chunk=chunk, groups=groups, heads=XATTN_HEADS)
    return pl.pallas_call(
        kern,
        grid=(n // ts,),
        in_specs=[
            pl.BlockSpec((ts, gmlp_w), lambda i: (i, 0)),
            pl.BlockSpec((ts, gmlp_w), lambda i: (i, 1)),
            pl.BlockSpec((ts, conv_w), lambda i: (i, cb_blk)),
            pl.BlockSpec((ts, conv_w), lambda i: (i, cb_blk + 1)),
            pl.BlockSpec((ts, conv_w), lambda i: (i, cb_blk + 2)),
            pl.BlockSpec((ts, xattn_w), lambda i: (i, q_blk)),
            pl.BlockSpec((sub, conv_w), prev_map(cb_blk + 1)),
            pl.BlockSpec((sub, conv_w), prev_map(cb_blk + 2)),
            pl.BlockSpec((sub, conv_w), next_map(cb_blk + 1)),
            pl.BlockSpec((sub, conv_w), next_map(cb_blk + 2)),
            pl.BlockSpec((n_mem, 2 * xattn_w), lambda i: ((i * ts) // seq, 0)),
            pl.BlockSpec(wsp.shape, lambda i: (0, 0, 0)),
            pl.BlockSpec(bsp.shape, lambda i: (0, 0, 0)),
            pl.BlockSpec((1, gmlp_w), lambda i: (0, 0)),
            pl.BlockSpec(cw.shape, lambda i: (0, 0)),
        ],
        out_specs=pl.BlockSpec((ts, gmlp_w + conv_w + xattn_w), lambda i: (i, 0)),
        out_shape=jax.ShapeDtypeStruct((n, gmlp_w + conv_w + xattn_w), BF16),
        compiler_params=_params("parallel"),
        name="branches",
    )(proj, proj, proj, proj, proj, proj, proj, proj, proj, proj, kv, wsp, bsp, gg, cw)


def _merge_kernel(y_ref, w_ref, ga_ref, gb_ref, gx_ref, o_ref, *, gmlp_w, conv_w, sub):
    ya = y_ref[:, :gmlp_w]
    yb = y_ref[:, gmlp_w:gmlp_w + conv_w]
    yx = y_ref[:, gmlp_w + conv_w:]
    for s in range(o_ref.shape[1] // sub):
        cols = slice(s * sub, (s + 1) * sub)
        a = jnp.dot(ya, w_ref[:gmlp_w, cols], preferred_element_type=F32)
        b = jnp.dot(yb, w_ref[gmlp_w:gmlp_w + conv_w, cols], preferred_element_type=F32)
        x = jnp.dot(yx, w_ref[gmlp_w + conv_w:, cols], preferred_element_type=F32)
        o_ref[:, cols] = (ga_ref[:, cols].astype(F32) * a + gb_ref[:, cols].astype(F32) * b
                          + gx_ref[:, cols].astype(F32) * x).astype(o_ref.dtype)


def _merge(ycat, w_br, proj, *, tm, tn, off_g, d, gmlp_w, conv_w):
    n, k = ycat.shape
    g0 = off_g // tn
    gstep = d // tn
    kern = functools.partial(_merge_kernel, gmlp_w=gmlp_w, conv_w=conv_w, sub=min(MXU_COLS, tn))
    return pl.pallas_call(
        kern,
        grid=(n // tm, d // tn),
        in_specs=[pl.BlockSpec((tm, k), lambda i, j: (i, 0)),
                  pl.BlockSpec((k, tn), lambda i, j: (0, j)),
                  pl.BlockSpec((tm, tn), lambda i, j: (i, g0 + j)),
                  pl.BlockSpec((tm, tn), lambda i, j: (i, g0 + gstep + j)),
                  pl.BlockSpec((tm, tn), lambda i, j: (i, g0 + 2 * gstep + j))],
        out_specs=pl.BlockSpec((tm, tn), lambda i, j: (i, j)),
        out_shape=jax.ShapeDtypeStruct((n, d), BF16),
        compiler_params=_params("parallel", "arbitrary"),
        name="merge",
    )(ycat, w_br, proj, proj, proj)


def _outproj_kernel(m_ref, w_ref, x_ref, o_ref, *, sub):
    m = m_ref[...]
    for s in range(o_ref.shape[1] // sub):
        cols = slice(s * sub, (s + 1) * sub)
        o_ref[:, cols] = x_ref[:, cols] + jnp.dot(m, w_ref[:, cols], preferred_element_type=F32)


def _outproj(merged, w_out, x, tm, tn):
    n, d = x.shape
    return pl.pallas_call(
        functools.partial(_outproj_kernel, sub=min(MXU_COLS, tn)),
        grid=(n // tm, d // tn),
        in_specs=[pl.BlockSpec((tm, d), lambda i, j: (i, 0)),
                  pl.BlockSpec((d, tn), lambda i, j: (0, j)),
                  pl.BlockSpec((tm, tn), lambda i, j: (i, j))],
        out_specs=pl.BlockSpec((tm, tn), lambda i, j: (i, j)),
        out_shape=jax.ShapeDtypeStruct((n, d), F32),
        compiler_params=_params("parallel", "arbitrary"),
        name="out_proj",
    )(merged, w_out, x)


def _pack_bf16_pair(lo, hi):
    lo_bits = pltpu.bitcast(lo.astype(BF16).astype(F32), jnp.uint32)
    hi_bits = pltpu.bitcast(hi.astype(BF16).astype(F32), jnp.uint32)
    return lax.shift_right_logical(lo_bits, jnp.uint32(16)) | (hi_bits & jnp.uint32(0xFFFF0000))


def _unpack_bf16_pair(words):
    lo = pltpu.bitcast(lax.shift_left(words, jnp.uint32(16)), F32)
    hi = pltpu.bitcast(words & jnp.uint32(0xFFFF0000), F32)
    return lo.astype(BF16), hi.astype(BF16)


def _router_kernel(x_ref, g_ref, wrt_ref, hp_ref, afft_ref):
    hf = _rms_scale(x_ref[...]) * g_ref[...]
    half = hf.shape[1] // 2
    words = _pack_bf16_pair(hf[:, :half], hf[:, half:])
    pw = hp_ref.shape[2]
    for c in range(hp_ref.shape[0]):
        hp_ref[c] = words[:, c * pw:(c + 1) * pw]
    lt = lax.dot_general(wrt_ref[...], hf.astype(BF16), (((1,), (1,)), ((), ())),
                         preferred_element_type=F32)
    et = jnp.exp(lt - jnp.max(lt, axis=0, keepdims=True))
    afft_ref[...] = et / jnp.sum(et, axis=0, keepdims=True)


def _router(x2, g, wr, tm, pieces):
    n, d = x2.shape
    ne = wr.shape[1]
    pw = d // 2 // pieces
    return pl.pallas_call(
        _router_kernel,
        grid=(n // tm,),
        in_specs=[pl.BlockSpec((tm, d), lambda i: (i, 0)),
                  pl.BlockSpec((1, d), lambda i: (0, 0)),
                  pl.BlockSpec((ne, d), lambda i: (0, 0))],
        out_specs=[pl.BlockSpec((pieces, tm, pw), lambda i: (0, i, 0)),
                   pl.BlockSpec((ne, tm), lambda i: (0, i))],
        out_shape=[jax.ShapeDtypeStruct((pieces, n, pw), jnp.uint32),
                   jax.ShapeDtypeStruct((ne, n), F32)],
        compiler_params=_params("parallel"),
        name="ffn_norm_router",
    )(x2, g.reshape(1, d), wr.T)


def _select_kernel(aff_ref, slot_ref, base_ref, *, cap, tb):
    ne, n = aff_ref.shape
    bits = pltpu.bitcast(aff_ref[...], jnp.int32)
    capf = jnp.float32(cap)

    def search(k, t):
        cand = t | lax.shift_left(jnp.int32(1), 30 - k)
        cnt = jnp.sum(jnp.where(bits >= cand, 1.0, 0.0), axis=1, keepdims=True)
        return jnp.where(cnt >= capf, cand, t)

    thr = lax.fori_loop(0, 31, search, jnp.zeros((ne, 1), jnp.int32))
    n_gt = jnp.sum(jnp.where(bits > thr, 1.0, 0.0), axis=1, keepdims=True)
    need = capf - n_gt

    r = lax.broadcasted_iota(jnp.int32, (tb, tb), 0)
    c = lax.broadcasted_iota(jnp.int32, (tb, tb), 1)
    tri = jnp.where(r < c, 1.0, 0.0).astype(BF16)
    lane = lax.broadcasted_iota(jnp.int32, base_ref.shape, 1)
    eq_seen = jnp.zeros((ne, 1), F32)
    sel_seen = jnp.zeros((ne, 1), F32)
    base = jnp.zeros(base_ref.shape, F32)
    for b in range(n // tb):
        blk = pltpu.bitcast(aff_ref[:, b * tb:(b + 1) * tb], jnp.int32)
        eq = jnp.where(blk == thr, 1.0, 0.0)
        eq_rank = jnp.dot(eq.astype(BF16), tri, preferred_element_type=F32) + eq_seen
        sel = jnp.where(blk > thr, 1.0, jnp.where(eq_rank < need, eq, 0.0))
        pos = jnp.dot(sel.astype(BF16), tri, preferred_element_type=F32) + sel_seen
        slot_ref[:, b * tb:(b + 1) * tb] = jnp.where(sel > 0.0, pos, -1.0).astype(jnp.int32)
        base = jnp.where(lane == b, sel_seen, base)
        eq_seen = eq_seen + jnp.sum(eq, axis=1, keepdims=True)
        sel_seen = sel_seen + jnp.sum(sel, axis=1, keepdims=True)
    base = jnp.where(lane == n // tb, sel_seen, base)
    base_ref[...] = base.astype(jnp.int32)


def _select(afft, cap, tb):
    ne, n = afft.shape
    lanes = 128
    assert n // tb < lanes
    kern = functools.partial(_select_kernel, cap=cap, tb=tb)
    return pl.pallas_call(
        kern,
        out_shape=[jax.ShapeDtypeStruct((ne, n), jnp.int32),
                   jax.ShapeDtypeStruct((ne, lanes), jnp.int32)],
        compiler_params=pltpu.CompilerParams(vmem_limit_bytes=VMEM_LIMIT_BYTES),
        name="expert_choice_select",
    )(afft)


def _pick_list_kernel(base_ref, slot_ref, aff_ref, tok_ref, gw_ref, *, cap, nblk, chunk):
    eb = pl.program_id(0)
    b = pl.program_id(1)

    @pl.when(b == 0)
    def _():
        tok_ref[...] = jnp.zeros_like(tok_ref)
        gw_ref[...] = jnp.zeros_like(gw_ref)

    tb = slot_ref.shape[2]
    token = (lax.broadcasted_iota(jnp.int32, (1, tb), 1) + b * tb).astype(F32)
    for k in range(tok_ref.shape[0]):
        e = eb * tok_ref.shape[0] + k
        first = (base_ref[e * (nblk + 1) + b] // SUBLANES) * SUBLANES
        end = base_ref[e * (nblk + 1) + b + 1]
        slot = slot_ref[k]
        aff = aff_ref[k]

        def body(c, carry, k=k, first=first, slot=slot, aff=aff):
            lo = first + c * chunk
            st = pl.multiple_of(jnp.minimum(lo, cap - chunk), SUBLANES)
            rows = lax.broadcasted_iota(jnp.int32, (chunk, tb), 0) + st
            hit = jnp.where(rows == slot, jnp.where(rows >= lo, 1.0, 0.0), 0.0)
            tok_ref[k, pl.ds(st, chunk), :] += jnp.sum(hit * token, axis=1,
                                                       keepdims=True).astype(jnp.int32)
            gw_ref[k, pl.ds(st, chunk), :] += jnp.sum(hit * aff, axis=1, keepdims=True)
            return carry

        lax.fori_loop(0, (end - first + chunk - 1) // chunk, body, 0)


def _pick_list(base_flat, slot3, aff3, *, cap, tb, chunk, experts_per_step):
    ne, _, n = slot3.shape
    nblk = n // tb
    eps = experts_per_step
    assert cap >= chunk and (cap - chunk) % SUBLANES == 0 and chunk % SUBLANES == 0
    kern = functools.partial(_pick_list_kernel, cap=cap, nblk=nblk, chunk=chunk)
    return pl.pallas_call(
        kern,
        grid_spec=pltpu.PrefetchScalarGridSpec(
            num_scalar_prefetch=1,
            grid=(ne // eps, nblk),
            in_specs=[pl.BlockSpec((eps, 1, tb), lambda e, b, base: (e, 0, b)),
                      pl.BlockSpec((eps, 1, tb), lambda e, b, base: (e, 0, b))],
            out_specs=[pl.BlockSpec((eps, cap, 1), lambda e, b, base: (e, 0, 0)),
                       pl.BlockSpec((eps, cap, 1), lambda e, b, base: (e, 0, 0))],
        ),
        out_shape=[jax.ShapeDtypeStruct((ne, cap, 1), jnp.int32),
                   jax.ShapeDtypeStruct((ne, cap, 1), F32)],
        compiler_params=_params("parallel", "arbitrary"),
        name="expert_pick_list",
    )(base_flat, slot3, aff3)


def _row_gather_sparsecore(table, rows, *, window):
    num, width = rows.shape[0], table.shape[1]
    mesh = plsc.VectorSubcoreMesh(core_axis_name="c", subcore_axis_name="s")

    @pl.kernel(out_type=jax.ShapeDtypeStruct((num, width), table.dtype), mesh=mesh,
               name="expert_row_gather")
    def gather(table_hbm, rows_hbm, out_hbm):
        def body(rows_vmem, out_vmem):
            pltpu.sync_copy(table_hbm.at[rows_vmem.at[0]], out_vmem)

        pltpu.emit_pipeline(
            body,
            grid=(num // window,),
            in_specs=[pl.BlockSpec((1, window), index_map=lambda i: (0, i))],
            out_specs=[pl.BlockSpec((window, width), index_map=lambda i: (i, 0))],
            core_axis_name=("c", "s"),
            dimension_semantics=(pltpu.PARALLEL,),
        )(rows_hbm, out_hbm)

    return gather(table, rows.reshape(1, num))


def _gather(tok, hp, *, window):
    ne, cap, _ = tok.shape
    pieces, n, pw = hp.shape
    rows = (jnp.arange(pieces, dtype=jnp.int32)[:, None] * n + tok.reshape(1, ne * cap)).reshape(-1)
    out = _row_gather_sparsecore(hp.reshape(pieces * n, pw), rows, window=window)
    return out.reshape(pieces, ne, cap, pw)


def _ffn_up_kernel(xp_ref, wg_ref, wu_ref, hid_ref, xe_ref, *, sub):
    @pl.when(pl.program_id(1) == 0)
    def _():
        pieces, _, pw = xp_ref.shape
        for c in range(pieces):
            lo, hi = _unpack_bf16_pair(xp_ref[c])
            xe_ref[:, c * pw:(c + 1) * pw] = lo
            xe_ref[:, (pieces + c) * pw:(pieces + c + 1) * pw] = hi

    x = xe_ref[...]
    for s in range(hid_ref.shape[1] // sub):
        cols = slice(s * sub, (s + 1) * sub)
        g = jnp.dot(x, wg_ref[:, cols].astype(BF16), preferred_element_type=F32)
        u = jnp.dot(x, wu_ref[:, cols].astype(BF16), preferred_element_type=F32)
        hid_ref[:, cols] = ((g * _sigmoid(g)) * u).astype(hid_ref.dtype)


def _ffn_down_kernel(hid_ref, gw_ref, wd_ref, ye_ref, *, sub):
    h = hid_ref[...]
    for s in range(ye_ref.shape[1] // sub):
        cols = slice(s * sub, (s + 1) * sub)
        y = jnp.dot(h, wd_ref[:, cols].astype(BF16), preferred_element_type=F32)
        ye_ref[:, cols] = (y * gw_ref[...]).astype(ye_ref.dtype)


def _ffn(xp, gw, wg, wu, wd, tf, tn):
    pieces, ne, cap, pw = xp.shape
    d, f = wg.shape[1], wg.shape[2]
    assert d == 2 * pieces * pw
    params = pltpu.CompilerParams(dimension_semantics=("parallel", "arbitrary"),
                                  vmem_limit_bytes=FFN_VMEM_LIMIT_BYTES)
    hid = pl.pallas_call(
        functools.partial(_ffn_up_kernel, sub=min(MXU_COLS, tf)),
        grid=(ne, f // tf),
        in_specs=[pl.BlockSpec((pieces, None, cap, pw), lambda e, j: (0, e, 0, 0),
                               pipeline_mode=pl.Buffered(1)),
                  pl.BlockSpec((None, d, tf), lambda e, j: (e, 0, j)),
                  pl.BlockSpec((None, d, tf), lambda e, j: (e, 0, j))],
        out_specs=pl.BlockSpec((None, cap, tf), lambda e, j: (e, 0, j)),
        out_shape=jax.ShapeDtypeStruct((ne, cap, f), BF16),
        scratch_shapes=[pltpu.VMEM((cap, d), BF16)],
        compiler_params=params,
        name="expert_ffn_up",
    )(xp, wg, wu)
    return pl.pallas_call(
        functools.partial(_ffn_down_kernel, sub=min(MXU_COLS, tn)),
        grid=(ne, d // tn),
        in_specs=[pl.BlockSpec((None, cap, f), lambda e, j: (e, 0, 0)),
                  pl.BlockSpec((None, cap, 1), lambda e, j: (e, 0, 0)),
                  pl.BlockSpec((None, f, tn), lambda e, j: (e, 0, j))],
        out_specs=pl.BlockSpec((None, cap, tn), lambda e, j: (e, 0, j)),
        out_shape=jax.ShapeDtypeStruct((ne, cap, d), BF16),
        compiler_params=params,
        name="expert_ffn_down",
    )(hid, gw, wd)


def _expert_column(ref, e):
    lane = lax.broadcasted_iota(jnp.int32, ref.shape, 1)
    return jnp.sum(jnp.where(lane == e, ref[...], 0), axis=1, keepdims=True)


def _combine_any_kernel(base_ref, slot_ref, x_ref, ya_ref, yb_ref, g_ref, o_ref, *, cap, nblk):
    b = pl.program_id(0)
    e = pl.program_id(1)
    ne = pl.num_programs(1)

    @pl.when(e == 0)
    def _():
        o_ref[...] = x_ref[...]

    tb = x_ref.shape[0]
    blk = ya_ref.shape[0]
    start = jnp.minimum(base_ref[e * (nblk + 1) + b] // blk, cap // blk - 1) * blk
    slot = _expert_column(slot_ref, e)
    cols = lax.broadcasted_iota(jnp.int32, (tb, blk), 1) + start
    oh_a = jnp.where(cols == slot, 1.0, 0.0).astype(BF16)
    oh_b = jnp.where(cols + blk == slot, 1.0, 0.0).astype(BF16)
    o_ref[...] += (jnp.dot(oh_a, ya_ref[...], preferred_element_type=F32)
                   + jnp.dot(oh_b, yb_ref[...], preferred_element_type=F32))

    @pl.when(e == ne - 1)
    def _():
        o_ref[...] = _rms_scale(o_ref[...]) * g_ref[...]


def _combine_any(base_flat, slot_tok, x2, ye, g_final, *, cap, tb):
    n, d = x2.shape
    ne = ye.shape[0]
    nblk = n // tb
    blk = tb
    last = cap // blk - 1

    def ya_map(b, e, base):
        return (e, jnp.minimum(base[e * (nblk + 1) + b] // blk, last), 0)

    def yb_map(b, e, base):
        return (e, jnp.minimum(jnp.minimum(base[e * (nblk + 1) + b] // blk, last) + 1, last), 0)

    kern = functools.partial(_combine_any_kernel, cap=cap, nblk=nblk)
    return pl.pallas_call(
        kern,
        grid_spec=pltpu.PrefetchScalarGridSpec(
            num_scalar_prefetch=1,
            grid=(nblk, ne),
            in_specs=[pl.BlockSpec((tb, ne), lambda b, e, base: (b, 0)),
                      pl.BlockSpec((tb, d), lambda b, e, base: (b, 0)),
                      pl.BlockSpec((None, blk, d), ya_map),
                      pl.BlockSpec((None, blk, d), yb_map),
                      pl.BlockSpec((1, d), lambda b, e, base: (0, 0))],
            out_specs=pl.BlockSpec((tb, d), lambda b, e, base: (b, 0)),
        ),
        out_shape=jax.ShapeDtypeStruct((n, d), F32),
        compiler_params=_params("parallel", "arbitrary"),
        name="expert_combine_any",
    )(base_flat, slot_tok, x2, ye, ye, g_final.reshape(1, d))


def _window_start(base, e, b, *, nblk, cap, win):
    tile = jnp.minimum(base[e * (nblk + 1) + b] // BF16_ROW_TILE, (cap - win) // BF16_ROW_TILE)
    return tile * BF16_ROW_TILE


def _combine_windows_kernel(base_ref, slot_ref, x_ref, g_ref, *rest, cap, nblk, win):
    y_refs, o_ref = rest[:-1], rest[-1]
    b = pl.program_id(0)
    s = pl.program_id(1)
    per = len(y_refs)

    @pl.when(s == 0)
    def _():
        o_ref[...] = x_ref[...]

    tb = x_ref.shape[0]
    col = lax.broadcasted_iota(jnp.int32, (tb, win), 1)
    onehots = []
    for k in range(per):
        e = s * per + k
        start = _window_start(base_ref, e, b, nblk=nblk, cap=cap, win=win)
        onehots.append(jnp.where(col + start == _expert_column(slot_ref, e), 1.0, 0.0).astype(BF16))
    y = jnp.concatenate([r[...] for r in y_refs], axis=0)
    o_ref[...] += jnp.dot(jnp.concatenate(onehots, axis=1), y, preferred_element_type=F32)

    @pl.when(s == pl.num_programs(1) - 1)
    def _():
        o_ref[...] = _rms_scale(o_ref[...]) * g_ref[...]


def _combine_windows(base_flat, slot_tok, x2, ye, g_final, *, cap, tb, win, experts_per_step):
    n, d = x2.shape
    ne = ye.shape[0]
    nblk = n // tb
    per = experts_per_step
    assert ne % per == 0

    def y_map(k):
        def index(b, s, base):
            e = s * per + k
            return (e, _window_start(base, e, b, nblk=nblk, cap=cap, win=win), 0)
        return index

    kern = functools.partial(_combine_windows_kernel, cap=cap, nblk=nblk, win=win)
    return pl.pallas_call(
        kern,
        grid_spec=pltpu.PrefetchScalarGridSpec(
            num_scalar_prefetch=1,
            grid=(nblk, ne // per),
            in_specs=[pl.BlockSpec((tb, ne), lambda b, s, base: (b, 0)),
                      pl.BlockSpec((tb, d), lambda b, s, base: (b, 0)),
                      pl.BlockSpec((1, d), lambda b, s, base: (0, 0))]
                     + [pl.BlockSpec((None, pl.Element(win), pl.Element(d)), y_map(k))
                        for k in range(per)],
            out_specs=pl.BlockSpec((tb, d), lambda b, s, base: (b, 0)),
        ),
        out_shape=jax.ShapeDtypeStruct((n, d), F32),
        compiler_params=_params("parallel", "arbitrary"),
        name="expert_combine_windows",
    )(base_flat, slot_tok, x2, g_final.reshape(1, d), *([ye] * per))


def _combine(base, slot_tok, x2, ye, g_final, *, cap, tb, win):
    nblk = x2.shape[0] // tb
    base_flat = base[:, :nblk + 1].reshape(-1)
    first = (base[:, :nblk] // BF16_ROW_TILE) * BF16_ROW_TILE
    fits = jnp.all(base[:, 1:nblk + 1] - jnp.minimum(first, cap - win) <= win)
    return lax.cond(
        fits,
        lambda: _combine_windows(base_flat, slot_tok, x2, ye, g_final, cap=cap, tb=tb, win=win,
                                 experts_per_step=8),
        lambda: _combine_any(base_flat, slot_tok, x2, ye, g_final, cap=cap, tb=tb))


def _encode(x3, mem3, p):
    bsz, seq, d = x3.shape
    n = bsz * seq
    n_mem = mem3.shape[1]
    x = x3.reshape(n, d)
    mem = mem3.reshape(bsz * n_mem, d)

    gmlp_w = p["gmlp_norm_g"].shape[0]
    conv_w = p["conv_w"].shape[1]
    xattn_w = p["w_mem_kv"].shape[1] // 2
    off_cb = 2 * gmlp_w
    off_g = off_cb + 3 * conv_w + xattn_w
    ne = p["w_router"].shape[1]
    cap = CAPACITY_FACTOR * n // ne

    def col_tile(cols):
        return next(t for t in (1024, 512, 256, 128) if cols % t == 0 and t <= cols)

    tm = min(1024, n)
    ts = min(256, seq)
    tb = 256 if cap >= 256 + BF16_ROW_TILE else 128

    h = _rmsnorm_cast(x, p["norm_mix_g"], ts)
    proj = _proj(h, p["w_in"], min(2 * tm, n), col_tile(p["w_in"].shape[1]), off_cb, off_g, BF16)
    m = _rmsnorm_cast(mem, p["mem_norm_g"], n_mem)
    kv = _proj(m, p["w_mem_kv"], n_mem, col_tile(2 * xattn_w), 0, 2 * xattn_w, BF16)
    ycat = _branches(proj, kv, p["w_spatial"], p["b_spatial"], p["gmlp_norm_g"].reshape(1, gmlp_w),
                     p["conv_w"], seq=seq, ts=ts, gmlp_w=gmlp_w, conv_w=conv_w, xattn_w=xattn_w,
                     n_mem=n_mem)
    merged = _merge(ycat, p["w_branch"], proj, tm=tm, tn=min(512, col_tile(d)), off_g=off_g, d=d,
                    gmlp_w=gmlp_w, conv_w=conv_w)
    x2 = _outproj(merged, p["w_out"], x, tm, col_tile(d))

    hp, afft = _router(x2, p["norm_ffn_g"], p["w_router"], ts, min(SC_ROW_PIECES, d // 256))
    slot, base = _select(afft, cap, tb)
    nblk = n // tb
    base_flat = base[:, :nblk + 1].reshape(-1)
    tok, gw = _pick_list(base_flat, slot.reshape(ne, 1, n), afft.reshape(ne, 1, n), cap=cap, tb=tb,
                         chunk=PICK_CHUNK_ROWS, experts_per_step=min(8, ne))
    xp = _gather(tok, hp, window=SC_GATHER_WINDOW)
    ye = _ffn(xp, gw, p["w_gate"], p["w_up"], p["w_down"], min(FFN_UP_COLS, d), min(FFN_DOWN_COLS, d))
    y = _combine(base, slot.T, x2, ye, p["final_norm_g"], cap=cap, tb=tb,
                 win=min(COMBINE_WINDOW_ROWS, tb))
    return y.reshape(bsz, seq, d)


def kernel(x_prompt, x_sample, mem_prompt, mem_sample, norm_mix_g, w_in, gmlp_norm_g, w_spatial,
           b_spatial, conv_w, mem_norm_g, w_mem_kv, w_branch_a, w_branch_b, w_branch_x, w_out,
           norm_ffn_g, w_router, w_gate, w_up, w_down, final_norm_g):
    assert w_in.shape[0] == 1, "one layer"
    groups, chunk = b_spatial.shape[1], b_spatial.shape[2]
    gc = gmlp_norm_g.shape[1] // groups
    p = {
        "norm_mix_g": norm_mix_g[0],
        "w_in": w_in[0].astype(BF16),
        "gmlp_norm_g": gmlp_norm_g[0],
        "w_spatial": w_spatial[0].astype(BF16),
        "b_spatial": jnp.broadcast_to(b_spatial[0][:, :, None], (groups, chunk, gc)),
        "conv_w": conv_w[0],
        "mem_norm_g": mem_norm_g[0],
        "w_mem_kv": w_mem_kv[0].astype(BF16),
        "w_branch": jnp.concatenate([w_branch_a[0], w_branch_b[0], w_branch_x[0]],
                                    axis=0).astype(BF16),
        "w_out": w_out[0].astype(BF16),
        "norm_ffn_g": norm_ffn_g[0],
        "w_router": w_router[0].astype(BF16),
        "w_gate": w_gate.reshape(w_gate.shape[1:]),
        "w_up": w_up.reshape(w_up.shape[1:]),
        "w_down": w_down.reshape(w_down.shape[1:]),
        "final_norm_g": final_norm_g,
    }
    return (_encode(x_prompt, mem_prompt, p), _encode(x_sample, mem_sample, p))
```

```python
import functools

import jax
import jax.numpy as jnp
from jax import lax
from jax.experimental import pallas as pl
from jax.experimental.pallas import tpu as pltpu
from jax.experimental.pallas import tpu_sc as plsc

EPS = 1e-6
XATTN_HEADS = 4
CAPACITY_FACTOR = 2
BF16_ROW_TILE = 16
SUBLANES = 8
PICK_CHUNK_ROWS = 80
SC_ROW_PIECES = 8
SC_GATHER_WINDOW = 128
COMBINE_WINDOW_ROWS = 128
VMEM_LIMIT_BYTES = 56 * 1024 * 1024
FFN_VMEM_LIMIT_BYTES = 60 * 1024 * 1024
FFN_UP_COLS = 256
FFN_DOWN_COLS = 512

F32 = jnp.float32
BF16 = jnp.bfloat16


def _params(*semantics):
    return pltpu.CompilerParams(dimension_semantics=semantics,
                                vmem_limit_bytes=VMEM_LIMIT_BYTES)


GELU_K = 0.7978845608028654
GELU_CUBIC = 0.044715
TANH_SATURATED = 30.0
MXU_COLS = 256


def _sigmoid(x):
    return 1.0 / (1.0 + jnp.exp(-x))


def _rms_scale(x):
    return x * lax.rsqrt(jnp.mean(x * x, axis=-1, keepdims=True) + EPS)


def _rmsnorm_cast_kernel(x_ref, g_ref, o_ref):
    o_ref[...] = (_rms_scale(x_ref[...]) * g_ref[...]).astype(o_ref.dtype)


def _rmsnorm_cast(x, g, tm):
    n, d = x.shape
    return pl.pallas_call(
        _rmsnorm_cast_kernel,
        grid=(n // tm,),
        in_specs=[pl.BlockSpec((tm, d), lambda i: (i, 0)),
                  pl.BlockSpec((1, d), lambda i: (0, 0))],
        out_specs=pl.BlockSpec((tm, d), lambda i: (i, 0)),
        out_shape=jax.ShapeDtypeStruct((n, d), BF16),
        compiler_params=_params("parallel"),
        name="rmsnorm_cast",
    )(x, g.reshape(1, d))


def _proj_kernel(h_ref, w_ref, o_ref, *, sub, gelu_tiles, sigmoid_from):
    j = pl.program_id(1)
    nsub = o_ref.shape[1] // sub
    h = h_ref[...]
    for s in range(nsub):
        c = j * nsub + s
        is_gelu = c < gelu_tiles
        is_sig = c >= sigmoid_from
        q1 = jnp.where(is_gelu, GELU_K, jnp.where(is_sig, 0.5, 0.0))
        q3 = jnp.where(is_gelu, GELU_K * GELU_CUBIC, 0.0)
        q0 = jnp.where(jnp.logical_or(is_gelu, is_sig), 0.0, TANH_SATURATED)
        p1 = jnp.where(is_sig, 0.0, 1.0)
        p0 = jnp.where(is_sig, 1.0, 0.0)
        cols = slice(s * sub, (s + 1) * sub)
        x = jnp.dot(h, w_ref[:, cols], preferred_element_type=F32)
        gate = 0.5 * (1.0 + jnp.tanh(x * (q1 + q3 * (x * x)) + q0))
        o_ref[:, cols] = ((x * p1 + p0) * gate).astype(o_ref.dtype)


def _proj(h, w, tm, tn, gelu_cols, sigmoid_from_col, out_dtype):
    n, d = h.shape
    cols = w.shape[1]
    sub = min(MXU_COLS, tn)
    kern = functools.partial(_proj_kernel, sub=sub, gelu_tiles=gelu_cols // sub,
                             sigmoid_from=sigmoid_from_col // sub)
    return pl.pallas_call(
        kern,
        grid=(n // tm, cols // tn),
        in_specs=[pl.BlockSpec((tm, d), lambda i, j: (i, 0)),
                  pl.BlockSpec((d, tn), lambda i, j: (0, j))],
        out_specs=pl.BlockSpec((tm, tn), lambda i, j: (i, j)),
        out_shape=jax.ShapeDtypeStruct((n, cols), out_dtype),
        compiler_params=_params("parallel", "arbitrary"),
        name="in_proj",
    )(h, w)


def _branch_kernel(u_ref, v_ref, cb_ref, cc_ref, ch_ref, q_ref, ccp_ref, chp_ref, ccn_ref, chn_ref,
                   kv_ref, wsp_ref, bsp_ref, gg_ref, cw_ref, y_ref, *, seq, chunk, groups, heads):
    i = pl.program_id(0)
    ts = u_ref.shape[0]
    gw = u_ref.shape[1]
    gc = gw // groups
    cw = cb_ref.shape[1]
    xw = q_ref.shape[1]
    hd = xw // heads

    vn = (_rms_scale(v_ref[...].astype(F32)) * gg_ref[...]).astype(BF16)
    for c in range(ts // chunk):
        rows = slice(c * chunk, (c + 1) * chunk)
        for g in range(groups):
            cols = slice(g * gc, (g + 1) * gc)
            mixed = jnp.dot(wsp_ref[g], vn[rows, cols], preferred_element_type=F32) + bsp_ref[g]
            y_ref[rows, cols] = (u_ref[rows, cols].astype(F32) * mixed).astype(y_ref.dtype)

    pos = (i * ts) % seq
    z = cc_ref[...].astype(F32) * ch_ref[...].astype(F32)
    sub = ccp_ref.shape[0]
    z_before = jnp.where(pos == 0, 0.0, ccp_ref[sub - 1:sub, :].astype(F32)
                         * chp_ref[sub - 1:sub, :].astype(F32))
    z_after = jnp.where(pos + ts == seq, 0.0, ccn_ref[0:1, :].astype(F32)
                        * chn_ref[0:1, :].astype(F32))
    row = lax.broadcasted_iota(jnp.int32, z.shape, 0)
    z_prev = jnp.where(row == 0, z_before, pltpu.roll(z, 1, 0))
    z_next = jnp.where(row == ts - 1, z_after, pltpu.roll(z, ts - 1, 0))
    conv = z_prev * cw_ref[0:1, :] + z * cw_ref[1:2, :] + z_next * cw_ref[2:3, :]
    y_ref[:, gw:gw + cw] = (cb_ref[...].astype(F32) * conv).astype(y_ref.dtype)

    scale = hd ** -0.5
    for h in range(heads):
        cols = slice(h * hd, (h + 1) * hd)
        qh = q_ref[:, cols]
        kh = kv_ref[:, cols]
        vh = kv_ref[:, xw + h * hd: xw + (h + 1) * hd]
        s = lax.dot_general(qh, kh, (((1,), (1,)), ((), ())), preferred_element_type=F32) * scale
        e = jnp.exp(s - jnp.max(s, axis=-1, keepdims=True))
        p = (e / jnp.sum(e, axis=-1, keepdims=True)).astype(BF16)
        yh = jnp.dot(p, vh, preferred_element_type=F32)
        y_ref[:, gw + cw + h * hd: gw + cw + (h + 1) * hd] = yh.astype(y_ref.dtype)


def _branches(proj, kv, wsp, bsp, gg, cw, *, seq, ts, gmlp_w, conv_w, xattn_w, n_mem):
    n = proj.shape[0]
    chunk = wsp.shape[1]
    groups = wsp.shape[0]
    sub = BF16_ROW_TILE
    off_cb = 2 * gmlp_w
    cb_blk = off_cb // conv_w
    q_blk = (off_cb + 3 * conv_w) // xattn_w
    last = n // sub - 1
    tsb = ts // sub

    def prev_map(col):
        return lambda i: (jnp.maximum(i * tsb - 1, 0), col)

    def next_map(col):
        return lambda i: (jnp.minimum((i + 1) * tsb, last), col)

    kern = functools.partial(_branch_kernel, seq=seq, chunk=chunk, groups=groups, heads=XATTN_HEADS)
    return pl.pallas_call(
        kern,
        grid=(n // ts,),
        in_specs=[
            pl.BlockSpec((ts, gmlp_w), lambda i: (i, 0)),
            pl.BlockSpec((ts, gmlp_w), lambda i: (i, 1)),
            pl.BlockSpec((ts, conv_w), lambda i: (i, cb_blk)),
            pl.BlockSpec((ts, conv_w), lambda i: (i, cb_blk + 1)),
            pl.BlockSpec((ts, conv_w), lambda i: (i, cb_blk + 2)),
            pl.BlockSpec((ts, xattn_w), lambda i: (i, q_blk)),
            pl.BlockSpec((sub, conv_w), prev_map(cb_blk + 1)),
            pl.BlockSpec((sub, conv_w), prev_map(cb_blk + 2)),
            pl.BlockSpec((sub, conv_w), next_map(cb_blk + 1)),
            pl.BlockSpec((sub, conv_w), next_map(cb_blk + 2)),
            pl.BlockSpec((n_mem, 2 * xattn_w), lambda i: ((i * ts) // seq, 0)),
            pl.BlockSpec(wsp.shape, lambda i: (0, 0, 0)),
            pl.BlockSpec(bsp.shape, lambda i: (0, 0, 0)),
            pl.BlockSpec((1, gmlp_w), lambda i: (0, 0)),
            pl.BlockSpec(cw.shape, lambda i: (0, 0)),
        ],
        out_specs=pl.BlockSpec((ts, gmlp_w + conv_w + xattn_w), lambda i: (i, 0)),
        out_shape=jax.ShapeDtypeStruct((n, gmlp_w + conv_w + xattn_w), BF16),
        compiler_params=_params("parallel"),
        name="branches",
    )(proj, proj, proj, proj, proj, proj, proj, proj, proj, proj, kv, wsp, bsp, gg, cw)


def _merge_kernel(y_ref, w_ref, ga_ref, gb_ref, gx_ref, o_ref, *, gmlp_w, conv_w, sub):
    ya = y_ref[:, :gmlp_w]
    yb = y_ref[:, gmlp_w:gmlp_w + conv_w]
    yx = y_ref[:, gmlp_w + conv_w:]
    for s in range(o_ref.shape[1] // sub):
        cols = slice(s * sub, (s + 1) * sub)
        a = jnp.dot(ya, w_ref[:gmlp_w, cols], preferred_element_type=F32)
        b = jnp.dot(yb, w_ref[gmlp_w:gmlp_w + conv_w, cols], preferred_element_type=F32)
        x = jnp.dot(yx, w_ref[gmlp_w + conv_w:, cols], preferred_element_type=F32)
        o_ref[:, cols] = (ga_ref[:, cols].astype(F32) * a + gb_ref[:, cols].astype(F32) * b
                          + gx_ref[:, cols].astype(F32) * x).astype(o_ref.dtype)


def _merge(ycat, w_br, proj, *, tm, tn, off_g, d, gmlp_w, conv_w):
    n, k = ycat.shape
    g0 = off_g // tn
    gstep = d // tn
    kern = functools.partial(_merge_kernel, gmlp_w=gmlp_w, conv_w=conv_w, sub=min(MXU_COLS, tn))
    return pl.pallas_call(
        kern,
        grid=(n // tm, d // tn),
        in_specs=[pl.BlockSpec((tm, k), lambda i, j: (i, 0)),
                  pl.BlockSpec((k, tn), lambda i, j: (0, j)),
                  pl.BlockSpec((tm, tn), lambda i, j: (i, g0 + j)),
                  pl.BlockSpec((tm, tn), lambda i, j: (i, g0 + gstep + j)),
                  pl.BlockSpec((tm, tn), lambda i, j: (i, g0 + 2 * gstep + j))],
        out_specs=pl.BlockSpec((tm, tn), lambda i, j: (i, j)),
        out_shape=jax.ShapeDtypeStruct((n, d), BF16),
        compiler_params=_params("parallel", "arbitrary"),
        name="merge",
    )(ycat, w_br, proj, proj, proj)


def _outproj_kernel(m_ref, w_ref, x_ref, o_ref, *, sub):
    m = m_ref[...]
    for s in range(o_ref.shape[1] // sub):
        cols = slice(s * sub, (s + 1) * sub)
        o_ref[:, cols] = x_ref[:, cols] + jnp.dot(m, w_ref[:, cols], preferred_element_type=F32)


def _outproj(merged, w_out, x, tm, tn):
    n, d = x.shape
    return pl.pallas_call(
        functools.partial(_outproj_kernel, sub=min(MXU_COLS, tn)),
        grid=(n // tm, d // tn),
        in_specs=[pl.BlockSpec((tm, d), lambda i, j: (i, 0)),
                  pl.BlockSpec((d, tn), lambda i, j: (0, j)),
                  pl.BlockSpec((tm, tn), lambda i, j: (i, j))],
        out_specs=pl.BlockSpec((tm, tn), lambda i, j: (i, j)),
        out_shape=jax.ShapeDtypeStruct((n, d), F32),
        compiler_params=_params("parallel", "arbitrary"),
        name="out_proj",
    )(merged, w_out, x)


def _pack_bf16_pair(lo, hi):
    lo_bits = pltpu.bitcast(lo.astype(BF16).astype(F32), jnp.uint32)
    hi_bits = pltpu.bitcast(hi.astype(BF16).astype(F32), jnp.uint32)
    return lax.shift_right_logical(lo_bits, jnp.uint32(16)) | (hi_bits & jnp.uint32(0xFFFF0000))


def _unpack_bf16_pair(words):
    lo = pltpu.bitcast(lax.shift_left(words, jnp.uint32(16)), F32)
    hi = pltpu.bitcast(words & jnp.uint32(0xFFFF0000), F32)
    return lo.astype(BF16), hi.astype(BF16)


def _router_kernel(x_ref, g_ref, wrt_ref, hp_ref, afft_ref):
    hf = _rms_scale(x_ref[...]) * g_ref[...]
    half = hf.shape[1] // 2
    words = _pack_bf16_pair(hf[:, :half], hf[:, half:])
    pw = hp_ref.shape[2]
    for c in range(hp_ref.shape[0]):
        hp_ref[c] = words[:, c * pw:(c + 1) * pw]
    lt = lax.dot_general(wrt_ref[...], hf.astype(BF16), (((1,), (1,)), ((), ())),
                         preferred_element_type=F32)
    et = jnp.exp(lt - jnp.max(lt, axis=0, keepdims=True))
    afft_ref[...] = et / jnp.sum(et, axis=0, keepdims=True)


def _router(x2, g, wr, tm, pieces):
    n, d = x2.shape
    ne = wr.shape[1]
    pw = d // 2 // pieces
    return pl.pallas_call(
        _router_kernel,
        grid=(n // tm,),
        in_specs=[pl.BlockSpec((tm, d), lambda i: (i, 0)),
                  pl.BlockSpec((1, d), lambda i: (0, 0)),
                  pl.BlockSpec((ne, d), lambda i: (0, 0))],
        out_specs=[pl.BlockSpec((pieces, tm, pw), lambda i: (0, i, 0)),
                   pl.BlockSpec((ne, tm), lambda i: (0, i))],
        out_shape=[jax.ShapeDtypeStruct((pieces, n, pw), jnp.uint32),
                   jax.ShapeDtypeStruct((ne, n), F32)],
        compiler_params=_params("parallel"),
        name="ffn_norm_router",
    )(x2, g.reshape(1, d), wr.T)


def _select_kernel(aff_ref, slot_ref, base_ref, *, cap, tb):
    ne, n = aff_ref.shape
    bits = pltpu.bitcast(aff_ref[...], jnp.int32)
    capf = jnp.float32(cap)

    def search(k, t):
        cand = t | lax.shift_left(jnp.int32(1), 30 - k)
        cnt = jnp.sum(jnp.where(bits >= cand, 1.0, 0.0), axis=1, keepdims=True)
        return jnp.where(cnt >= capf, cand, t)

    thr = lax.fori_loop(0, 31, search, jnp.zeros((ne, 1), jnp.int32))
    n_gt = jnp.sum(jnp.where(bits > thr, 1.0, 0.0), axis=1, keepdims=True)
    need = capf - n_gt

    r = lax.broadcasted_iota(jnp.int32, (tb, tb), 0)
    c = lax.broadcasted_iota(jnp.int32, (tb, tb), 1)
    tri = jnp.where(r < c, 1.0, 0.0).astype(BF16)
    lane = lax.broadcasted_iota(jnp.int32, base_ref.shape, 1)
    eq_seen = jnp.zeros((ne, 1), F32)
    sel_seen = jnp.zeros((ne, 1), F32)
    base = jnp.zeros(base_ref.shape, F32)
    for b in range(n // tb):
        blk = pltpu.bitcast(aff_ref[:, b * tb:(b + 1) * tb], jnp.int32)
        eq = jnp.where(blk == thr, 1.0, 0.0)
        eq_rank = jnp.dot(eq.astype(BF16), tri, preferred_element_type=F32) + eq_seen
        sel = jnp.where(blk > thr, 1.0, jnp.where(eq_rank < need, eq, 0.0))
        pos = jnp.dot(sel.astype(BF16), tri, preferred_element_type=F32) + sel_seen
        slot_ref[:, b * tb:(b + 1) * tb] = jnp.where(sel > 0.0, pos, -1.0).astype(jnp.int32)
        base = jnp.where(lane == b, sel_seen, base)
        eq_seen = eq_seen + jnp.sum(eq, axis=1, keepdims=True)
        sel_seen = sel_seen + jnp.sum(sel, axis=1, keepdims=True)
    base = jnp.where(lane == n // tb, sel_seen, base)
    base_ref[...] = base.astype(jnp.int32)


def _select(afft, cap, tb):
    ne, n = afft.shape
    lanes = 128
    assert n // tb < lanes
    kern = functools.partial(_select_kernel, cap=cap, tb=tb)
    return pl.pallas_call(
        kern,
        out_shape=[jax.ShapeDtypeStruct((ne, n), jnp.int32),
                   jax.ShapeDtypeStruct((ne, lanes), jnp.int32)],
        compiler_params=pltpu.CompilerParams(vmem_limit_bytes=VMEM_LIMIT_BYTES),
        name="expert_choice_select",
    )(afft)


def _pick_list_kernel(base_ref, slot_ref, aff_ref, tok_ref, gw_ref, *, cap, nblk, chunk):
    eb = pl.program_id(0)
    b = pl.program_id(1)

    @pl.when(b == 0)
    def _():
        tok_ref[...] = jnp.zeros_like(tok_ref)
        gw_ref[...] = jnp.zeros_like(gw_ref)

    tb = slot_ref.shape[2]
    token = (lax.broadcasted_iota(jnp.int32, (1, tb), 1) + b * tb).astype(F32)
    for k in range(tok_ref.shape[0]):
        e = eb * tok_ref.shape[0] + k
        first = (base_ref[e * (nblk + 1) + b] // SUBLANES) * SUBLANES
        end = base_ref[e * (nblk + 1) + b + 1]
        slot = slot_ref[k]
        aff = aff_ref[k]

        def body(c, carry, k=k, first=first, slot=slot, aff=aff):
            lo = first + c * chunk
            st = pl.multiple_of(jnp.minimum(lo, cap - chunk), SUBLANES)
            rows = lax.broadcasted_iota(jnp.int32, (chunk, tb), 0) + st
            hit = jnp.where(rows == slot, jnp.where(rows >= lo, 1.0, 0.0), 0.0)
            tok_ref[k, pl.ds(st, chunk), :] += jnp.sum(hit * token, axis=1,
                                                       keepdims=True).astype(jnp.int32)
            gw_ref[k, pl.ds(st, chunk), :] += jnp.sum(hit * aff, axis=1, keepdims=True)
            return carry

        lax.fori_loop(0, (end - first + chunk - 1) // chunk, body, 0)


def _pick_list(base_flat, slot3, aff3, *, cap, tb, chunk, experts_per_step):
    ne, _, n = slot3.shape
    nblk = n // tb
    eps = experts_per_step
    assert cap >= chunk and (cap - chunk) % SUBLANES == 0 and chunk % SUBLANES == 0
    kern = functools.partial(_pick_list_kernel, cap=cap, nblk=nblk, chunk=chunk)
    return pl.pallas_call(
        kern,
        grid_spec=pltpu.PrefetchScalarGridSpec(
            num_scalar_prefetch=1,
            grid=(ne // eps, nblk),
            in_specs=[pl.BlockSpec((eps, 1, tb), lambda e, b, base: (e, 0, b)),
                      pl.BlockSpec((eps, 1, tb), lambda e, b, base: (e, 0, b))],
            out_specs=[pl.BlockSpec((eps, cap, 1), lambda e, b, base: (e, 0, 0)),
                       pl.BlockSpec((eps, cap, 1), lambda e, b, base: (e, 0, 0))],
        ),
        out_shape=[jax.ShapeDtypeStruct((ne, cap, 1), jnp.int32),
                   jax.ShapeDtypeStruct((ne, cap, 1), F32)],
        compiler_params=_params("parallel", "arbitrary"),
        name="expert_pick_list",
    )(base_flat, slot3, aff3)


def _row_gather_sparsecore(table, rows, *, window):
    num, width = rows.shape[0], table.shape[1]
    mesh = plsc.VectorSubcoreMesh(core_axis_name="c", subcore_axis_name="s")

    @pl.kernel(out_type=jax.ShapeDtypeStruct((num, width), table.dtype), mesh=mesh,
               name="expert_row_gather")
    def gather(table_hbm, rows_hbm, out_hbm):
        def body(rows_vmem, out_vmem):
            pltpu.sync_copy(table_hbm.at[rows_vmem.at[0]], out_vmem)

        pltpu.emit_pipeline(
            body,
            grid=(num // window,),
            in_specs=[pl.BlockSpec((1, window), index_map=lambda i: (0, i))],
            out_specs=[pl.BlockSpec((window, width), index_map=lambda i: (i, 0))],
            core_axis_name=("c", "s"),
            dimension_semantics=(pltpu.PARALLEL,),
        )(rows_hbm, out_hbm)

    return gather(table, rows.reshape(1, num))


def _gather(tok, hp, *, window):
    ne, cap, _ = tok.shape
    pieces, n, pw = hp.shape
    rows = (jnp.arange(pieces, dtype=jnp.int32)[:, None] * n + tok.reshape(1, ne * cap)).reshape(-1)
    out = _row_gather_sparsecore(hp.reshape(pieces * n, pw), rows, window=window)
    return out.reshape(pieces, ne, cap, pw)


def _ffn_up_kernel(*refs, groups, sub):
    xp_refs = refs[:groups]
    wg_ref, wu_ref = refs[groups:groups + 2]
    hid_refs = refs[groups + 2:2 * groups + 2]
    xe_refs = refs[2 * groups + 2:]

    @pl.when(pl.program_id(1) == 0)
    def _():
        for xp_ref, xe_ref in zip(xp_refs, xe_refs):
            pieces, _, pw = xp_ref.shape
            for c in range(pieces):
                lo, hi = _unpack_bf16_pair(xp_ref[c])
                xe_ref[:, c * pw:(c + 1) * pw] = lo
                xe_ref[:, (pieces + c) * pw:(pieces + c + 1) * pw] = hi

    for s in range(hid_refs[0].shape[1] // sub):
        cols = slice(s * sub, (s + 1) * sub)
        wg = wg_ref[:, cols].astype(BF16)
        wu = wu_ref[:, cols].astype(BF16)
        for xe_ref, hid_ref in zip(xe_refs, hid_refs):
            x = xe_ref[...]
            g = jnp.dot(x, wg, preferred_element_type=F32)
            u = jnp.dot(x, wu, preferred_element_type=F32)
            hid_ref[:, cols] = ((g * _sigmoid(g)) * u).astype(hid_ref.dtype)


def _ffn_down_kernel(*refs, groups, sub):
    hid_refs = refs[:groups]
    gw_refs = refs[groups:2 * groups]
    wd_ref = refs[2 * groups]
    ye_refs = refs[2 * groups + 1:]
    for s in range(ye_refs[0].shape[1] // sub):
        cols = slice(s * sub, (s + 1) * sub)
        wd = wd_ref[:, cols].astype(BF16)
        for hid_ref, gw_ref, ye_ref in zip(hid_refs, gw_refs, ye_refs):
            y = jnp.dot(hid_ref[...], wd, preferred_element_type=F32)
            ye_ref[:, cols] = (y * gw_ref[...]).astype(ye_ref.dtype)


def _ffn(xps, gws, wg, wu, wd, tf, tn):
    groups = len(xps)
    pieces, ne, _, pw = xps[0].shape
    caps = [xp.shape[2] for xp in xps]
    d, f = wg.shape[1], wg.shape[2]
    assert d == 2 * pieces * pw
    params = pltpu.CompilerParams(dimension_semantics=("parallel", "arbitrary"),
                                  vmem_limit_bytes=FFN_VMEM_LIMIT_BYTES)
    hids = pl.pallas_call(
        functools.partial(_ffn_up_kernel, groups=groups, sub=min(MXU_COLS, tf)),
        grid=(ne, f // tf),
        in_specs=[pl.BlockSpec((pieces, None, c, pw), lambda e, j: (0, e, 0, 0),
                               pipeline_mode=pl.Buffered(1)) for c in caps]
                 + [pl.BlockSpec((None, d, tf), lambda e, j: (e, 0, j)),
                    pl.BlockSpec((None, d, tf), lambda e, j: (e, 0, j))],
        out_specs=[pl.BlockSpec((None, c, tf), lambda e, j: (e, 0, j)) for c in caps],
        out_shape=[jax.ShapeDtypeStruct((ne, c, f), BF16) for c in caps],
        scratch_shapes=[pltpu.VMEM((c, d), BF16) for c in caps],
        compiler_params=params,
        name="expert_ffn_up",
    )(*xps, wg, wu)
    return pl.pallas_call(
        functools.partial(_ffn_down_kernel, groups=groups, sub=min(MXU_COLS, tn)),
        grid=(ne, d // tn),
        in_specs=[pl.BlockSpec((None, c, f), lambda e, j: (e, 0, 0)) for c in caps]
                 + [pl.BlockSpec((None, c, 1), lambda e, j: (e, 0, 0)) for c in caps]
                 + [pl.BlockSpec((None, f, tn), lambda e, j: (e, 0, j))],
        out_specs=[pl.BlockSpec((None, c, tn), lambda e, j: (e, 0, j)) for c in caps],
        out_shape=[jax.ShapeDtypeStruct((ne, c, d), BF16) for c in caps],
        compiler_params=params,
        name="expert_ffn_down",
    )(*hids, *gws, wd)


def _expert_column(ref, e):
    lane = lax.broadcasted_iota(jnp.int32, ref.shape, 1)
    return jnp.sum(jnp.where(lane == e, ref[...], 0), axis=1, keepdims=True)


def _combine_any_kernel(base_ref, slot_ref, x_ref, ya_ref, yb_ref, g_ref, o_ref, *, cap, nblk):
    b = pl.program_id(0)
    e = pl.program_id(1)
    ne = pl.num_programs(1)

    @pl.when(e == 0)
    def _():
        o_ref[...] = x_ref[...]

    tb = x_ref.shape[0]
    blk = ya_ref.shape[0]
    start = jnp.minimum(base_ref[e * (nblk + 1) + b] // blk, cap // blk - 1) * blk
    slot = _expert_column(slot_ref, e)
    cols = lax.broadcasted_iota(jnp.int32, (tb, blk), 1) + start
    oh_a = jnp.where(cols == slot, 1.0, 0.0).astype(BF16)
    oh_b = jnp.where(cols + blk == slot, 1.0, 0.0).astype(BF16)
    o_ref[...] += (jnp.dot(oh_a, ya_ref[...], preferred_element_type=F32)
                   + jnp.dot(oh_b, yb_ref[...], preferred_element_type=F32))

    @pl.when(e == ne - 1)
    def _():
        o_ref[...] = _rms_scale(o_ref[...]) * g_ref[...]


def _combine_any(base_flat, slot_tok, x2, ye, g_final, *, cap, tb):
    n, d = x2.shape
    ne = ye.shape[0]
    nblk = n // tb
    blk = tb
    last = cap // blk - 1

    def ya_map(b, e, base):
        return (e, jnp.minimum(base[e * (nblk + 1) + b] // blk, last), 0)

    def yb_map(b, e, base):
        return (e, jnp.minimum(jnp.minimum(base[e * (nblk + 1) + b] // blk, last) + 1, last), 0)

    kern = functools.partial(_combine_any_kernel, cap=cap, nblk=nblk)
    return pl.pallas_call(
        kern,
        grid_spec=pltpu.PrefetchScalarGridSpec(
            num_scalar_prefetch=1,
            grid=(nblk, ne),
            in_specs=[pl.BlockSpec((tb, ne), lambda b, e, base: (b, 0)),
                      pl.BlockSpec((tb, d), lambda b, e, base: (b, 0)),
                      pl.BlockSpec((None, blk, d), ya_map),
                      pl.BlockSpec((None, blk, d), yb_map),
                      pl.BlockSpec((1, d), lambda b, e, base: (0, 0))],
            out_specs=pl.BlockSpec((tb, d), lambda b, e, base: (b, 0)),
        ),
        out_shape=jax.ShapeDtypeStruct((n, d), F32),
        compiler_params=_params("parallel", "arbitrary"),
        name="expert_combine_any",
    )(base_flat, slot_tok, x2, ye, ye, g_final.reshape(1, d))


def _window_start(base, e, b, *, nblk, cap, win):
    tile = jnp.minimum(base[e * (nblk + 1) + b] // BF16_ROW_TILE, (cap - win) // BF16_ROW_TILE)
    return tile * BF16_ROW_TILE


def _combine_windows_kernel(base_ref, slot_ref, x_ref, g_ref, *rest, cap, nblk, win):
    y_refs, o_ref = rest[:-1], rest[-1]
    b = pl.program_id(0)
    s = pl.program_id(1)
    per = len(y_refs)

    @pl.when(s == 0)
    def _():
        o_ref[...] = x_ref[...]

    tb = x_ref.shape[0]
    col = lax.broadcasted_iota(jnp.int32, (tb, win), 1)
    onehots = []
    for k in range(per):
        e = s * per + k
        start = _window_start(base_ref, e, b, nblk=nblk, cap=cap, win=win)
        onehots.append(jnp.where(col + start == _expert_column(slot_ref, e), 1.0, 0.0).astype(BF16))
    y = jnp.concatenate([r[...] for r in y_refs], axis=0)
    o_ref[...] += jnp.dot(jnp.concatenate(onehots, axis=1), y, preferred_element_type=F32)

    @pl.when(s == pl.num_programs(1) - 1)
    def _():
        o_ref[...] = _rms_scale(o_ref[...]) * g_ref[...]


def _combine_windows(base_flat, slot_tok, x2, ye, g_final, *, cap, tb, win, experts_per_step):
    n, d = x2.shape
    ne = ye.shape[0]
    nblk = n // tb
    per = experts_per_step
    assert ne % per == 0

    def y_map(k):
        def index(b, s, base):
            e = s * per + k
            return (e, _window_start(base, e, b, nblk=nblk, cap=cap, win=win), 0)
        return index

    kern = functools.partial(_combine_windows_kernel, cap=cap, nblk=nblk, win=win)
    return pl.pallas_call(
        kern,
        grid_spec=pltpu.PrefetchScalarGridSpec(
            num_scalar_prefetch=1,
            grid=(nblk, ne // per),
            in_specs=[pl.BlockSpec((tb, ne), lambda b, s, base: (b, 0)),
                      pl.BlockSpec((tb, d), lambda b, s, base: (b, 0)),
                      pl.BlockSpec((1, d), lambda b, s, base: (0, 0))]
                     + [pl.BlockSpec((None, pl.Element(win), pl.Element(d)), y_map(k))
                        for k in range(per)],
            out_specs=pl.BlockSpec((tb, d), lambda b, s, base: (b, 0)),
        ),
        out_shape=jax.ShapeDtypeStruct((n, d), F32),
        compiler_params=_params("parallel", "arbitrary"),
        name="expert_combine_windows",
    )(base_flat, slot_tok, x2, g_final.reshape(1, d), *([ye] * per))


def _combine(base, slot_tok, x2, ye, g_final, *, cap, tb, win):
    nblk = x2.shape[0] // tb
    base_flat = base[:, :nblk + 1].reshape(-1)
    first = (base[:, :nblk] // BF16_ROW_TILE) * BF16_ROW_TILE
    fits = jnp.all(base[:, 1:nblk + 1] - jnp.minimum(first, cap - win) <= win)
    return lax.cond(
        fits,
        lambda: _combine_windows(base_flat, slot_tok, x2, ye, g_final, cap=cap, tb=tb, win=win,
                                 experts_per_step=8),
        lambda: _combine_any(base_flat, slot_tok, x2, ye, g_final, cap=cap, tb=tb))


def _mix_and_route(x3, mem3, p):
    bsz, seq, d = x3.shape
    n = bsz * seq
    n_mem = mem3.shape[1]
    x = x3.reshape(n, d)
    mem = mem3.reshape(bsz * n_mem, d)

    gmlp_w = p["gmlp_norm_g"].shape[0]
    conv_w = p["conv_w"].shape[1]
    xattn_w = p["w_mem_kv"].shape[1] // 2
    off_cb = 2 * gmlp_w
    off_g = off_cb + 3 * conv_w + xattn_w
    ne = p["w_router"].shape[1]
    cap = CAPACITY_FACTOR * n // ne

    def col_tile(cols):
        return next(t for t in (1024, 512, 256, 128) if cols % t == 0 and t <= cols)

    tm = min(1024, n)
    ts = min(256, seq)
    tb = 256 if cap >= 256 + BF16_ROW_TILE else 128

    h = _rmsnorm_cast(x, p["norm_mix_g"], ts)
    proj = _proj(h, p["w_in"], tm, col_tile(p["w_in"].shape[1]), off_cb, off_g, BF16)
    m = _rmsnorm_cast(mem, p["mem_norm_g"], n_mem)
    kv = _proj(m, p["w_mem_kv"], n_mem, col_tile(2 * xattn_w), 0, 2 * xattn_w, BF16)
    ycat = _branches(proj, kv, p["w_spatial"], p["b_spatial"], p["gmlp_norm_g"].reshape(1, gmlp_w),
                     p["conv_w"], seq=seq, ts=ts, gmlp_w=gmlp_w, conv_w=conv_w, xattn_w=xattn_w,
                     n_mem=n_mem)
    merged = _merge(ycat, p["w_branch"], proj, tm=tm, tn=min(512, col_tile(d)), off_g=off_g, d=d,
                    gmlp_w=gmlp_w, conv_w=conv_w)
    x2 = _outproj(merged, p["w_out"], x, tm, col_tile(d))

    hp, afft = _router(x2, p["norm_ffn_g"], p["w_router"], ts, min(SC_ROW_PIECES, d // 256))
    slot, base = _select(afft, cap, tb)
    nblk = n // tb
    base_flat = base[:, :nblk + 1].reshape(-1)
    tok, gw = _pick_list(base_flat, slot.reshape(ne, 1, n), afft.reshape(ne, 1, n), cap=cap, tb=tb,
                         chunk=PICK_CHUNK_ROWS, experts_per_step=min(8, ne))
    xp = _gather(tok, hp, window=SC_GATHER_WINDOW)
    return {"xp": xp, "gw": gw, "x2": x2, "base": base, "slot_tok": slot.T, "cap": cap, "tb": tb}


def _combine_group(r, ye, g_final, shape):
    y = _combine(r["base"], r["slot_tok"], r["x2"], ye, g_final, cap=r["cap"], tb=r["tb"],
                 win=min(COMBINE_WINDOW_ROWS, r["tb"]))
    return y.reshape(shape)


def kernel(x_prompt, x_sample, mem_prompt, mem_sample, norm_mix_g, w_in, gmlp_norm_g, w_spatial,
           b_spatial, conv_w, mem_norm_g, w_mem_kv, w_branch_a, w_branch_b, w_branch_x, w_out,
           norm_ffn_g, w_router, w_gate, w_up, w_down, final_norm_g):
    assert w_in.shape[0] == 1, "one layer"
    groups, chunk = b_spatial.shape[1], b_spatial.shape[2]
    gc = gmlp_norm_g.shape[1] // groups
    p = {
        "norm_mix_g": norm_mix_g[0],
        "w_in": w_in[0].astype(BF16),
        "gmlp_norm_g": gmlp_norm_g[0],
        "w_spatial": w_spatial[0].astype(BF16),
        "b_spatial": jnp.broadcast_to(b_spatial[0][:, :, None], (groups, chunk, gc)),
        "conv_w": conv_w[0],
        "mem_norm_g": mem_norm_g[0],
        "w_mem_kv": w_mem_kv[0].astype(BF16),
        "w_branch": jnp.concatenate([w_branch_a[0], w_branch_b[0], w_branch_x[0]],
                                    axis=0).astype(BF16),
        "w_out": w_out[0].astype(BF16),
        "norm_ffn_g": norm_ffn_g[0],
        "w_router": w_router[0].astype(BF16),
        "w_gate": w_gate.reshape(w_gate.shape[1:]),
        "w_up": w_up.reshape(w_up.shape[1:]),
        "w_down": w_down.reshape(w_down.shape[1:]),
        "final_norm_g": final_norm_g,
    }
    routed = [_mix_and_route(x_prompt, mem_prompt, p), _mix_and_route(x_sample, mem_sample, p)]
    d = x_prompt.shape[-1]
    yes = _ffn([r["xp"] for r in routed], [r["gw"] for r in routed], p["w_gate"], p["w_up"],
               p["w_down"], min(FFN_UP_COLS, d), min(FFN_DOWN_COLS, d))
    return tuple(_combine_group(r, ye, final_norm_g, x.shape)
                 for r, ye, x in zip(routed, yes, (x_prompt, x_sample)))
```

```python
import functools

import jax
import jax.numpy as jnp
from jax import lax
from jax.experimental import pallas as pl
from jax.experimental.pallas import tpu as pltpu
from jax.experimental.pallas import tpu_sc as plsc

EPS = 1e-6
XATTN_HEADS = 4
CAPACITY_FACTOR = 2
BF16_ROW_TILE = 16
SUBLANES = 8
PICK_CHUNK_ROWS = 80
SC_ROW_PIECES = 8
SC_GATHER_WINDOW = 128
COMBINE_WINDOW_ROWS = 128
VMEM_LIMIT_BYTES = 56 * 1024 * 1024
FFN_VMEM_LIMIT_BYTES = 60 * 1024 * 1024
FFN_UP_COLS = 256
FFN_DOWN_COLS = 512

F32 = jnp.float32
BF16 = jnp.bfloat16


def _params(*semantics):
    return pltpu.CompilerParams(dimension_semantics=semantics,
                                vmem_limit_bytes=VMEM_LIMIT_BYTES)


GELU_K = 0.7978845608028654
GELU_CUBIC = 0.044715
TANH_SATURATED = 30.0
MXU_COLS = 256


def _sigmoid(x):
    return 1.0 / (1.0 + jnp.exp(-x))


def _rms_scale(x):
    return x * lax.rsqrt(jnp.mean(x * x, axis=-1, keepdims=True) + EPS)


def _rmsnorm_cast_kernel(x_ref, g_ref, o_ref):
    o_ref[...] = (_rms_scale(x_ref[...]) * g_ref[...]).astype(o_ref.dtype)


def _rmsnorm_cast(x, g, tm):
    n, d = x.shape
    return pl.pallas_call(
        _rmsnorm_cast_kernel,
        grid=(n // tm,),
        in_specs=[pl.BlockSpec((tm, d), lambda i: (i, 0)),
                  pl.BlockSpec((1, d), lambda i: (0, 0))],
        out_specs=pl.BlockSpec((tm, d), lambda i: (i, 0)),
        out_shape=jax.ShapeDtypeStruct((n, d), BF16),
        compiler_params=_params("parallel"),
        name="rmsnorm_cast",
    )(x, g.reshape(1, d))


def _proj_kernel(h_ref, w_ref, o_ref, *, sub, gelu_tiles, sigmoid_from):
    j = pl.program_id(1)
    nsub = o_ref.shape[1] // sub
    h = h_ref[...]
    for s in range(nsub):
        c = j * nsub + s
        is_gelu = c < gelu_tiles
        is_sig = c >= sigmoid_from
        q1 = jnp.where(is_gelu, GELU_K, jnp.where(is_sig, 0.5, 0.0))
        q3 = jnp.where(is_gelu, GELU_K * GELU_CUBIC, 0.0)
        q0 = jnp.where(jnp.logical_or(is_gelu, is_sig), 0.0, TANH_SATURATED)
        p1 = jnp.where(is_sig, 0.0, 1.0)
        p0 = jnp.where(is_sig, 1.0, 0.0)
        cols = slice(s * sub, (s + 1) * sub)
        x = jnp.dot(h, w_ref[:, cols], preferred_element_type=F32)
        gate = 0.5 * (1.0 + jnp.tanh(x * (q1 + q3 * (x * x)) + q0))
        o_ref[:, cols] = ((x * p1 + p0) * gate).astype(o_ref.dtype)


def _proj(h, w, tm, tn, gelu_cols, sigmoid_from_col, out_dtype):
    n, d = h.shape
    cols = w.shape[1]
    sub = min(MXU_COLS, tn)
    kern = functools.partial(_proj_kernel, sub=sub, gelu_tiles=gelu_cols // sub,
                             sigmoid_from=sigmoid_from_col // sub)
    return pl.pallas_call(
        kern,
        grid=(n // tm, cols // tn),
        in_specs=[pl.BlockSpec((tm, d), lambda i, j: (i, 0)),
                  pl.BlockSpec((d, tn), lambda i, j: (0, j))],
        out_specs=pl.BlockSpec((tm, tn), lambda i, j: (i, j)),
        out_shape=jax.ShapeDtypeStruct((n, cols), out_dtype),
        compiler_params=_params("parallel", "arbitrary"),
        name="in_proj",
    )(h, w)


def _branch_kernel(u_ref, v_ref, cb_ref, cc_ref, ch_ref, q_ref, ccp_ref, chp_ref, ccn_ref, chn_ref,
                   kv_ref, wsp_ref, bsp_ref, gg_ref, cw_ref, y_ref, *, seq, chunk, groups, heads):
    i = pl.program_id(0)
    ts = u_ref.shape[0]
    gw = u_ref.shape[1]
    gc = gw // groups
    cw = cb_ref.shape[1]
    xw = q_ref.shape[1]
    hd = xw // heads

    vn = (_rms_scale(v_ref[...].astype(F32)) * gg_ref[...]).astype(BF16)
    for c in range(ts // chunk):
        rows = slice(c * chunk, (c + 1) * chunk)
        for g in range(groups):
            cols = slice(g * gc, (g + 1) * gc)
            mixed = jnp.dot(wsp_ref[g], vn[rows, cols], preferred_element_type=F32) + bsp_ref[g]
            y_ref[rows, cols] = (u_ref[rows, cols].astype(F32) * mixed).astype(y_ref.dtype)

    pos = (i * ts) % seq
    z = cc_ref[...].astype(F32) * ch_ref[...].astype(F32)
    sub = ccp_ref.shape[0]
    z_before = jnp.where(pos == 0, 0.0, ccp_ref[sub - 1:sub, :].astype(F32)
                         * chp_ref[sub - 1:sub, :].astype(F32))
    z_after = jnp.where(pos + ts == seq, 0.0, ccn_ref[0:1, :].astype(F32)
                        * chn_ref[0:1, :].astype(F32))
    row = lax.broadcasted_iota(jnp.int32, z.shape, 0)
    z_prev = jnp.where(row == 0, z_before, pltpu.roll(z, 1, 0))
    z_next = jnp.where(row == ts - 1, z_after, pltpu.roll(z, ts - 1, 0))
    conv = z_prev * cw_ref[0:1, :] + z * cw_ref[1:2, :] + z_next * cw_ref[2:3, :]
    y_ref[:, gw:gw + cw] = (cb_ref[...].astype(F32) * conv).astype(y_ref.dtype)

    scale = hd ** -0.5
    for h in range(heads):
        cols = slice(h * hd, (h + 1) * hd)
        qh = q_ref[:, cols]
        kh = kv_ref[:, cols]
        vh = kv_ref[:, xw + h * hd: xw + (h + 1) * hd]
        s = lax.dot_general(qh, kh, (((1,), (1,)), ((), ())), preferred_element_type=F32) * scale
        e = jnp.exp(s - jnp.max(s, axis=-1, keepdims=True))
        p = (e / jnp.sum(e, axis=-1, keepdims=True)).astype(BF16)
        yh = jnp.dot(p, vh, preferred_element_type=F32)
        y_ref[:, gw + cw + h * hd: gw + cw + (h + 1) * hd] = yh.astype(y_ref.dtype)


def _branches(proj, kv, wsp, bsp, gg, cw, *, seq, ts, gmlp_w, conv_w, xattn_w, n_mem):
    n = proj.shape[0]
    chunk = wsp.shape[1]
    groups = wsp.shape[0]
    sub = BF16_ROW_TILE
    off_cb = 2 * gmlp_w
    cb_blk = off_cb // conv_w
    q_blk = (off_cb + 3 * conv_w) // xattn_w
    last = n // sub - 1
    tsb = ts // sub

    def prev_map(col):
        return lambda i: (jnp.maximum(i * tsb - 1, 0), col)

    def next_map(col):
        return lambda i: (jnp.minimum((i + 1) * tsb, last), col)

    kern = functools.partial(_branch_kernel, seq=seq, chunk=chunk, groups=groups, heads=XATTN_HEADS)
    return pl.pallas_call(
        kern,
        grid=(n // ts,),
        in_specs=[
            pl.BlockSpec((ts, gmlp_w), lambda i: (i, 0)),
            pl.BlockSpec((ts, gmlp_w), lambda i: (i, 1)),
            pl.BlockSpec((ts, conv_w), lambda i: (i, cb_blk)),
            pl.BlockSpec((ts, conv_w), lambda i: (i, cb_blk + 1)),
            pl.BlockSpec((ts, conv_w), lambda i: (i, cb_blk + 2)),
            pl.BlockSpec((ts, xattn_w), lambda i: (i, q_blk)),
            pl.BlockSpec((sub, conv_w), prev_map(cb_blk + 1)),
            pl.BlockSpec((sub, conv_w), prev_map(cb_blk + 2)),
            pl.BlockSpec((sub, conv_w), next_map(cb_blk + 1)),
            pl.BlockSpec((sub, conv_w), next_map(cb_blk + 2)),
            pl.BlockSpec((n_mem, 2 * xattn_w), lambda i: ((i * ts) // seq, 0)),
            pl.BlockSpec(wsp.shape, lambda i: (0, 0, 0)),
            pl.BlockSpec(bsp.shape, lambda i: (0, 0, 0)),
            pl.BlockSpec((1, gmlp_w), lambda i: (0, 0)),
            pl.BlockSpec(cw.shape, lambda i: (0, 0)),
        ],
        out_specs=pl.BlockSpec((ts, gmlp_w + conv_w + xattn_w), lambda i: (i, 0)),
        out_shape=jax.ShapeDtypeStruct((n, gmlp_w + conv_w + xattn_w), BF16),
        compiler_params=_params("parallel"),
        name="branches",
    )(proj, proj, proj, proj, proj, proj, proj, proj, proj, proj, kv, wsp, bsp, gg, cw)


def _merge_kernel(y_ref, w_ref, ga_ref, gb_ref, gx_ref, o_ref, *, gmlp_w, conv_w, sub):
    ya = y_ref[:, :gmlp_w]
    yb = y_ref[:, gmlp_w:gmlp_w + conv_w]
    yx = y_ref[:, gmlp_w + conv_w:]
    for s in range(o_ref.shape[1] // sub):
        cols = slice(s * sub, (s + 1) * sub)
        a = jnp.dot(ya, w_ref[:gmlp_w, cols], preferred_element_type=F32)
        b = jnp.dot(yb, w_ref[gmlp_w:gmlp_w + conv_w, cols], preferred_element_type=F32)
        x = jnp.dot(yx, w_ref[gmlp_w + conv_w:, cols], preferred_element_type=F32)
        o_ref[:, cols] = (ga_ref[:, cols].astype(F32) * a + gb_ref[:, cols].astype(F32) * b
                          + gx_ref[:, cols].astype(F32) * x).astype(o_ref.dtype)


def _merge(ycat, w_br, proj, *, tm, tn, off_g, d, gmlp_w, conv_w):
    n, k = ycat.shape
    g0 = off_g // tn
    gstep = d // tn
    kern = functools.partial(_merge_kernel, gmlp_w=gmlp_w, conv_w=conv_w, sub=min(MXU_COLS, tn))
    return pl.pallas_call(
        kern,
        grid=(n // tm, d // tn),
        in_specs=[pl.BlockSpec((tm, k), lambda i, j: (i, 0)),
                  pl.BlockSpec((k, tn), lambda i, j: (0, j)),
                  pl.BlockSpec((tm, tn), lambda i, j: (i, g0 + j)),
                  pl.BlockSpec((tm, tn), lambda i, j: (i, g0 + gstep + j)),
                  pl.BlockSpec((tm, tn), lambda i, j: (i, g0 + 2 * gstep + j))],
        out_specs=pl.BlockSpec((tm, tn), lambda i, j: (i, j)),
        out_shape=jax.ShapeDtypeStruct((n, d), BF16),
        compiler_params=_params("parallel", "arbitrary"),
        name="merge",
    )(ycat, w_br, proj, proj, proj)


def _outproj_kernel(m_ref, w_ref, x_ref, o_ref, *, sub):
    m = m_ref[...]
    for s in range(o_ref.shape[1] // sub):
        cols = slice(s * sub, (s + 1) * sub)
        o_ref[:, cols] = x_ref[:, cols] + jnp.dot(m, w_ref[:, cols], preferred_element_type=F32)


def _outproj(merged, w_out, x, tm, tn):
    n, d = x.shape
    return pl.pallas_call(
        functools.partial(_outproj_kernel, sub=min(MXU_COLS, tn)),
        grid=(n // tm, d // tn),
        in_specs=[pl.BlockSpec((tm, d), lambda i, j: (i, 0)),
                  pl.BlockSpec((d, tn), lambda i, j: (0, j)),
                  pl.BlockSpec((tm, tn), lambda i, j: (i, j))],
        out_specs=pl.BlockSpec((tm, tn), lambda i, j: (i, j)),
        out_shape=jax.ShapeDtypeStruct((n, d), F32),
        compiler_params=_params("parallel", "arbitrary"),
        name="out_proj",
    )(merged, w_out, x)


def _pack_bf16_pair(lo, hi):
    lo_bits = pltpu.bitcast(lo.astype(BF16).astype(F32), jnp.uint32)
    hi_bits = pltpu.bitcast(hi.astype(BF16).astype(F32), jnp.uint32)
    return lax.shift_right_logical(lo_bits, jnp.uint32(16)) | (hi_bits & jnp.uint32(0xFFFF0000))


def _unpack_bf16_pair(words):
    lo = pltpu.bitcast(lax.shift_left(words, jnp.uint32(16)), F32)
    hi = pltpu.bitcast(words & jnp.uint32(0xFFFF0000), F32)
    return lo.astype(BF16), hi.astype(BF16)


def _router_kernel(x_ref, g_ref, wrt_ref, hp_ref, afft_ref):
    hf = _rms_scale(x_ref[...]) * g_ref[...]
    half = hf.shape[1] // 2
    words = _pack_bf16_pair(hf[:, :half], hf[:, half:])
    pw = hp_ref.shape[2]
    for c in range(hp_ref.shape[0]):
        hp_ref[c] = words[:, c * pw:(c + 1) * pw]
    lt = lax.dot_general(wrt_ref[...], hf.astype(BF16), (((1,), (1,)), ((), ())),
                         preferred_element_type=F32)
    et = jnp.exp(lt - jnp.max(lt, axis=0, keepdims=True))
    afft_ref[...] = et / jnp.sum(et, axis=0, keepdims=True)


def _router(x2, g, wr, tm, pieces):
    n, d = x2.shape
    ne = wr.shape[1]
    pw = d // 2 // pieces
    return pl.pallas_call(
        _router_kernel,
        grid=(n // tm,),
        in_specs=[pl.BlockSpec((tm, d), lambda i: (i, 0)),
                  pl.BlockSpec((1, d), lambda i: (0, 0)),
                  pl.BlockSpec((ne, d), lambda i: (0, 0))],
        out_specs=[pl.BlockSpec((pieces, tm, pw), lambda i: (0, i, 0)),
                   pl.BlockSpec((ne, tm), lambda i: (0, i))],
        out_shape=[jax.ShapeDtypeStruct((pieces, n, pw), jnp.uint32),
                   jax.ShapeDtypeStruct((ne, n), F32)],
        compiler_params=_params("parallel"),
        name="ffn_norm_router",
    )(x2, g.reshape(1, d), wr.T)


def _select_kernel(aff_ref, slot_ref, base_ref, *, cap, tb):
    ne, n = aff_ref.shape
    bits = pltpu.bitcast(aff_ref[...], jnp.int32)
    capf = jnp.float32(cap)

    def search(k, t):
        cand = t | lax.shift_left(jnp.int32(1), 30 - k)
        cnt = jnp.sum(jnp.where(bits >= cand, 1.0, 0.0), axis=1, keepdims=True)
        return jnp.where(cnt >= capf, cand, t)

    thr = lax.fori_loop(0, 31, search, jnp.zeros((ne, 1), jnp.int32))
    n_gt = jnp.sum(jnp.where(bits > thr, 1.0, 0.0), axis=1, keepdims=True)
    need = capf - n_gt

    r = lax.broadcasted_iota(jnp.int32, (tb, tb), 0)
    c = lax.broadcasted_iota(jnp.int32, (tb, tb), 1)
    tri = jnp.where(r < c, 1.0, 0.0).astype(BF16)
    lane = lax.broadcasted_iota(jnp.int32, base_ref.shape, 1)
    eq_seen = jnp.zeros((ne, 1), F32)
    sel_seen = jnp.zeros((ne, 1), F32)
    base = jnp.zeros(base_ref.shape, F32)
    for b in range(n // tb):
        blk = pltpu.bitcast(aff_ref[:, b * tb:(b + 1) * tb], jnp.int32)
        eq = jnp.where(blk == thr, 1.0, 0.0)
        eq_rank = jnp.dot(eq.astype(BF16), tri, preferred_element_type=F32) + eq_seen
        sel = jnp.where(blk > thr, 1.0, jnp.where(eq_rank < need, eq, 0.0))
        pos = jnp.dot(sel.astype(BF16), tri, preferred_element_type=F32) + sel_seen
        slot_ref[:, b * tb:(b + 1) * tb] = jnp.where(sel > 0.0, pos, -1.0).astype(jnp.int32)
        base = jnp.where(lane == b, sel_seen, base)
        eq_seen = eq_seen + jnp.sum(eq, axis=1, keepdims=True)
        sel_seen = sel_seen + jnp.sum(sel, axis=1, keepdims=True)
    base = jnp.where(lane == n // tb, sel_seen, base)
    base_ref[...] = base.astype(jnp.int32)


def _select(afft, cap, tb):
    ne, n = afft.shape
    lanes = 128
    assert n // tb < lanes
    kern = functools.partial(_select_kernel, cap=cap, tb=tb)
    return pl.pallas_call(
        kern,
        out_shape=[jax.ShapeDtypeStruct((ne, n), jnp.int32),
                   jax.ShapeDtypeStruct((ne, lanes), jnp.int32)],
        compiler_params=pltpu.CompilerParams(vmem_limit_bytes=VMEM_LIMIT_BYTES),
        name="expert_choice_select",
    )(afft)


def _pick_list_kernel(base_ref, slot_ref, aff_ref, tok_ref, gw_ref, *, cap, nblk, chunk):
    eb = pl.program_id(0)
    b = pl.program_id(1)

    @pl.when(b == 0)
    def _():
        tok_ref[...] = jnp.zeros_like(tok_ref)
        gw_ref[...] = jnp.zeros_like(gw_ref)

    tb = slot_ref.shape[2]
    token = (lax.broadcasted_iota(jnp.int32, (1, tb), 1) + b * tb).astype(F32)
    for k in range(tok_ref.shape[0]):
        e = eb * tok_ref.shape[0] + k
        first = (base_ref[e * (nblk + 1) + b] // SUBLANES) * SUBLANES
        end = base_ref[e * (nblk + 1) + b + 1]
        slot = slot_ref[k]
        aff = aff_ref[k]

        def body(c, carry, k=k, first=first, slot=slot, aff=aff):
            lo = first + c * chunk
            st = pl.multiple_of(jnp.minimum(lo, cap - chunk), SUBLANES)
            rows = lax.broadcasted_iota(jnp.int32, (chunk, tb), 0) + st
            hit = jnp.where(rows == slot, jnp.where(rows >= lo, 1.0, 0.0), 0.0)
            tok_ref[k, pl.ds(st, chunk), :] += jnp.sum(hit * token, axis=1,
                                                       keepdims=True).astype(jnp.int32)
            gw_ref[k, pl.ds(st, chunk), :] += jnp.sum(hit * aff, axis=1, keepdims=True)
            return carry

        lax.fori_loop(0, (end - first + chunk - 1) // chunk, body, 0)


def _pick_list(base_flat, slot3, aff3, *, cap, tb, chunk, experts_per_step):
    ne, _, n = slot3.shape
    nblk = n // tb
    eps = experts_per_step
    assert cap >= chunk and (cap - chunk) % SUBLANES == 0 and chunk % SUBLANES == 0
    kern = functools.partial(_pick_list_kernel, cap=cap, nblk=nblk, chunk=chunk)
    return pl.pallas_call(
        kern,
        grid_spec=pltpu.PrefetchScalarGridSpec(
            num_scalar_prefetch=1,
            grid=(ne // eps, nblk),
            in_specs=[pl.BlockSpec((eps, 1, tb), lambda e, b, base: (e, 0, b)),
                      pl.BlockSpec((eps, 1, tb), lambda e, b, base: (e, 0, b))],
            out_specs=[pl.BlockSpec((eps, cap, 1), lambda e, b, base: (e, 0, 0)),
                       pl.BlockSpec((eps, cap, 1), lambda e, b, base: (e, 0, 0))],
        ),
        out_shape=[jax.ShapeDtypeStruct((ne, cap, 1), jnp.int32),
                   jax.ShapeDtypeStruct((ne, cap, 1), F32)],
        compiler_params=_params("parallel", "arbitrary"),
        name="expert_pick_list",
    )(base_flat, slot3, aff3)


def _row_gather_sparsecore(table, rows, *, window):
    num, width = rows.shape[0], table.shape[1]
    mesh = plsc.VectorSubcoreMesh(core_axis_name="c", subcore_axis_name="s")

    @pl.kernel(out_type=jax.ShapeDtypeStruct((num, width), table.dtype), mesh=mesh,
               name="expert_row_gather")
    def gather(table_hbm, rows_hbm, out_hbm):
        def body(rows_vmem, out_vmem):
            pltpu.sync_copy(table_hbm.at[rows_vmem.at[0]], out_vmem)

        pltpu.emit_pipeline(
            body,
            grid=(num // window,),
            in_specs=[pl.BlockSpec((1, window), index_map=lambda i: (0, i))],
            out_specs=[pl.BlockSpec((window, width), index_map=lambda i: (i, 0))],
            core_axis_name=("c", "s"),
            dimension_semantics=(pltpu.PARALLEL,),
        )(rows_hbm, out_hbm)

    return gather(table, rows.reshape(1, num))


def _gather(tok, hp, *, window):
    ne, cap, _ = tok.shape
    pieces, n, pw = hp.shape
    rows = (jnp.arange(pieces, dtype=jnp.int32)[:, None] * n + tok.reshape(1, ne * cap)).reshape(-1)
    out = _row_gather_sparsecore(hp.reshape(pieces * n, pw), rows, window=window)
    return out.reshape(pieces, ne, cap, pw)


def _ffn_up_kernel(*refs, groups, sub):
    xp_refs = refs[:groups]
    wg_ref, wu_ref = refs[groups:groups + 2]
    hid_refs = refs[groups + 2:2 * groups + 2]
    xe_refs = refs[2 * groups + 2:]

    @pl.when(pl.program_id(1) == 0)
    def _():
        for xp_ref, xe_ref in zip(xp_refs, xe_refs):
            pieces, _, pw = xp_ref.shape
            for c in range(pieces):
                lo, hi = _unpack_bf16_pair(xp_ref[c])
                xe_ref[:, c * pw:(c + 1) * pw] = lo
                xe_ref[:, (pieces + c) * pw:(pieces + c + 1) * pw] = hi

    for s in range(hid_refs[0].shape[1] // sub):
        cols = slice(s * sub, (s + 1) * sub)
        wg = wg_ref[:, cols].astype(BF16)
        wu = wu_ref[:, cols].astype(BF16)
        for xe_ref, hid_ref in zip(xe_refs, hid_refs):
            x = xe_ref[...]
            g = jnp.dot(x, wg, preferred_element_type=F32)
            u = jnp.dot(x, wu, preferred_element_type=F32)
            hid_ref[:, cols] = ((g * _sigmoid(g)) * u).astype(hid_ref.dtype)


def _ffn_down_kernel(*refs, groups, sub):
    hid_refs = refs[:groups]
    gw_refs = refs[groups:2 * groups]
    wd_ref = refs[2 * groups]
    ye_refs = refs[2 * groups + 1:]
    for s in range(ye_refs[0].shape[1] // sub):
        cols = slice(s * sub, (s + 1) * sub)
        wd = wd_ref[:, cols].astype(BF16)
        for hid_ref, gw_ref, ye_ref in zip(hid_refs, gw_refs, ye_refs):
            y = jnp.dot(hid_ref[...], wd, preferred_element_type=F32)
            ye_ref[:, cols] = (y * gw_ref[...]).astype(ye_ref.dtype)


def _ffn(xps, gws, wg, wu, wd, tf, tn):
    groups = len(xps)
    pieces, ne, _, pw = xps[0].shape
    caps = [xp.shape[2] for xp in xps]
    d, f = wg.shape[1], wg.shape[2]
    assert d == 2 * pieces * pw
    params = pltpu.CompilerParams(dimension_semantics=("parallel", "arbitrary"),
                                  vmem_limit_bytes=FFN_VMEM_LIMIT_BYTES)
    hids = pl.pallas_call(
        functools.partial(_ffn_up_kernel, groups=groups, sub=min(MXU_COLS, tf)),
        grid=(ne, f // tf),
        in_specs=[pl.BlockSpec((pieces, None, c, pw), lambda e, j: (0, e, 0, 0),
                               pipeline_mode=pl.Buffered(1)) for c in caps]
                 + [pl.BlockSpec((None, d, tf), lambda e, j: (e, 0, j)),
                    pl.BlockSpec((None, d, tf), lambda e, j: (e, 0, j))],
        out_specs=[pl.BlockSpec((None, c, tf), lambda e, j: (e, 0, j)) for c in caps],
        out_shape=[jax.ShapeDtypeStruct((ne, c, f), BF16) for c in caps],
        scratch_shapes=[pltpu.VMEM((c, d), BF16) for c in caps],
        compiler_params=params,
        name="expert_ffn_up",
    )(*xps, wg, wu)
    return pl.pallas_call(
        functools.partial(_ffn_down_kernel, groups=groups, sub=min(MXU_COLS, tn)),
        grid=(ne, d // tn),
        in_specs=[pl.BlockSpec((None, c, f), lambda e, j: (e, 0, 0)) for c in caps]
                 + [pl.BlockSpec((None, c, 1), lambda e, j: (e, 0, 0)) for c in caps]
                 + [pl.BlockSpec((None, f, tn), lambda e, j: (e, 0, j))],
        out_specs=[pl.BlockSpec((None, c, tn), lambda e, j: (e, 0, j)) for c in caps],
        out_shape=[jax.ShapeDtypeStruct((ne, c, d), BF16) for c in caps],
        compiler_params=params,
        name="expert_ffn_down",
    )(*hids, *gws, wd)


def _expert_column(ref, e):
    lane = lax.broadcasted_iota(jnp.int32, ref.shape, 1)
    return jnp.sum(jnp.where(lane == e, ref[...], 0), axis=1, keepdims=True)


def _combine_any_kernel(base_ref, slot_ref, x_ref, ya_ref, yb_ref, g_ref, o_ref, *, cap, nblk):
    b = pl.program_id(0)
    e = pl.program_id(1)
    ne = pl.num_programs(1)

    @pl.when(e == 0)
    def _():
        o_ref[...] = x_ref[...]

    tb = x_ref.shape[0]
    blk = ya_ref.shape[0]
    start = jnp.minimum(base_ref[e * (nblk + 1) + b] // blk, cap // blk - 1) * blk
    slot = _expert_column(slot_ref, e)
    cols = lax.broadcasted_iota(jnp.int32, (tb, blk), 1) + start
    oh_a = jnp.where(cols == slot, 1.0, 0.0).astype(BF16)
    oh_b = jnp.where(cols + blk == slot, 1.0, 0.0).astype(BF16)
    o_ref[...] += (jnp.dot(oh_a, ya_ref[...], preferred_element_type=F32)
                   + jnp.dot(oh_b, yb_ref[...], preferred_element_type=F32))

    @pl.when(e == ne - 1)
    def _():
        o_ref[...] = _rms_scale(o_ref[...]) * g_ref[...]


def _combine_any(base_flat, slot_tok, x2, ye, g_final, *, cap, tb):
    n, d = x2.shape
    ne = ye.shape[0]
    nblk = n // tb
    blk = tb
    last = cap // blk - 1

    def ya_map(b, e, base):
        return (e, jnp.minimum(base[e * (nblk + 1) + b] // blk, last), 0)

    def yb_map(b, e, base):
        return (e, jnp.minimum(jnp.minimum(base[e * (nblk + 1) + b] // blk, last) + 1, last), 0)

    kern = functools.partial(_combine_any_kernel, cap=cap, nblk=nblk)
    return pl.pallas_call(
        kern,
        grid_spec=pltpu.PrefetchScalarGridSpec(
            num_scalar_prefetch=1,
            grid=(nblk, ne),
            in_specs=[pl.BlockSpec((tb, ne), lambda b, e, base: (b, 0)),
                      pl.BlockSpec((tb, d), lambda b, e, base: (b, 0)),
                      pl.BlockSpec((None, blk, d), ya_map),
                      pl.BlockSpec((None, blk, d), yb_map),
                      pl.BlockSpec((1, d), lambda b, e, base: (0, 0))],
            out_specs=pl.BlockSpec((tb, d), lambda b, e, base: (b, 0)),
        ),
        out_shape=jax.ShapeDtypeStruct((n, d), F32),
        compiler_params=_params("parallel", "arbitrary"),
        name="expert_combine_any",
    )(base_flat, slot_tok, x2, ye, ye, g_final.reshape(1, d))


def _window_start(base, e, b, *, nblk, cap, win):
    tile = jnp.minimum(base[e * (nblk + 1) + b] // BF16_ROW_TILE, (cap - win) // BF16_ROW_TILE)
    return tile * BF16_ROW_TILE


def _combine_windows_kernel(base_ref, slot_ref, x_ref, g_ref, *rest, cap, nblk, win):
    y_refs, o_ref = rest[:-1], rest[-1]
    b = pl.program_id(0)
    s = pl.program_id(1)
    per = len(y_refs)

    @pl.when(s == 0)
    def _():
        o_ref[...] = x_ref[...]

    tb = x_ref.shape[0]
    col = lax.broadcasted_iota(jnp.int32, (tb, win), 1)
    onehots = []
    for k in range(per):
        e = s * per + k
        start = _window_start(base_ref, e, b, nblk=nblk, cap=cap, win=win)
        onehots.append(jnp.where(col + start == _expert_column(slot_ref, e), 1.0, 0.0).astype(BF16))
    y = jnp.concatenate([r[...] for r in y_refs], axis=0)
    o_ref[...] += jnp.dot(jnp.concatenate(onehots, axis=1), y, preferred_element_type=F32)

    @pl.when(s == pl.num_programs(1) - 1)
    def _():
        o_ref[...] = _rms_scale(o_ref[...]) * g_ref[...]


def _combine_windows(base_flat, slot_tok, x2, ye, g_final, *, cap, tb, win, experts_per_step):
    n, d = x2.shape
    ne = ye.shape[0]
    nblk = n // tb
    per = experts_per_step
    assert ne % per == 0

    def y_map(k):
        def index(b, s, base):
            e = s * per + k
            return (e, _window_start(base, e, b, nblk=nblk, cap=cap, win=win), 0)
        return index

    kern = functools.partial(_combine_windows_kernel, cap=cap, nblk=nblk, win=win)
    return pl.pallas_call(
        kern,
        grid_spec=pltpu.PrefetchScalarGridSpec(
            num_scalar_prefetch=1,
            grid=(nblk, ne // per),
            in_specs=[pl.BlockSpec((tb, ne), lambda b, s, base: (b, 0)),
                      pl.BlockSpec((tb, d), lambda b, s, base: (b, 0)),
                      pl.BlockSpec((1, d), lambda b, s, base: (0, 0))]
                     + [pl.BlockSpec((None, pl.Element(win), pl.Element(d)), y_map(k))
                        for k in range(per)],
            out_specs=pl.BlockSpec((tb, d), lambda b, s, base: (b, 0)),
        ),
        out_shape=jax.ShapeDtypeStruct((n, d), F32),
        compiler_params=_params("parallel", "arbitrary"),
        name="expert_combine_windows",
    )(base_flat, slot_tok, x2, g_final.reshape(1, d), *([ye] * per))


def _combine(base, slot_tok, x2, ye, g_final, *, cap, tb, win):
    nblk = x2.shape[0] // tb
    base_flat = base[:, :nblk + 1].reshape(-1)
    first = (base[:, :nblk] // BF16_ROW_TILE) * BF16_ROW_TILE
    fits = jnp.all(base[:, 1:nblk + 1] - jnp.minimum(first, cap - win) <= win)
    return lax.cond(
        fits,
        lambda: _combine_windows(base_flat, slot_tok, x2, ye, g_final, cap=cap, tb=tb, win=win,
                                 experts_per_step=8),
        lambda: _combine_any(base_flat, slot_tok, x2, ye, g_final, cap=cap, tb=tb))


def _mix_and_route(x3, mem3, p):
    bsz, seq, d = x3.shape
    n = bsz * seq
    n_mem = mem3.shape[1]
    x = x3.reshape(n, d)
    mem = mem3.reshape(bsz * n_mem, d)

    gmlp_w = p["gmlp_norm_g"].shape[0]
    conv_w = p["conv_w"].shape[1]
    xattn_w = p["w_mem_kv"].shape[1] // 2
    off_cb = 2 * gmlp_w
    off_g = off_cb + 3 * conv_w + xattn_w
    ne = p["w_router"].shape[1]
    cap = CAPACITY_FACTOR * n // ne

    def col_tile(cols):
        return next(t for t in (1024, 512, 256, 128) if cols % t == 0 and t <= cols)

    tm = min(1024, n)
    ts = min(256, seq)
    tb = 256 if cap >= 256 + BF16_ROW_TILE else 128

    h = _rmsnorm_cast(x, p["norm_mix_g"], ts)
    proj = _proj(h, p["w_in"], tm, col_tile(p["w_in"].shape[1]), off_cb, off_g, BF16)
    m = _rmsnorm_cast(mem, p["mem_norm_g"], n_mem)
    kv = _proj(m, p["w_mem_kv"], n_mem, col_tile(2 * xattn_w), 0, 2 * xattn_w, BF16)
    ycat = _branches(proj, kv, p["w_spatial"], p["b_spatial"], p["gmlp_norm_g"].reshape(1, gmlp_w),
                     p["conv_w"], seq=seq, ts=ts, gmlp_w=gmlp_w, conv_w=conv_w, xattn_w=xattn_w,
                     n_mem=n_mem)
    merged = _merge(ycat, p["w_branch"], proj, tm=tm, tn=min(512, col_tile(d)), off_g=off_g, d=d,
                    gmlp_w=gmlp_w, conv_w=conv_w)
    x2 = _outproj(merged, p["w_out"], x, tm, col_tile(d))

    hp, afft = _router(x2, p["norm_ffn_g"], p["w_router"], ts, min(SC_ROW_PIECES, d // 256))
    slot, base = _select(afft, cap, tb)
    nblk = n // tb
    base_flat = base[:, :nblk + 1].reshape(-1)
    tok, gw = _pick_list(base_flat, slot.reshape(ne, 1, n), afft.reshape(ne, 1, n), cap=cap, tb=tb,
                         chunk=PICK_CHUNK_ROWS, experts_per_step=min(16, ne))
    xp = _gather(tok, hp, window=SC_GATHER_WINDOW)
    return {"xp": xp, "gw": gw, "x2": x2, "base": base, "slot_tok": slot.T, "cap": cap, "tb": tb,
            "tok": tok}


def _combine_group(r, ye, g_final, shape):
    y = _combine(r["base"], r["slot_tok"], r["x2"], ye, g_final, cap=r["cap"], tb=r["tb"],
                 win=min(COMBINE_WINDOW_ROWS, r["tb"]))
    return y.reshape(shape)


def kernel(x_prompt, x_sample, mem_prompt, mem_sample, norm_mix_g, w_in, gmlp_norm_g, w_spatial,
           b_spatial, conv_w, mem_norm_g, w_mem_kv, w_branch_a, w_branch_b, w_branch_x, w_out,
           norm_ffn_g, w_router, w_gate, w_up, w_down, final_norm_g):
    assert w_in.shape[0] == 1, "one layer"
    groups, chunk = b_spatial.shape[1], b_spatial.shape[2]
    gc = gmlp_norm_g.shape[1] // groups
    p = {
        "norm_mix_g": norm_mix_g[0],
        "w_in": w_in[0].astype(BF16),
        "gmlp_norm_g": gmlp_norm_g[0],
        "w_spatial": w_spatial[0].astype(BF16),
        "b_spatial": jnp.broadcast_to(b_spatial[0][:, :, None], (groups, chunk, gc)),
        "conv_w": conv_w[0],
        "mem_norm_g": mem_norm_g[0],
        "w_mem_kv": w_mem_kv[0].astype(BF16),
        "w_branch": jnp.concatenate([w_branch_a[0], w_branch_b[0], w_branch_x[0]],
                                    axis=0).astype(BF16),
        "w_out": w_out[0].astype(BF16),
        "norm_ffn_g": norm_ffn_g[0],
        "w_router": w_router[0].astype(BF16),
        "w_gate": w_gate.reshape(w_gate.shape[1:]),
        "w_up": w_up.reshape(w_up.shape[1:]),
        "w_down": w_down.reshape(w_down.shape[1:]),
        "final_norm_g": final_norm_g,
    }
    first = _mix_and_route(x_prompt, mem_prompt, p)
    zero = lax.shift_right_arithmetic(first["tok"][0, 0, 0], 31).astype(F32)
    second = _mix_and_route(x_sample, mem_sample, {**p, "norm_mix_g": p["norm_mix_g"] + zero})
    routed = [first, second]
    d = x_prompt.shape[-1]
    yes = _ffn([r["xp"] for r in routed], [r["gw"] for r in routed], p["w_gate"], p["w_up"],
               p["w_down"], min(FFN_UP_COLS, d), min(FFN_DOWN_COLS, d))
    return tuple(_combine_group(r, ye, final_norm_g, x.shape)
                 for r, ye, x in zip(routed, yes, (x_prompt, x_sample)))
```

```python
import functools

import jax
import jax.numpy as jnp
from jax import lax
from jax.experimental import pallas as pl
from jax.experimental.pallas import tpu as pltpu
from jax.experimental.pallas import tpu_sc as plsc

EPS = 1e-6
XATTN_HEADS = 4
CAPACITY_FACTOR = 2
BF16_ROW_TILE = 16
SUBLANES = 8
PICK_CHUNK_ROWS = 80
SC_ROW_PIECES = 8
SC_GATHER_WINDOW = 128
COMBINE_WINDOW_ROWS = 128
VMEM_LIMIT_BYTES = 56 * 1024 * 1024
FFN_VMEM_LIMIT_BYTES = 60 * 1024 * 1024
FFN_UP_COLS = 256
FFN_DOWN_COLS = 512

F32 = jnp.float32
BF16 = jnp.bfloat16


def _params(*semantics):
    return pltpu.CompilerParams(dimension_semantics=semantics,
                                vmem_limit_bytes=VMEM_LIMIT_BYTES)


GELU_K = 0.7978845608028654
GELU_CUBIC = 0.044715
MXU_COLS = 256


def _sigmoid(x):
    return 1.0 / (1.0 + jnp.exp(-x))


def _rms_scale(x):
    return x * lax.rsqrt(jnp.mean(x * x, axis=-1, keepdims=True) + EPS)


def _rmsnorm_cast_kernel(x_ref, g_ref, o_ref):
    o_ref[...] = (_rms_scale(x_ref[...]) * g_ref[...]).astype(o_ref.dtype)


def _rmsnorm_cast(x, g, tm):
    n, d = x.shape
    return pl.pallas_call(
        _rmsnorm_cast_kernel,
        grid=(n // tm,),
        in_specs=[pl.BlockSpec((tm, d), lambda i: (i, 0)),
                  pl.BlockSpec((1, d), lambda i: (0, 0))],
        out_specs=pl.BlockSpec((tm, d), lambda i: (i, 0)),
        out_shape=jax.ShapeDtypeStruct((n, d), BF16),
        compiler_params=_params("parallel"),
        name="rmsnorm_cast",
    )(x, g.reshape(1, d))


def _gelu_tanh(x):
    return (0.5 * x) * (1.0 + jnp.tanh(GELU_K * (x + GELU_CUBIC * (x * x * x))))


def _sigmoid_tanh(x):
    return 0.5 * (1.0 + jnp.tanh(0.5 * x))


def _proj_kernel(h_ref, w_ref, o_ref, *, sub, epilogue):
    h = h_ref[...]
    for s in range(o_ref.shape[1] // sub):
        cols = slice(s * sub, (s + 1) * sub)
        x = jnp.dot(h, w_ref[:, cols], preferred_element_type=F32)
        o_ref[:, cols] = epilogue(x).astype(o_ref.dtype)


def _proj_range(h, w, col_lo, col_hi, epilogue, tm, tn, out_dtype):
    n, d = h.shape
    tn = min(tn, col_hi - col_lo)
    assert col_lo % tn == 0 and col_hi % tn == 0
    first = col_lo // tn
    kern = functools.partial(_proj_kernel, sub=min(MXU_COLS, tn), epilogue=epilogue)
    return pl.pallas_call(
        kern,
        grid=(n // tm, (col_hi - col_lo) // tn),
        in_specs=[pl.BlockSpec((tm, d), lambda i, j: (i, 0)),
                  pl.BlockSpec((d, tn), lambda i, j: (0, first + j))],
        out_specs=pl.BlockSpec((tm, tn), lambda i, j: (i, j)),
        out_shape=jax.ShapeDtypeStruct((n, col_hi - col_lo), out_dtype),
        compiler_params=_params("parallel", "arbitrary"),
        name="in_proj",
    )(h, w)


def _proj(h, w, tm, tn, gelu_cols, sigmoid_from_col, out_dtype):
    cols = w.shape[1]
    return (_proj_range(h, w, 0, gelu_cols, _gelu_tanh, tm, tn, out_dtype),
            _proj_range(h, w, gelu_cols, sigmoid_from_col, lambda x: x, tm, tn, out_dtype),
            _proj_range(h, w, sigmoid_from_col, cols, _sigmoid_tanh, tm, tn, out_dtype))


def _branch_kernel(u_ref, v_ref, cb_ref, cc_ref, ch_ref, q_ref, ccp_ref, chp_ref, ccn_ref, chn_ref,
                   kv_ref, wsp_ref, bsp_ref, gg_ref, cw_ref, y_ref, *, seq, chunk, groups, heads):
    i = pl.program_id(0)
    ts = u_ref.shape[0]
    gw = u_ref.shape[1]
    gc = gw // groups
    cw = cb_ref.shape[1]
    xw = q_ref.shape[1]
    hd = xw // heads

    vn = (_rms_scale(v_ref[...].astype(F32)) * gg_ref[...]).astype(BF16)
    for c in range(ts // chunk):
        rows = slice(c * chunk, (c + 1) * chunk)
        for g in range(groups):
            cols = slice(g * gc, (g + 1) * gc)
            mixed = jnp.dot(wsp_ref[g], vn[rows, cols], preferred_element_type=F32) + bsp_ref[g]
            y_ref[rows, cols] = (u_ref[rows, cols].astype(F32) * mixed).astype(y_ref.dtype)

    pos = (i * ts) % seq
    z = cc_ref[...].astype(F32) * ch_ref[...].astype(F32)
    sub = ccp_ref.shape[0]
    z_before = jnp.where(pos == 0, 0.0, ccp_ref[sub - 1:sub, :].astype(F32)
                         * chp_ref[sub - 1:sub, :].astype(F32))
    z_after = jnp.where(pos + ts == seq, 0.0, ccn_ref[0:1, :].astype(F32)
                        * chn_ref[0:1, :].astype(F32))
    row = lax.broadcasted_iota(jnp.int32, z.shape, 0)
    z_prev = jnp.where(row == 0, z_before, pltpu.roll(z, 1, 0))
    z_next = jnp.where(row == ts - 1, z_after, pltpu.roll(z, ts - 1, 0))
    conv = z_prev * cw_ref[0:1, :] + z * cw_ref[1:2, :] + z_next * cw_ref[2:3, :]
    y_ref[:, gw:gw + cw] = (cb_ref[...].astype(F32) * conv).astype(y_ref.dtype)

    scale = hd ** -0.5
    for h in range(heads):
        cols = slice(h * hd, (h + 1) * hd)
        qh = q_ref[:, cols]
        kh = kv_ref[:, cols]
        vh = kv_ref[:, xw + h * hd: xw + (h + 1) * hd]
        s = lax.dot_general(qh, kh, (((1,), (1,)), ((), ())), preferred_element_type=F32) * scale
        e = jnp.exp(s - jnp.max(s, axis=-1, keepdims=True))
        p = (e / jnp.sum(e, axis=-1, keepdims=True)).astype(BF16)
        yh = jnp.dot(p, vh, preferred_element_type=F32)
        y_ref[:, gw + cw + h * hd: gw + cw + (h + 1) * hd] = yh.astype(y_ref.dtype)


def _branches(uv, mid, kv, wsp, bsp, gg, cw, *, seq, ts, gmlp_w, conv_w, xattn_w, n_mem):
    n = uv.shape[0]
    chunk = wsp.shape[1]
    groups = wsp.shape[0]
    sub = BF16_ROW_TILE
    cb_blk = 0
    q_blk = 3 * conv_w // xattn_w
    last = n // sub - 1
    tsb = ts // sub

    def prev_map(col):
        return lambda i: (jnp.maximum(i * tsb - 1, 0), col)

    def next_map(col):
        return lambda i: (jnp.minimum((i + 1) * tsb, last), col)

    kern = functools.partial(_branch_kernel, seq=seq, chunk=chunk, groups=groups, heads=XATTN_HEADS)
    return pl.pallas_call(
        kern,
        grid=(n // ts,),
        in_specs=[
            pl.BlockSpec((ts, gmlp_w), lambda i: (i, 0)),
            pl.BlockSpec((ts, gmlp_w), lambda i: (i, 1)),
            pl.BlockSpec((ts, conv_w), lambda i: (i, cb_blk)),
            pl.BlockSpec((ts, conv_w), lambda i: (i, cb_blk + 1)),
            pl.BlockSpec((ts, conv_w), lambda i: (i, cb_blk + 2)),
            pl.BlockSpec((ts, xattn_w), lambda i: (i, q_blk)),
            pl.BlockSpec((sub, conv_w), prev_map(cb_blk + 1)),
            pl.BlockSpec((sub, conv_w), prev_map(cb_blk + 2)),
            pl.BlockSpec((sub, conv_w), next_map(cb_blk + 1)),
            pl.BlockSpec((sub, conv_w), next_map(cb_blk + 2)),
            pl.BlockSpec((n_mem, 2 * xattn_w), lambda i: ((i * ts) // seq, 0)),
            pl.BlockSpec(wsp.shape, lambda i: (0, 0, 0)),
            pl.BlockSpec(bsp.shape, lambda i: (0, 0, 0)),
            pl.BlockSpec((1, gmlp_w), lambda i: (0, 0)),
            pl.BlockSpec(cw.shape, lambda i: (0, 0)),
        ],
        out_specs=pl.BlockSpec((ts, gmlp_w + conv_w + xattn_w), lambda i: (i, 0)),
        out_shape=jax.ShapeDtypeStruct((n, gmlp_w + conv_w + xattn_w), BF16),
        compiler_params=_params("parallel"),
        name="branches",
    )(uv, uv, mid, mid, mid, mid, mid, mid, mid, mid, kv, wsp, bsp, gg, cw)


def _merge_kernel(y_ref, w_ref, ga_ref, gb_ref, gx_ref, o_ref, *, gmlp_w, conv_w, sub):
    ya = y_ref[:, :gmlp_w]
    yb = y_ref[:, gmlp_w:gmlp_w + conv_w]
    yx = y_ref[:, gmlp_w + conv_w:]
    for s in range(o_ref.shape[1] // sub):
        cols = slice(s * sub, (s + 1) * sub)
        a = jnp.dot(ya, w_ref[:gmlp_w, cols], preferred_element_type=F32)
        b = jnp.dot(yb, w_ref[gmlp_w:gmlp_w + conv_w, cols], preferred_element_type=F32)
        x = jnp.dot(yx, w_ref[gmlp_w + conv_w:, cols], preferred_element_type=F32)
        o_ref[:, cols] = (ga_ref[:, cols].astype(F32) * a + gb_ref[:, cols].astype(F32) * b
                          + gx_ref[:, cols].astype(F32) * x).astype(o_ref.dtype)


def _merge(ycat, w_br, gates, *, tm, tn, d, gmlp_w, conv_w):
    n, k = ycat.shape
    g0 = 0
    gstep = d // tn
    kern = functools.partial(_merge_kernel, gmlp_w=gmlp_w, conv_w=conv_w, sub=min(MXU_COLS, tn))
    return pl.pallas_call(
        kern,
        grid=(n // tm, d // tn),
        in_specs=[pl.BlockSpec((tm, k), lambda i, j: (i, 0)),
                  pl.BlockSpec((k, tn), lambda i, j: (0, j)),
                  pl.BlockSpec((tm, tn), lambda i, j: (i, g0 + j)),
                  pl.BlockSpec((tm, tn), lambda i, j: (i, g0 + gstep + j)),
                  pl.BlockSpec((tm, tn), lambda i, j: (i, g0 + 2 * gstep + j))],
        out_specs=pl.BlockSpec((tm, tn), lambda i, j: (i, j)),
        out_shape=jax.ShapeDtypeStruct((n, d), BF16),
        compiler_params=_params("parallel", "arbitrary"),
        name="merge",
    )(ycat, w_br, gates, gates, gates)


def _outproj_kernel(m_ref, w_ref, x_ref, o_ref, *, sub):
    m = m_ref[...]
    for s in range(o_ref.shape[1] // sub):
        cols = slice(s * sub, (s + 1) * sub)
        o_ref[:, cols] = x_ref[:, cols] + jnp.dot(m, w_ref[:, cols], preferred_element_type=F32)


def _outproj(merged, w_out, x, tm, tn):
    n, d = x.shape
    return pl.pallas_call(
        functools.partial(_outproj_kernel, sub=min(MXU_COLS, tn)),
        grid=(n // tm, d // tn),
        in_specs=[pl.BlockSpec((tm, d), lambda i, j: (i, 0)),
                  pl.BlockSpec((d, tn), lambda i, j: (0, j)),
                  pl.BlockSpec((tm, tn), lambda i, j: (i, j))],
        out_specs=pl.BlockSpec((tm, tn), lambda i, j: (i, j)),
        out_shape=jax.ShapeDtypeStruct((n, d), F32),
        compiler_params=_params("parallel", "arbitrary"),
        name="out_proj",
    )(merged, w_out, x)


def _pack_bf16_pair(lo, hi):
    lo_bits = pltpu.bitcast(lo.astype(BF16).astype(F32), jnp.uint32)
    hi_bits = pltpu.bitcast(hi.astype(BF16).astype(F32), jnp.uint32)
    return lax.shift_right_logical(lo_bits, jnp.uint32(16)) | (hi_bits & jnp.uint32(0xFFFF0000))


def _unpack_bf16_pair(words):
    lo = pltpu.bitcast(lax.shift_left(words, jnp.uint32(16)), F32)
    hi = pltpu.bitcast(words & jnp.uint32(0xFFFF0000), F32)
    return lo.astype(BF16), hi.astype(BF16)


def _router_kernel(x_ref, g_ref, wrt_ref, hp_ref, afft_ref):
    hf = _rms_scale(x_ref[...]) * g_ref[...]
    half = hf.shape[1] // 2
    words = _pack_bf16_pair(hf[:, :half], hf[:, half:])
    pw = hp_ref.shape[2]
    for c in range(hp_ref.shape[0]):
        hp_ref[c] = words[:, c * pw:(c + 1) * pw]
    lt = lax.dot_general(wrt_ref[...], hf.astype(BF16), (((1,), (1,)), ((), ())),
                         preferred_element_type=F32)
    et = jnp.exp(lt - jnp.max(lt, axis=0, keepdims=True))
    afft_ref[...] = et / jnp.sum(et, axis=0, keepdims=True)


def _router(x2, g, wr, tm, pieces):
    n, d = x2.shape
    ne = wr.shape[1]
    pw = d // 2 // pieces
    return pl.pallas_call(
        _router_kernel,
        grid=(n // tm,),
        in_specs=[pl.BlockSpec((tm, d), lambda i: (i, 0)),
                  pl.BlockSpec((1, d), lambda i: (0, 0)),
                  pl.BlockSpec((ne, d), lambda i: (0, 0))],
        out_specs=[pl.BlockSpec((pieces, tm, pw), lambda i: (0, i, 0)),
                   pl.BlockSpec((ne, tm), lambda i: (0, i))],
        out_shape=[jax.ShapeDtypeStruct((pieces, n, pw), jnp.uint32),
                   jax.ShapeDtypeStruct((ne, n), F32)],
        compiler_params=_params("parallel"),
        name="ffn_norm_router",
    )(x2, g.reshape(1, d), wr.T)


def _select_kernel(aff_ref, slot_ref, base_ref, *, cap, tb):
    ne, n = aff_ref.shape
    bits = pltpu.bitcast(aff_ref[...], jnp.int32)
    capf = jnp.float32(cap)

    def search(k, t):
        cand = t | lax.shift_left(jnp.int32(1), 30 - k)
        cnt = jnp.sum(jnp.where(bits >= cand, 1.0, 0.0), axis=1, keepdims=True)
        return jnp.where(cnt >= capf, cand, t)

    thr = lax.fori_loop(0, 31, search, jnp.zeros((ne, 1), jnp.int32))
    n_gt = jnp.sum(jnp.where(bits > thr, 1.0, 0.0), axis=1, keepdims=True)
    need = capf - n_gt

    r = lax.broadcasted_iota(jnp.int32, (tb, tb), 0)
    c = lax.broadcasted_iota(jnp.int32, (tb, tb), 1)
    tri = jnp.where(r < c, 1.0, 0.0).astype(BF16)
    lane = lax.broadcasted_iota(jnp.int32, base_ref.shape, 1)
    eq_seen = jnp.zeros((ne, 1), F32)
    sel_seen = jnp.zeros((ne, 1), F32)
    base = jnp.zeros(base_ref.shape, F32)
    for b in range(n // tb):
        blk = pltpu.bitcast(aff_ref[:, b * tb:(b + 1) * tb], jnp.int32)
        eq = jnp.where(blk == thr, 1.0, 0.0)
        eq_rank = jnp.dot(eq.astype(BF16), tri, preferred_element_type=F32) + eq_seen
        sel = jnp.where(blk > thr, 1.0, jnp.where(eq_rank < need, eq, 0.0))
        pos = jnp.dot(sel.astype(BF16), tri, preferred_element_type=F32) + sel_seen
        slot_ref[:, b * tb:(b + 1) * tb] = jnp.where(sel > 0.0, pos, -1.0).astype(jnp.int32)
        base = jnp.where(lane == b, sel_seen, base)
        eq_seen = eq_seen + jnp.sum(eq, axis=1, keepdims=True)
        sel_seen = sel_seen + jnp.sum(sel, axis=1, keepdims=True)
    base = jnp.where(lane == n // tb, sel_seen, base)
    base_ref[...] = base.astype(jnp.int32)


def _select(afft, cap, tb):
    ne, n = afft.shape
    lanes = 128
    assert n // tb < lanes
    kern = functools.partial(_select_kernel, cap=cap, tb=tb)
    return pl.pallas_call(
        kern,
        out_shape=[jax.ShapeDtypeStruct((ne, n), jnp.int32),
                   jax.ShapeDtypeStruct((ne, lanes), jnp.int32)],
        compiler_params=pltpu.CompilerParams(vmem_limit_bytes=VMEM_LIMIT_BYTES),
        name="expert_choice_select",
    )(afft)


def _pick_list_kernel(base_ref, slot_ref, aff_ref, tok_ref, gw_ref, *, cap, nblk, chunk):
    eb = pl.program_id(0)
    b = pl.program_id(1)

    @pl.when(b == 0)
    def _():
        tok_ref[...] = jnp.zeros_like(tok_ref)
        gw_ref[...] = jnp.zeros_like(gw_ref)

    tb = slot_ref.shape[2]
    token = (lax.broadcasted_iota(jnp.int32, (1, tb), 1) + b * tb).astype(F32)
    for k in range(tok_ref.shape[0]):
        e = eb * tok_ref.shape[0] + k
        first = (base_ref[e * (nblk + 1) + b] // SUBLANES) * SUBLANES
        end = base_ref[e * (nblk + 1) + b + 1]
        slot = slot_ref[k]
        aff = aff_ref[k]

        def body(c, carry, k=k, first=first, slot=slot, aff=aff):
            lo = first + c * chunk
            st = pl.multiple_of(jnp.minimum(lo, cap - chunk), SUBLANES)
            rows = lax.broadcasted_iota(jnp.int32, (chunk, tb), 0) + st
            hit = jnp.where(rows == slot, jnp.where(rows >= lo, 1.0, 0.0), 0.0)
            tok_ref[k, pl.ds(st, chunk), :] += jnp.sum(hit * token, axis=1,
                                                       keepdims=True).astype(jnp.int32)
            gw_ref[k, pl.ds(st, chunk), :] += jnp.sum(hit * aff, axis=1, keepdims=True)
            return carry

        lax.fori_loop(0, (end - first + chunk - 1) // chunk, body, 0)


def _pick_list(base_flat, slot3, aff3, *, cap, tb, chunk, experts_per_step):
    ne, _, n = slot3.shape
    nblk = n // tb
    eps = experts_per_step
    assert cap >= chunk and (cap - chunk) % SUBLANES == 0 and chunk % SUBLANES == 0
    kern = functools.partial(_pick_list_kernel, cap=cap, nblk=nblk, chunk=chunk)
    return pl.pallas_call(
        kern,
        grid_spec=pltpu.PrefetchScalarGridSpec(
            num_scalar_prefetch=1,
            grid=(ne // eps, nblk),
            in_specs=[pl.BlockSpec((eps, 1, tb), lambda e, b, base: (e, 0, b)),
                      pl.BlockSpec((eps, 1, tb), lambda e, b, base: (e, 0, b))],
            out_specs=[pl.BlockSpec((eps, cap, 1), lambda e, b, base: (e, 0, 0)),
                       pl.BlockSpec((eps, cap, 1), lambda e, b, base: (e, 0, 0))],
        ),
        out_shape=[jax.ShapeDtypeStruct((ne, cap, 1), jnp.int32),
                   jax.ShapeDtypeStruct((ne, cap, 1), F32)],
        compiler_params=_params("parallel", "arbitrary"),
        name="expert_pick_list",
    )(base_flat, slot3, aff3)


def _row_gather_sparsecore(table, rows, *, window):
    num, width = rows.shape[0], table.shape[1]
    mesh = plsc.VectorSubcoreMesh(core_axis_name="c", subcore_axis_name="s")

    @pl.kernel(out_type=jax.ShapeDtypeStruct((num, width), table.dtype), mesh=mesh,
               name="expert_row_gather")
    def gather(table_hbm, rows_hbm, out_hbm):
        def body(rows_vmem, out_vmem):
            pltpu.sync_copy(table_hbm.at[rows_vmem.at[0]], out_vmem)

        pltpu.emit_pipeline(
            body,
            grid=(num // window,),
            in_specs=[pl.BlockSpec((1, window), index_map=lambda i: (0, i))],
            out_specs=[pl.BlockSpec((window, width), index_map=lambda i: (i, 0))],
            core_axis_name=("c", "s"),
            dimension_semantics=(pltpu.PARALLEL,),
        )(rows_hbm, out_hbm)

    return gather(table, rows.reshape(1, num))


def _gather(tok, hp, *, window):
    ne, cap, _ = tok.shape
    pieces, n, pw = hp.shape
    rows = (jnp.arange(pieces, dtype=jnp.int32)[:, None] * n + tok.reshape(1, ne * cap)).reshape(-1)
    out = _row_gather_sparsecore(hp.reshape(pieces * n, pw), rows, window=window)
    return out.reshape(pieces, ne, cap, pw)


def _ffn_up_kernel(*refs, groups, sub):
    xp_refs = refs[:groups]
    wg_ref, wu_ref = refs[groups:groups + 2]
    hid_refs = refs[groups + 2:2 * groups + 2]
    xe_refs = refs[2 * groups + 2:]

    @pl.when(pl.program_id(1) == 0)
    def _():
        for xp_ref, xe_ref in zip(xp_refs, xe_refs):
            pieces, _, pw = xp_ref.shape
            for c in range(pieces):
                lo, hi = _unpack_bf16_pair(xp_ref[c])
                xe_ref[:, c * pw:(c + 1) * pw] = lo
                xe_ref[:, (pieces + c) * pw:(pieces + c + 1) * pw] = hi

    for s in range(hid_refs[0].shape[1] // sub):
        cols = slice(s * sub, (s + 1) * sub)
        wg = wg_ref[:, cols].astype(BF16)
        wu = wu_ref[:, cols].astype(BF16)
        for xe_ref, hid_ref in zip(xe_refs, hid_refs):
            x = xe_ref[...]
            g = jnp.dot(x, wg, preferred_element_type=F32)
            u = jnp.dot(x, wu, preferred_element_type=F32)
            hid_ref[:, cols] = ((g * _sigmoid(g)) * u).astype(hid_ref.dtype)


def _ffn_down_kernel(*refs, groups, sub):
    hid_refs = refs[:groups]
    gw_refs = refs[groups:2 * groups]
    wd_ref = refs[2 * groups]
    ye_refs = refs[2 * groups + 1:]
    for s in range(ye_refs[0].shape[1] // sub):
        cols = slice(s * sub, (s + 1) * sub)
        wd = wd_ref[:, cols].astype(BF16)
        for hid_ref, gw_ref, ye_ref in zip(hid_refs, gw_refs, ye_refs):
            y = jnp.dot(hid_ref[...], wd, preferred_element_type=F32)
            ye_ref[:, cols] = (y * gw_ref[...]).astype(ye_ref.dtype)


def _ffn(xps, gws, wg, wu, wd, tf, tn):
    groups = len(xps)
    pieces, ne, _, pw = xps[0].shape
    caps = [xp.shape[2] for xp in xps]
    d, f = wg.shape[1], wg.shape[2]
    assert d == 2 * pieces * pw
    params = pltpu.CompilerParams(dimension_semantics=("parallel", "arbitrary"),
                                  vmem_limit_bytes=FFN_VMEM_LIMIT_BYTES)
    hids = pl.pallas_call(
        functools.partial(_ffn_up_kernel, groups=groups, sub=min(MXU_COLS, tf)),
        grid=(ne, f // tf),
        in_specs=[pl.BlockSpec((pieces, None, c, pw), lambda e, j: (0, e, 0, 0),
                               pipeline_mode=pl.Buffered(1)) for c in caps]
                 + [pl.BlockSpec((None, d, tf), lambda e, j: (e, 0, j)),
                    pl.BlockSpec((None, d, tf), lambda e, j: (e, 0, j))],
        out_specs=[pl.BlockSpec((None, c, tf), lambda e, j: (e, 0, j)) for c in caps],
        out_shape=[jax.ShapeDtypeStruct((ne, c, f), BF16) for c in caps],
        scratch_shapes=[pltpu.VMEM((c, d), BF16) for c in caps],
        compiler_params=params,
        name="expert_ffn_up",
    )(*xps, wg, wu)
    return pl.pallas_call(
        functools.partial(_ffn_down_kernel, groups=groups, sub=min(MXU_COLS, tn)),
        grid=(ne, d // tn),
        in_specs=[pl.BlockSpec((None, c, f), lambda e, j: (e, 0, 0)) for c in caps]
                 + [pl.BlockSpec((None, c, 1), lambda e, j: (e, 0, 0)) for c in caps]
                 + [pl.BlockSpec((None, f, tn), lambda e, j: (e, 0, j))],
        out_specs=[pl.BlockSpec((None, c, tn), lambda e, j: (e, 0, j)) for c in caps],
        out_shape=[jax.ShapeDtypeStruct((ne, c, d), BF16) for c in caps],
        compiler_params=params,
        name="expert_ffn_down",
    )(*hids, *gws, wd)


def _expert_column(ref, e):
    lane = lax.broadcasted_iota(jnp.int32, ref.shape, 1)
    return jnp.sum(jnp.where(lane == e, ref[...], 0), axis=1, keepdims=True)


def _combine_any_kernel(base_ref, slot_ref, x_ref, ya_ref, yb_ref, g_ref, o_ref, *, cap, nblk):
    b = pl.program_id(0)
    e = pl.program_id(1)
    ne = pl.num_programs(1)

    @pl.when(e == 0)
    def _():
        o_ref[...] = x_ref[...]

    tb = x_ref.shape[0]
    blk = ya_ref.shape[0]
    start = jnp.minimum(base_ref[e * (nblk + 1) + b] // blk, cap // blk - 1) * blk
    slot = _expert_column(slot_ref, e)
    cols = lax.broadcasted_iota(jnp.int32, (tb, blk), 1) + start
    oh_a = jnp.where(cols == slot, 1.0, 0.0).astype(BF16)
    oh_b = jnp.where(cols + blk == slot, 1.0, 0.0).astype(BF16)
    o_ref[...] += (jnp.dot(oh_a, ya_ref[...], preferred_element_type=F32)
                   + jnp.dot(oh_b, yb_ref[...], preferred_element_type=F32))

    @pl.when(e == ne - 1)
    def _():
        o_ref[...] = _rms_scale(o_ref[...]) * g_ref[...]


def _combine_any(base_flat, slot_tok, x2, ye, g_final, *, cap, tb):
    n, d = x2.shape
    ne = ye.shape[0]
    nblk = n // tb
    blk = tb
    last = cap // blk - 1

    def ya_map(b, e, base):
        return (e, jnp.minimum(base[e * (nblk + 1) + b] // blk, last), 0)

    def yb_map(b, e, base):
        return (e, jnp.minimum(jnp.minimum(base[e * (nblk + 1) + b] // blk, last) + 1, last), 0)

    kern = functools.partial(_combine_any_kernel, cap=cap, nblk=nblk)
    return pl.pallas_call(
        kern,
        grid_spec=pltpu.PrefetchScalarGridSpec(
            num_scalar_prefetch=1,
            grid=(nblk, ne),
            in_specs=[pl.BlockSpec((tb, ne), lambda b, e, base: (b, 0)),
                      pl.BlockSpec((tb, d), lambda b, e, base: (b, 0)),
                      pl.BlockSpec((None, blk, d), ya_map),
                      pl.BlockSpec((None, blk, d), yb_map),
                      pl.BlockSpec((1, d), lambda b, e, base: (0, 0))],
            out_specs=pl.BlockSpec((tb, d), lambda b, e, base: (b, 0)),
        ),
        out_shape=jax.ShapeDtypeStruct((n, d), F32),
        compiler_params=_params("parallel", "arbitrary"),
        name="expert_combine_any",
    )(base_flat, slot_tok, x2, ye, ye, g_final.reshape(1, d))


def _window_start(base, e, b, *, nblk, cap, win):
    tile = jnp.minimum(base[e * (nblk + 1) + b] // BF16_ROW_TILE, (cap - win) // BF16_ROW_TILE)
    return tile * BF16_ROW_TILE


def _combine_windows_kernel(base_ref, slot_ref, x_ref, g_ref, *rest, cap, nblk, win):
    y_refs, o_ref = rest[:-1], rest[-1]
    b = pl.program_id(0)
    s = pl.program_id(1)
    per = len(y_refs)

    @pl.when(s == 0)
    def _():
        o_ref[...] = x_ref[...]

    tb = x_ref.shape[0]
    col = lax.broadcasted_iota(jnp.int32, (tb, win), 1)
    onehots = []
    for k in range(per):
        e = s * per + k
        start = _window_start(base_ref, e, b, nblk=nblk, cap=cap, win=win)
        onehots.append(jnp.where(col + start == _expert_column(slot_ref, e), 1.0, 0.0).astype(BF16))
    y = jnp.concatenate([r[...] for r in y_refs], axis=0)
    o_ref[...] += jnp.dot(jnp.concatenate(onehots, axis=1), y, preferred_element_type=F32)

    @pl.when(s == pl.num_programs(1) - 1)
    def _():
        o_ref[...] = _rms_scale(o_ref[...]) * g_ref[...]


def _combine_windows(base_flat, slot_tok, x2, ye, g_final, *, cap, tb, win, experts_per_step):
    n, d = x2.shape
    ne = ye.shape[0]
    nblk = n // tb
    per = experts_per_step
    assert ne % per == 0

    def y_map(k):
        def index(b, s, base):
            e = s * per + k
            return (e, _window_start(base, e, b, nblk=nblk, cap=cap, win=win), 0)
        return index

    kern = functools.partial(_combine_windows_kernel, cap=cap, nblk=nblk, win=win)
    return pl.pallas_call(
        kern,
        grid_spec=pltpu.PrefetchScalarGridSpec(
            num_scalar_prefetch=1,
            grid=(nblk, ne // per),
            in_specs=[pl.BlockSpec((tb, ne), lambda b, s, base: (b, 0)),
                      pl.BlockSpec((tb, d), lambda b, s, base: (b, 0)),
                      pl.BlockSpec((1, d), lambda b, s, base: (0, 0))]
                     + [pl.BlockSpec((None, pl.Element(win), pl.Element(d)), y_map(k))
                        for k in range(per)],
            out_specs=pl.BlockSpec((tb, d), lambda b, s, base: (b, 0)),
        ),
        out_shape=jax.ShapeDtypeStruct((n, d), F32),
        compiler_params=_params("parallel", "arbitrary"),
        name="expert_combine_windows",
    )(base_flat, slot_tok, x2, g_final.reshape(1, d), *([ye] * per))


def _combine(base, slot_tok, x2, ye, g_final, *, cap, tb, win):
    nblk = x2.shape[0] // tb
    base_flat = base[:, :nblk + 1].reshape(-1)
    first = (base[:, :nblk] // BF16_ROW_TILE) * BF16_ROW_TILE
    fits = jnp.all(base[:, 1:nblk + 1] - jnp.minimum(first, cap - win) <= win)
    return lax.cond(
        fits,
        lambda: _combine_windows(base_flat, slot_tok, x2, ye, g_final, cap=cap, tb=tb, win=win,
                                 experts_per_step=8),
        lambda: _combine_any(base_flat, slot_tok, x2, ye, g_final, cap=cap, tb=tb))


def _mix_and_route(x3, mem3, p):
    bsz, seq, d = x3.shape
    n = bsz * seq
    n_mem = mem3.shape[1]
    x = x3.reshape(n, d)
    mem = mem3.reshape(bsz * n_mem, d)

    gmlp_w = p["gmlp_norm_g"].shape[0]
    conv_w = p["conv_w"].shape[1]
    xattn_w = p["w_mem_kv"].shape[1] // 2
    off_cb = 2 * gmlp_w
    off_g = off_cb + 3 * conv_w + xattn_w
    ne = p["w_router"].shape[1]
    cap = CAPACITY_FACTOR * n // ne

    def col_tile(cols):
        return next(t for t in (1024, 512, 256, 128) if cols % t == 0 and t <= cols)

    tm = min(1024, n)
    ts = min(256, seq)
    tb = 256 if cap >= 256 + BF16_ROW_TILE else 128

    h = _rmsnorm_cast(x, p["norm_mix_g"], ts)
    uv, mid, gates = _proj(h, p["w_in"], tm, col_tile(d), off_cb, off_g, BF16)
    m = _rmsnorm_cast(mem, p["mem_norm_g"], n_mem)
    kv = _proj_range(m, p["w_mem_kv"], 0, 2 * xattn_w, lambda x: x, n_mem, col_tile(2 * xattn_w), BF16)
    ycat = _branches(uv, mid, kv, p["w_spatial"], p["b_spatial"], p["gmlp_norm_g"].reshape(1, gmlp_w),
                     p["conv_w"], seq=seq, ts=ts, gmlp_w=gmlp_w, conv_w=conv_w, xattn_w=xattn_w,
                     n_mem=n_mem)
    merged = _merge(ycat, p["w_branch"], gates, tm=tm, tn=min(512, col_tile(d)), d=d,
                    gmlp_w=gmlp_w, conv_w=conv_w)
    x2 = _outproj(merged, p["w_out"], x, tm, col_tile(d))

    hp, afft = _router(x2, p["norm_ffn_g"], p["w_router"], ts, min(SC_ROW_PIECES, d // 256))
    slot, base = _select(afft, cap, tb)
    nblk = n // tb
    base_flat = base[:, :nblk + 1].reshape(-1)
    tok, gw = _pick_list(base_flat, slot.reshape(ne, 1, n), afft.reshape(ne, 1, n), cap=cap, tb=tb,
                         chunk=PICK_CHUNK_ROWS, experts_per_step=min(16, ne))
    xp = _gather(tok, hp, window=SC_GATHER_WINDOW)
    return {"xp": xp, "gw": gw, "x2": x2, "base": base, "slot_tok": slot.T, "cap": cap, "tb": tb}


def _combine_group(r, ye, g_final, shape):
    y = _combine(r["base"], r["slot_tok"], r["x2"], ye, g_final, cap=r["cap"], tb=r["tb"],
                 win=min(COMBINE_WINDOW_ROWS, r["tb"]))
    return y.reshape(shape)


def kernel(x_prompt, x_sample, mem_prompt, mem_sample, norm_mix_g, w_in, gmlp_norm_g, w_spatial,
           b_spatial, conv_w, mem_norm_g, w_mem_kv, w_branch_a, w_branch_b, w_branch_x, w_out,
           norm_ffn_g, w_router, w_gate, w_up, w_down, final_norm_g):
    assert w_in.shape[0] == 1, "one layer"
    groups, chunk = b_spatial.shape[1], b_spatial.shape[2]
    gc = gmlp_norm_g.shape[1] // groups
    p = {
        "norm_mix_g": norm_mix_g[0],
        "w_in": w_in[0].astype(BF16),
        "gmlp_norm_g": gmlp_norm_g[0],
        "w_spatial": w_spatial[0].astype(BF16),
        "b_spatial": jnp.broadcast_to(b_spatial[0][:, :, None], (groups, chunk, gc)),
        "conv_w": conv_w[0],
        "mem_norm_g": mem_norm_g[0],
        "w_mem_kv": w_mem_kv[0].astype(BF16),
        "w_branch": jnp.concatenate([w_branch_a[0], w_branch_b[0], w_branch_x[0]],
                                    axis=0).astype(BF16),
        "w_out": w_out[0].astype(BF16),
        "norm_ffn_g": norm_ffn_g[0],
        "w_router": w_router[0].astype(BF16),
        "w_gate": w_gate.reshape(w_gate.shape[1:]),
        "w_up": w_up.reshape(w_up.shape[1:]),
        "w_down": w_down.reshape(w_down.shape[1:]),
        "final_norm_g": final_norm_g,
    }
    routed = [_mix_and_route(x_prompt, mem_prompt, p), _mix_and_route(x_sample, mem_sample, p)]
    d = x_prompt.shape[-1]
    yes = _ffn([r["xp"] for r in routed], [r["gw"] for r in routed], p["w_gate"], p["w_up"],
               p["w_down"], min(FFN_UP_COLS, d), min(FFN_DOWN_COLS, d))
    return tuple(_combine_group(r, ye, final_norm_g, x.shape)
                 for r, ye, x in zip(routed, yes, (x_prompt, x_sample)))
```

```python
import functools

import jax
import jax.numpy as jnp
from jax import lax
from jax.experimental import pallas as pl
from jax.experimental.pallas import tpu as pltpu
from jax.experimental.pallas import tpu_sc as plsc

EPS = 1e-6
XATTN_HEADS = 4
CAPACITY_FACTOR = 2
BF16_ROW_TILE = 16
SUBLANES = 8
PICK_CHUNK_ROWS = 80
SC_ROW_PIECES = 8
SC_GATHER_WINDOW = 128
COMBINE_WINDOW_ROWS = 128
VMEM_LIMIT_BYTES = 56 * 1024 * 1024
FFN_VMEM_LIMIT_BYTES = 60 * 1024 * 1024
FFN_UP_COLS = 256
FFN_DOWN_COLS = 512

F32 = jnp.float32
BF16 = jnp.bfloat16


def _params(*semantics):
    return pltpu.CompilerParams(dimension_semantics=semantics,
                                vmem_limit_bytes=VMEM_LIMIT_BYTES)


GELU_K = 0.7978845608028654
GELU_CUBIC = 0.044715
MXU_COLS = 256


def _sigmoid(x):
    return 1.0 / (1.0 + jnp.exp(-x))


def _rms_scale(x):
    return x * lax.rsqrt(jnp.mean(x * x, axis=-1, keepdims=True) + EPS)


def _rmsnorm_cast_kernel(x_ref, g_ref, o_ref):
    o_ref[...] = (_rms_scale(x_ref[...]) * g_ref[...]).astype(o_ref.dtype)


def _rmsnorm_cast(x, g, tm):
    n, d = x.shape
    return pl.pallas_call(
        _rmsnorm_cast_kernel,
        grid=(n // tm,),
        in_specs=[pl.BlockSpec((tm, d), lambda i: (i, 0)),
                  pl.BlockSpec((1, d), lambda i: (0, 0))],
        out_specs=pl.BlockSpec((tm, d), lambda i: (i, 0)),
        out_shape=jax.ShapeDtypeStruct((n, d), BF16),
        compiler_params=_params("parallel"),
        name="rmsnorm_cast",
    )(x, g.reshape(1, d))


def _gelu_tanh(x):
    return (0.5 * x) * (1.0 + jnp.tanh(GELU_K * (x + GELU_CUBIC * (x * x * x))))


def _sigmoid_tanh(x):
    return 0.5 * (1.0 + jnp.tanh(0.5 * x))


def _proj_kernel(h_ref, w_ref, o_ref, *, sub, epilogue):
    h = h_ref[...]
    for s in range(o_ref.shape[1] // sub):
        cols = slice(s * sub, (s + 1) * sub)
        x = jnp.dot(h, w_ref[:, cols], preferred_element_type=F32)
        o_ref[:, cols] = epilogue(x).astype(o_ref.dtype)


def _proj_range(h, w, col_lo, col_hi, epilogue, tm, tn, out_dtype):
    n, d = h.shape
    tn = min(tn, col_hi - col_lo)
    assert col_lo % tn == 0 and col_hi % tn == 0
    first = col_lo // tn
    kern = functools.partial(_proj_kernel, sub=min(MXU_COLS, tn), epilogue=epilogue)
    return pl.pallas_call(
        kern,
        grid=(n // tm, (col_hi - col_lo) // tn),
        in_specs=[pl.BlockSpec((tm, d), lambda i, j: (i, 0)),
                  pl.BlockSpec((d, tn), lambda i, j: (0, first + j))],
        out_specs=pl.BlockSpec((tm, tn), lambda i, j: (i, j)),
        out_shape=jax.ShapeDtypeStruct((n, col_hi - col_lo), out_dtype),
        compiler_params=_params("parallel", "arbitrary"),
        name="in_proj",
    )(h, w)


def _proj(h, w, tm, tn, gelu_cols, sigmoid_from_col, out_dtype):
    cols = w.shape[1]
    return (_proj_range(h, w, 0, gelu_cols, _gelu_tanh, tm, tn, out_dtype),
            _proj_range(h, w, gelu_cols, sigmoid_from_col, lambda x: x, tm, tn, out_dtype),
            _proj_range(h, w, sigmoid_from_col, cols, _sigmoid_tanh, tm, tn, out_dtype))


def _branch_kernel(u_ref, v_ref, cb_ref, cc_ref, ch_ref, q_ref, ccp_ref, chp_ref, ccn_ref, chn_ref,
                   kv_ref, wsp_ref, bsp_ref, gg_ref, cw_ref, y_ref, *, seq, chunk, groups, heads):
    i = pl.program_id(0)
    ts = u_ref.shape[0]
    gw = u_ref.shape[1]
    gc = gw // groups
    cw = cb_ref.shape[1]
    xw = q_ref.shape[1]
    hd = xw // heads

    vn = (_rms_scale(v_ref[...].astype(F32)) * gg_ref[...]).astype(BF16)
    for c in range(ts // chunk):
        rows = slice(c * chunk, (c + 1) * chunk)
        for g in range(groups):
            cols = slice(g * gc, (g + 1) * gc)
            mixed = jnp.dot(wsp_ref[g], vn[rows, cols], preferred_element_type=F32) + bsp_ref[g]
            y_ref[rows, cols] = (u_ref[rows, cols].astype(F32) * mixed).astype(y_ref.dtype)

    pos = (i * ts) % seq
    z = cc_ref[...].astype(F32) * ch_ref[...].astype(F32)
    sub = ccp_ref.shape[0]
    z_before = jnp.where(pos == 0, 0.0, ccp_ref[sub - 1:sub, :].astype(F32)
                         * chp_ref[sub - 1:sub, :].astype(F32))
    z_after = jnp.where(pos + ts == seq, 0.0, ccn_ref[0:1, :].astype(F32)
                        * chn_ref[0:1, :].astype(F32))
    row = lax.broadcasted_iota(jnp.int32, z.shape, 0)
    z_prev = jnp.where(row == 0, z_before, pltpu.roll(z, 1, 0))
    z_next = jnp.where(row == ts - 1, z_after, pltpu.roll(z, ts - 1, 0))
    conv = z_prev * cw_ref[0:1, :] + z * cw_ref[1:2, :] + z_next * cw_ref[2:3, :]
    y_ref[:, gw:gw + cw] = (cb_ref[...].astype(F32) * conv).astype(y_ref.dtype)

    scale = hd ** -0.5
    for h in range(heads):
        cols = slice(h * hd, (h + 1) * hd)
        qh = q_ref[:, cols]
        kh = kv_ref[:, cols]
        vh = kv_ref[:, xw + h * hd: xw + (h + 1) * hd]
        s = lax.dot_general(qh, kh, (((1,), (1,)), ((), ())), preferred_element_type=F32) * scale
        e = jnp.exp(s - jnp.max(s, axis=-1, keepdims=True))
        p = (e / jnp.sum(e, axis=-1, keepdims=True)).astype(BF16)
        yh = jnp.dot(p, vh, preferred_element_type=F32)
        y_ref[:, gw + cw + h * hd: gw + cw + (h + 1) * hd] = yh.astype(y_ref.dtype)


def _branches(uv, mid, kv, wsp, bsp, gg, cw, *, seq, ts, gmlp_w, conv_w, xattn_w, n_mem):
    n = uv.shape[0]
    chunk = wsp.shape[1]
    groups = wsp.shape[0]
    sub = BF16_ROW_TILE
    cb_blk = 0
    q_blk = 3 * conv_w // xattn_w
    last = n // sub - 1
    tsb = ts // sub

    def prev_map(col):
        return lambda i: (jnp.maximum(i * tsb - 1, 0), col)

    def next_map(col):
        return lambda i: (jnp.minimum((i + 1) * tsb, last), col)

    kern = functools.partial(_branch_kernel, seq=seq, chunk=chunk, groups=groups, heads=XATTN_HEADS)
    return pl.pallas_call(
        kern,
        grid=(n // ts,),
        in_specs=[
            pl.BlockSpec((ts, gmlp_w), lambda i: (i, 0)),
            pl.BlockSpec((ts, gmlp_w), lambda i: (i, 1)),
            pl.BlockSpec((ts, conv_w), lambda i: (i, cb_blk)),
            pl.BlockSpec((ts, conv_w), lambda i: (i, cb_blk + 1)),
            pl.BlockSpec((ts, conv_w), lambda i: (i, cb_blk + 2)),
            pl.BlockSpec((ts, xattn_w), lambda i: (i, q_blk)),
            pl.BlockSpec((sub, conv_w), prev_map(cb_blk + 1)),
            pl.BlockSpec((sub, conv_w), prev_map(cb_blk + 2)),
            pl.BlockSpec((sub, conv_w), next_map(cb_blk + 1)),
            pl.BlockSpec((sub, conv_w), next_map(cb_blk + 2)),
            pl.BlockSpec((n_mem, 2 * xattn_w), lambda i: ((i * ts) // seq, 0)),
            pl.BlockSpec(wsp.shape, lambda i: (0, 0, 0)),
            pl.BlockSpec(bsp.shape, lambda i: (0, 0, 0)),
            pl.BlockSpec((1, gmlp_w), lambda i: (0, 0)),
            pl.BlockSpec(cw.shape, lambda i: (0, 0)),
        ],
        out_specs=pl.BlockSpec((ts, gmlp_w + conv_w + xattn_w), lambda i: (i, 0)),
        out_shape=jax.ShapeDtypeStruct((n, gmlp_w + conv_w + xattn_w), BF16),
        compiler_params=_params("parallel"),
        name="branches",
    )(uv, uv, mid, mid, mid, mid, mid, mid, mid, mid, kv, wsp, bsp, gg, cw)


def _merge_kernel(y_ref, w_ref, ga_ref, gb_ref, gx_ref, o_ref, *, gmlp_w, conv_w, sub):
    ya = y_ref[:, :gmlp_w]
    yb = y_ref[:, gmlp_w:gmlp_w + conv_w]
    yx = y_ref[:, gmlp_w + conv_w:]
    for s in range(o_ref.shape[1] // sub):
        cols = slice(s * sub, (s + 1) * sub)
        a = jnp.dot(ya, w_ref[:gmlp_w, cols], preferred_element_type=F32)
        b = jnp.dot(yb, w_ref[gmlp_w:gmlp_w + conv_w, cols], preferred_element_type=F32)
        x = jnp.dot(yx, w_ref[gmlp_w + conv_w:, cols], preferred_element_type=F32)
        o_ref[:, cols] = (ga_ref[:, cols].astype(F32) * a + gb_ref[:, cols].astype(F32) * b
                          + gx_ref[:, cols].astype(F32) * x).astype(o_ref.dtype)


def _merge(ycat, w_br, gates, *, tm, tn, d, gmlp_w, conv_w):
    n, k = ycat.shape
    g0 = 0
    gstep = d // tn
    kern = functools.partial(_merge_kernel, gmlp_w=gmlp_w, conv_w=conv_w, sub=min(MXU_COLS, tn))
    return pl.pallas_call(
        kern,
        grid=(n // tm, d // tn),
        in_specs=[pl.BlockSpec((tm, k), lambda i, j: (i, 0)),
                  pl.BlockSpec((k, tn), lambda i, j: (0, j)),
                  pl.BlockSpec((tm, tn), lambda i, j: (i, g0 + j)),
                  pl.BlockSpec((tm, tn), lambda i, j: (i, g0 + gstep + j)),
                  pl.BlockSpec((tm, tn), lambda i, j: (i, g0 + 2 * gstep + j))],
        out_specs=pl.BlockSpec((tm, tn), lambda i, j: (i, j)),
        out_shape=jax.ShapeDtypeStruct((n, d), BF16),
        compiler_params=_params("parallel", "arbitrary"),
        name="merge",
    )(ycat, w_br, gates, gates, gates)


def _outproj_kernel(m_ref, w_ref, x_ref, o_ref, *, sub):
    m = m_ref[...]
    for s in range(o_ref.shape[1] // sub):
        cols = slice(s * sub, (s + 1) * sub)
        o_ref[:, cols] = x_ref[:, cols] + jnp.dot(m, w_ref[:, cols], preferred_element_type=F32)


def _outproj(merged, w_out, x, tm, tn):
    n, d = x.shape
    return pl.pallas_call(
        functools.partial(_outproj_kernel, sub=min(MXU_COLS, tn)),
        grid=(n // tm, d // tn),
        in_specs=[pl.BlockSpec((tm, d), lambda i, j: (i, 0)),
                  pl.BlockSpec((d, tn), lambda i, j: (0, j)),
                  pl.BlockSpec((tm, tn), lambda i, j: (i, j))],
        out_specs=pl.BlockSpec((tm, tn), lambda i, j: (i, j)),
        out_shape=jax.ShapeDtypeStruct((n, d), F32),
        compiler_params=_params("parallel", "arbitrary"),
        name="out_proj",
    )(merged, w_out, x)


def _pack_bf16_pair(lo, hi):
    lo_bits = pltpu.bitcast(lo.astype(BF16).astype(F32), jnp.uint32)
    hi_bits = pltpu.bitcast(hi.astype(BF16).astype(F32), jnp.uint32)
    return lax.shift_right_logical(lo_bits, jnp.uint32(16)) | (hi_bits & jnp.uint32(0xFFFF0000))


def _unpack_bf16_pair(words):
    lo = pltpu.bitcast(lax.shift_left(words, jnp.uint32(16)), F32)
    hi = pltpu.bitcast(words & jnp.uint32(0xFFFF0000), F32)
    return lo.astype(BF16), hi.astype(BF16)


def _router_kernel(x_ref, g_ref, wrt_ref, hp_ref, afft_ref):
    hf = _rms_scale(x_ref[...]) * g_ref[...]
    half = hf.shape[1] // 2
    words = _pack_bf16_pair(hf[:, :half], hf[:, half:])
    pw = hp_ref.shape[2]
    for c in range(hp_ref.shape[0]):
        hp_ref[c] = words[:, c * pw:(c + 1) * pw]
    lt = lax.dot_general(wrt_ref[...], hf.astype(BF16), (((1,), (1,)), ((), ())),
                         preferred_element_type=F32)
    et = jnp.exp(lt - jnp.max(lt, axis=0, keepdims=True))
    afft_ref[...] = et / jnp.sum(et, axis=0, keepdims=True)


def _router(x2, g, wr, tm, pieces):
    n, d = x2.shape
    ne = wr.shape[1]
    pw = d // 2 // pieces
    return pl.pallas_call(
        _router_kernel,
        grid=(n // tm,),
        in_specs=[pl.BlockSpec((tm, d), lambda i: (i, 0)),
                  pl.BlockSpec((1, d), lambda i: (0, 0)),
                  pl.BlockSpec((ne, d), lambda i: (0, 0))],
        out_specs=[pl.BlockSpec((pieces, tm, pw), lambda i: (0, i, 0)),
                   pl.BlockSpec((ne, tm), lambda i: (0, i))],
        out_shape=[jax.ShapeDtypeStruct((pieces, n, pw), jnp.uint32),
                   jax.ShapeDtypeStruct((ne, n), F32)],
        compiler_params=_params("parallel"),
        name="ffn_norm_router",
    )(x2, g.reshape(1, d), wr.T)


def _select_kernel(aff_ref, slot_ref, base_ref, *, cap, tb):
    ne, n = aff_ref.shape
    bits = pltpu.bitcast(aff_ref[...], jnp.int32)
    capf = jnp.float32(cap)

    def search(k, t):
        cand = t | lax.shift_left(jnp.int32(1), 30 - k)
        cnt = jnp.sum(jnp.where(bits >= cand, 1.0, 0.0), axis=1, keepdims=True)
        return jnp.where(cnt >= capf, cand, t)

    thr = lax.fori_loop(0, 31, search, jnp.zeros((ne, 1), jnp.int32))
    n_gt = jnp.sum(jnp.where(bits > thr, 1.0, 0.0), axis=1, keepdims=True)
    need = capf - n_gt

    r = lax.broadcasted_iota(jnp.int32, (tb, tb), 0)
    c = lax.broadcasted_iota(jnp.int32, (tb, tb), 1)
    tri = jnp.where(r < c, 1.0, 0.0).astype(BF16)
    lane = lax.broadcasted_iota(jnp.int32, base_ref.shape, 1)
    eq_seen = jnp.zeros((ne, 1), F32)
    sel_seen = jnp.zeros((ne, 1), F32)
    base = jnp.zeros(base_ref.shape, F32)
    for b in range(n // tb):
        blk = pltpu.bitcast(aff_ref[:, b * tb:(b + 1) * tb], jnp.int32)
        eq = jnp.where(blk == thr, 1.0, 0.0)
        eq_rank = jnp.dot(eq.astype(BF16), tri, preferred_element_type=F32) + eq_seen
        sel = jnp.where(blk > thr, 1.0, jnp.where(eq_rank < need, eq, 0.0))
        pos = jnp.dot(sel.astype(BF16), tri, preferred_element_type=F32) + sel_seen
        slot_ref[:, b * tb:(b + 1) * tb] = jnp.where(sel > 0.0, pos, -1.0).astype(jnp.int32)
        base = jnp.where(lane == b, sel_seen, base)
        eq_seen = eq_seen + jnp.sum(eq, axis=1, keepdims=True)
        sel_seen = sel_seen + jnp.sum(sel, axis=1, keepdims=True)
    base = jnp.where(lane == n // tb, sel_seen, base)
    base_ref[...] = base.astype(jnp.int32)


def _select(afft, cap, tb):
    ne, n = afft.shape
    lanes = 128
    assert n // tb < lanes
    kern = functools.partial(_select_kernel, cap=cap, tb=tb)
    return pl.pallas_call(
        kern,
        out_shape=[jax.ShapeDtypeStruct((ne, n), jnp.int32),
                   jax.ShapeDtypeStruct((ne, lanes), jnp.int32)],
        compiler_params=pltpu.CompilerParams(vmem_limit_bytes=VMEM_LIMIT_BYTES),
        name="expert_choice_select",
    )(afft)


def _pick_list_kernel(base_ref, slot_ref, aff_ref, tok_ref, gw_ref, *, cap, nblk, chunk):
    eb = pl.program_id(0)
    b = pl.program_id(1)

    @pl.when(b == 0)
    def _():
        tok_ref[...] = jnp.zeros_like(tok_ref)
        gw_ref[...] = jnp.zeros_like(gw_ref)

    tb = slot_ref.shape[2]
    token = (lax.broadcasted_iota(jnp.int32, (1, tb), 1) + b * tb).astype(F32)
    for k in range(tok_ref.shape[0]):
        e = eb * tok_ref.shape[0] + k
        first = (base_ref[e * (nblk + 1) + b] // SUBLANES) * SUBLANES
        end = base_ref[e * (nblk + 1) + b + 1]
        slot = slot_ref[k]
        aff = aff_ref[k]

        def body(c, carry, k=k, first=first, slot=slot, aff=aff):
            lo = first + c * chunk
            st = pl.multiple_of(jnp.minimum(lo, cap - chunk), SUBLANES)
            rows = lax.broadcasted_iota(jnp.int32, (chunk, tb), 0) + st
            hit = jnp.where(rows == slot, jnp.where(rows >= lo, 1.0, 0.0), 0.0)
            tok_ref[k, pl.ds(st, chunk), :] += jnp.sum(hit * token, axis=1,
                                                       keepdims=True).astype(jnp.int32)
            gw_ref[k, pl.ds(st, chunk), :] += jnp.sum(hit * aff, axis=1, keepdims=True)
            return carry

        lax.fori_loop(0, (end - first + chunk - 1) // chunk, body, 0)


def _pick_list(base_flat, slot3, aff3, *, cap, tb, chunk, experts_per_step):
    ne, _, n = slot3.shape
    nblk = n // tb
    eps = experts_per_step
    assert cap >= chunk and (cap - chunk) % SUBLANES == 0 and chunk % SUBLANES == 0
    kern = functools.partial(_pick_list_kernel, cap=cap, nblk=nblk, chunk=chunk)
    return pl.pallas_call(
        kern,
        grid_spec=pltpu.PrefetchScalarGridSpec(
            num_scalar_prefetch=1,
            grid=(ne // eps, nblk),
            in_specs=[pl.BlockSpec((eps, 1, tb), lambda e, b, base: (e, 0, b)),
                      pl.BlockSpec((eps, 1, tb), lambda e, b, base: (e, 0, b))],
            out_specs=[pl.BlockSpec((eps, cap, 1), lambda e, b, base: (e, 0, 0)),
                       pl.BlockSpec((eps, cap, 1), lambda e, b, base: (e, 0, 0))],
        ),
        out_shape=[jax.ShapeDtypeStruct((ne, cap, 1), jnp.int32),
                   jax.ShapeDtypeStruct((ne, cap, 1), F32)],
        compiler_params=_params("parallel", "arbitrary"),
        name="expert_pick_list",
    )(base_flat, slot3, aff3)


def _row_gather_sparsecore(table, rows, *, window):
    num, width = rows.shape[0], table.shape[1]
    mesh = plsc.VectorSubcoreMesh(core_axis_name="c", subcore_axis_name="s")

    @pl.kernel(out_type=jax.ShapeDtypeStruct((num, width), table.dtype), mesh=mesh,
               name="expert_row_gather")
    def gather(table_hbm, rows_hbm, out_hbm):
        def body(rows_vmem, out_vmem):
            pltpu.sync_copy(table_hbm.at[rows_vmem.at[0]], out_vmem)

        pltpu.emit_pipeline(
            body,
            grid=(num // window,),
            in_specs=[pl.BlockSpec((1, window), index_map=lambda i: (0, i))],
            out_specs=[pl.BlockSpec((window, width), index_map=lambda i: (i, 0))],
            core_axis_name=("c", "s"),
            dimension_semantics=(pltpu.PARALLEL,),
        )(rows_hbm, out_hbm)

    return gather(table, rows.reshape(1, num))


def _gather(tok, hp, *, window):
    ne, cap, _ = tok.shape
    pieces, n, pw = hp.shape
    rows = (jnp.arange(pieces, dtype=jnp.int32)[:, None] * n + tok.reshape(1, ne * cap)).reshape(-1)
    out = _row_gather_sparsecore(hp.reshape(pieces * n, pw), rows, window=window)
    return out.reshape(pieces, ne, cap, pw)


def _ffn_up_kernel(*refs, groups, sub):
    xp_refs = refs[:groups]
    wg_ref, wu_ref = refs[groups:groups + 2]
    hid_refs = refs[groups + 2:2 * groups + 2]
    xe_refs = refs[2 * groups + 2:]

    @pl.when(pl.program_id(1) == 0)
    def _():
        for xp_ref, xe_ref in zip(xp_refs, xe_refs):
            pieces, _, pw = xp_ref.shape
            for c in range(pieces):
                lo, hi = _unpack_bf16_pair(xp_ref[c])
                xe_ref[:, c * pw:(c + 1) * pw] = lo
                xe_ref[:, (pieces + c) * pw:(pieces + c + 1) * pw] = hi

    for s in range(hid_refs[0].shape[1] // sub):
        cols = slice(s * sub, (s + 1) * sub)
        wg = wg_ref[:, cols].astype(BF16)
        wu = wu_ref[:, cols].astype(BF16)
        for xe_ref, hid_ref in zip(xe_refs, hid_refs):
            x = xe_ref[...]
            g = jnp.dot(x, wg, preferred_element_type=F32)
            u = jnp.dot(x, wu, preferred_element_type=F32)
            hid_ref[:, cols] = ((g * _sigmoid(g)) * u).astype(hid_ref.dtype)


def _ffn_down_kernel(*refs, groups, sub):
    hid_refs = refs[:groups]
    gw_refs = refs[groups:2 * groups]
    wd_ref = refs[2 * groups]
    ye_refs = refs[2 * groups + 1:]
    for s in range(ye_refs[0].shape[1] // sub):
        cols = slice(s * sub, (s + 1) * sub)
        wd = wd_ref[:, cols].astype(BF16)
        for hid_ref, gw_ref, ye_ref in zip(hid_refs, gw_refs, ye_refs):
            y = jnp.dot(hid_ref[...], wd, preferred_element_type=F32)
            ye_ref[:, cols] = (y * gw_ref[...]).astype(ye_ref.dtype)


def _ffn(xps, gws, wg, wu, wd, tf, tn):
    groups = len(xps)
    pieces, ne, _, pw = xps[0].shape
    caps = [xp.shape[2] for xp in xps]
    d, f = wg.shape[1], wg.shape[2]
    assert d == 2 * pieces * pw
    params = pltpu.CompilerParams(dimension_semantics=("parallel", "arbitrary"),
                                  vmem_limit_bytes=FFN_VMEM_LIMIT_BYTES)
    hids = pl.pallas_call(
        functools.partial(_ffn_up_kernel, groups=groups, sub=min(MXU_COLS, tf)),
        grid=(ne, f // tf),
        in_specs=[pl.BlockSpec((pieces, None, c, pw), lambda e, j: (0, e, 0, 0),
                               pipeline_mode=pl.Buffered(1)) for c in caps]
                 + [pl.BlockSpec((None, d, tf), lambda e, j: (e, 0, j)),
                    pl.BlockSpec((None, d, tf), lambda e, j: (e, 0, j))],
        out_specs=[pl.BlockSpec((None, c, tf), lambda e, j: (e, 0, j)) for c in caps],
        out_shape=[jax.ShapeDtypeStruct((ne, c, f), BF16) for c in caps],
        scratch_shapes=[pltpu.VMEM((c, d), BF16) for c in caps],
        compiler_params=params,
        name="expert_ffn_up",
    )(*xps, wg, wu)
    return pl.pallas_call(
        functools.partial(_ffn_down_kernel, groups=groups, sub=min(MXU_COLS, tn)),
        grid=(ne, d // tn),
        in_specs=[pl.BlockSpec((None, c, f), lambda e, j: (e, 0, 0)) for c in caps]
                 + [pl.BlockSpec((None, c, 1), lambda e, j: (e, 0, 0)) for c in caps]
                 + [pl.BlockSpec((None, f, tn), lambda e, j: (e, 0, j))],
        out_specs=[pl.BlockSpec((None, c, tn), lambda e, j: (e, 0, j)) for c in caps],
        out_shape=[jax.ShapeDtypeStruct((ne, c, d), BF16) for c in caps],
        compiler_params=params,
        name="expert_ffn_down",
    )(*hids, *gws, wd)


def _expert_column(ref, e):
    lane = lax.broadcasted_iota(jnp.int32, ref.shape, 1)
    return jnp.sum(jnp.where(lane == e, ref[...], 0), axis=1, keepdims=True)


def _combine_any_kernel(base_ref, slot_ref, x_ref, ya_ref, yb_ref, g_ref, o_ref, *, cap, nblk):
    b = pl.program_id(0)
    e = pl.program_id(1)
    ne = pl.num_programs(1)

    @pl.when(e == 0)
    def _():
        o_ref[...] = x_ref[...]

    tb = x_ref.shape[0]
    blk = ya_ref.shape[0]
    start = jnp.minimum(base_ref[e * (nblk + 1) + b] // blk, cap // blk - 1) * blk
    slot = _expert_column(slot_ref, e)
    cols = lax.broadcasted_iota(jnp.int32, (tb, blk), 1) + start
    oh_a = jnp.where(cols == slot, 1.0, 0.0).astype(BF16)
    oh_b = jnp.where(cols + blk == slot, 1.0, 0.0).astype(BF16)
    o_ref[...] += (jnp.dot(oh_a, ya_ref[...], preferred_element_type=F32)
                   + jnp.dot(oh_b, yb_ref[...], preferred_element_type=F32))

    @pl.when(e == ne - 1)
    def _():
        o_ref[...] = _rms_scale(o_ref[...]) * g_ref[...]


def _combine_any(base_flat, slot_tok, x2, ye, g_final, *, cap, tb):
    n, d = x2.shape
    ne = ye.shape[0]
    nblk = n // tb
    blk = tb
    last = cap // blk - 1

    def ya_map(b, e, base):
        return (e, jnp.minimum(base[e * (nblk + 1) + b] // blk, last), 0)

    def yb_map(b, e, base):
        return (e, jnp.minimum(jnp.minimum(base[e * (nblk + 1) + b] // blk, last) + 1, last), 0)

    kern = functools.partial(_combine_any_kernel, cap=cap, nblk=nblk)
    return pl.pallas_call(
        kern,
        grid_spec=pltpu.PrefetchScalarGridSpec(
            num_scalar_prefetch=1,
            grid=(nblk, ne),
            in_specs=[pl.BlockSpec((tb, ne), lambda b, e, base: (b, 0)),
                      pl.BlockSpec((tb, d), lambda b, e, base: (b, 0)),
                      pl.BlockSpec((None, blk, d), ya_map),
                      pl.BlockSpec((None, blk, d), yb_map),
                      pl.BlockSpec((1, d), lambda b, e, base: (0, 0))],
            out_specs=pl.BlockSpec((tb, d), lambda b, e, base: (b, 0)),
        ),
        out_shape=jax.ShapeDtypeStruct((n, d), F32),
        compiler_params=_params("parallel", "arbitrary"),
        name="expert_combine_any",
    )(base_flat, slot_tok, x2, ye, ye, g_final.reshape(1, d))


def _window_start(base, e, b, *, nblk, cap, win):
    tile = jnp.minimum(base[e * (nblk + 1) + b] // BF16_ROW_TILE, (cap - win) // BF16_ROW_TILE)
    return tile * BF16_ROW_TILE


def _combine_windows_kernel(base_ref, slot_ref, x_ref, g_ref, *rest, cap, nblk, win):
    y_refs, o_ref = rest[:-1], rest[-1]
    b = pl.program_id(0)
    s = pl.program_id(1)
    per = len(y_refs)

    @pl.when(s == 0)
    def _():
        o_ref[...] = x_ref[...]

    tb = x_ref.shape[0]
    col = lax.broadcasted_iota(jnp.int32, (tb, win), 1)
    onehots = []
    for k in range(per):
        e = s * per + k
        start = _window_start(base_ref, e, b, nblk=nblk, cap=cap, win=win)
        onehots.append(jnp.where(col + start == _expert_column(slot_ref, e), 1.0, 0.0).astype(BF16))
    y = jnp.concatenate([r[...] for r in y_refs], axis=0)
    o_ref[...] += jnp.dot(jnp.concatenate(onehots, axis=1), y, preferred_element_type=F32)

    @pl.when(s == pl.num_programs(1) - 1)
    def _():
        o_ref[...] = _rms_scale(o_ref[...]) * g_ref[...]


def _combine_windows(base_flat, slot_tok, x2, ye, g_final, *, cap, tb, win, experts_per_step):
    n, d = x2.shape
    ne = ye.shape[0]
    nblk = n // tb
    per = experts_per_step
    assert ne % per == 0

    def y_map(k):
        def index(b, s, base):
            e = s * per + k
            return (e, _window_start(base, e, b, nblk=nblk, cap=cap, win=win), 0)
        return index

    kern = functools.partial(_combine_windows_kernel, cap=cap, nblk=nblk, win=win)
    return pl.pallas_call(
        kern,
        grid_spec=pltpu.PrefetchScalarGridSpec(
            num_scalar_prefetch=1,
            grid=(nblk, ne // per),
            in_specs=[pl.BlockSpec((tb, ne), lambda b, s, base: (b, 0)),
                      pl.BlockSpec((tb, d), lambda b, s, base: (b, 0)),
                      pl.BlockSpec((1, d), lambda b, s, base: (0, 0))]
                     + [pl.BlockSpec((None, pl.Element(win), pl.Element(d)), y_map(k))
                        for k in range(per)],
            out_specs=pl.BlockSpec((tb, d), lambda b, s, base: (b, 0)),
        ),
        out_shape=jax.ShapeDtypeStruct((n, d), F32),
        compiler_params=_params("parallel", "arbitrary"),
        name="expert_combine_windows",
    )(base_flat, slot_tok, x2, g_final.reshape(1, d), *([ye] * per))


def _combine(base, slot_tok, x2, ye, g_final, *, cap, tb, win):
    nblk = x2.shape[0] // tb
    base_flat = base[:, :nblk + 1].reshape(-1)
    first = (base[:, :nblk] // BF16_ROW_TILE) * BF16_ROW_TILE
    fits = jnp.all(base[:, 1:nblk + 1] - jnp.minimum(first, cap - win) <= win)
    return lax.cond(
        fits,
        lambda: _combine_windows(base_flat, slot_tok, x2, ye, g_final, cap=cap, tb=tb, win=win,
                                 experts_per_step=8),
        lambda: _combine_any(base_flat, slot_tok, x2, ye, g_final, cap=cap, tb=tb))


def _mix_and_route(x3, mem3, p):
    bsz, seq, d = x3.shape
    n = bsz * seq
    n_mem = mem3.shape[1]
    x = x3.reshape(n, d)
    mem = mem3.reshape(bsz * n_mem, d)

    gmlp_w = p["gmlp_norm_g"].shape[0]
    conv_w = p["conv_w"].shape[1]
    xattn_w = p["w_mem_kv"].shape[1] // 2
    off_cb = 2 * gmlp_w
    off_g = off_cb + 3 * conv_w + xattn_w
    ne = p["w_router"].shape[1]
    cap = CAPACITY_FACTOR * n // ne

    def col_tile(cols):
        return next(t for t in (1024, 512, 256, 128) if cols % t == 0 and t <= cols)

    tm = min(1024, n)
    ts = min(512, seq)
    tb = 256 if cap >= 256 + BF16_ROW_TILE else 128

    h = _rmsnorm_cast(x, p["norm_mix_g"], ts)
    uv, mid, gates = _proj(h, p["w_in"], tm, col_tile(d), off_cb, off_g, BF16)
    m = _rmsnorm_cast(mem, p["mem_norm_g"], n_mem)
    kv = _proj_range(m, p["w_mem_kv"], 0, 2 * xattn_w, lambda x: x, n_mem, col_tile(2 * xattn_w), BF16)
    ycat = _branches(uv, mid, kv, p["w_spatial"], p["b_spatial"], p["gmlp_norm_g"].reshape(1, gmlp_w),
                     p["conv_w"], seq=seq, ts=ts, gmlp_w=gmlp_w, conv_w=conv_w, xattn_w=xattn_w,
                     n_mem=n_mem)
    merged = _merge(ycat, p["w_branch"], gates, tm=tm, tn=min(512, col_tile(d)), d=d,
                    gmlp_w=gmlp_w, conv_w=conv_w)
    x2 = _outproj(merged, p["w_out"], x, tm, col_tile(d))

    hp, afft = _router(x2, p["norm_ffn_g"], p["w_router"], ts, min(SC_ROW_PIECES, d // 256))
    slot, base = _select(afft, cap, tb)
    nblk = n // tb
    base_flat = base[:, :nblk + 1].reshape(-1)
    tok, gw = _pick_list(base_flat, slot.reshape(ne, 1, n), afft.reshape(ne, 1, n), cap=cap, tb=tb,
                         chunk=PICK_CHUNK_ROWS, experts_per_step=min(16, ne))
    xp = _gather(tok, hp, window=SC_GATHER_WINDOW)
    return {"xp": xp, "gw": gw, "x2": x2, "base": base, "slot_tok": slot.T, "cap": cap, "tb": tb}


def _combine_group(r, ye, g_final, shape):
    y = _combine(r["base"], r["slot_tok"], r["x2"], ye, g_final, cap=r["cap"], tb=r["tb"],
                 win=min(COMBINE_WINDOW_ROWS, r["tb"]))
    return y.reshape(shape)


def kernel(x_prompt, x_sample, mem_prompt, mem_sample, norm_mix_g, w_in, gmlp_norm_g, w_spatial,
           b_spatial, conv_w, mem_norm_g, w_mem_kv, w_branch_a, w_branch_b, w_branch_x, w_out,
           norm_ffn_g, w_router, w_gate, w_up, w_down, final_norm_g):
    assert w_in.shape[0] == 1, "one layer"
    groups, chunk = b_spatial.shape[1], b_spatial.shape[2]
    gc = gmlp_norm_g.shape[1] // groups
    p = {
        "norm_mix_g": norm_mix_g[0],
        "w_in": w_in[0].astype(BF16),
        "gmlp_norm_g": gmlp_norm_g[0],
        "w_spatial": w_spatial[0].astype(BF16),
        "b_spatial": jnp.broadcast_to(b_spatial[0][:, :, None], (groups, chunk, gc)),
        "conv_w": conv_w[0],
        "mem_norm_g": mem_norm_g[0],
        "w_mem_kv": w_mem_kv[0].astype(BF16),
        "w_branch": jnp.concatenate([w_branch_a[0], w_branch_b[0], w_branch_x[0]],
                                    axis=0).astype(BF16),
        "w_out": w_out[0].astype(BF16),
        "norm_ffn_g": norm_ffn_g[0],
        "w_router": w_router[0].astype(BF16),
        "w_gate": w_gate.reshape(w_gate.shape[1:]),
        "w_up": w_up.reshape(w_up.shape[1:]),
        "w_down": w_down.reshape(w_down.shape[1:]),
        "final_norm_g": final_norm_g,
    }
    routed = [_mix_and_route(x_prompt, mem_prompt, p), _mix_and_route(x_sample, mem_sample, p)]
    d = x_prompt.shape[-1]
    yes = _ffn([r["xp"] for r in routed], [r["gw"] for r in routed], p["w_gate"], p["w_up"],
               p["w_down"], min(FFN_UP_COLS, d), min(FFN_DOWN_COLS, d))
    return tuple(_combine_group(r, ye, final_norm_g, x.shape)
                 for r, ye, x in zip(routed, yes, (x_prompt, x_sample)))
```

```python
import functools

import jax
import jax.numpy as jnp
from jax import lax
from jax.experimental import pallas as pl
from jax.experimental.pallas import tpu as pltpu
from jax.experimental.pallas import tpu_sc as plsc

EPS = 1e-6
XATTN_HEADS = 4
CAPACITY_FACTOR = 2
BF16_ROW_TILE = 16
SUBLANES = 8
PICK_CHUNK_ROWS = 48
SC_ROW_PIECES = 8
SC_GATHER_WINDOW = 128
COMBINE_WINDOW_ROWS = 128
VMEM_LIMIT_BYTES = 56 * 1024 * 1024
FFN_VMEM_LIMIT_BYTES = 60 * 1024 * 1024
FFN_UP_COLS = 256
FFN_DOWN_COLS = 512

F32 = jnp.float32
BF16 = jnp.bfloat16


def _params(*semantics):
    return pltpu.CompilerParams(dimension_semantics=semantics,
                                vmem_limit_bytes=VMEM_LIMIT_BYTES)


GELU_K = 0.7978845608028654
GELU_CUBIC = 0.044715
MXU_COLS = 256


def _sigmoid(x):
    return 1.0 / (1.0 + jnp.exp(-x))


def _rms_scale(x):
    return x * lax.rsqrt(jnp.mean(x * x, axis=-1, keepdims=True) + EPS)


def _rmsnorm_cast_kernel(x_ref, g_ref, o_ref):
    o_ref[...] = (_rms_scale(x_ref[...]) * g_ref[...]).astype(o_ref.dtype)


def _rmsnorm_cast(x, g, tm):
    n, d = x.shape
    return pl.pallas_call(
        _rmsnorm_cast_kernel,
        grid=(n // tm,),
        in_specs=[pl.BlockSpec((tm, d), lambda i: (i, 0)),
                  pl.BlockSpec((1, d), lambda i: (0, 0))],
        out_specs=pl.BlockSpec((tm, d), lambda i: (i, 0)),
        out_shape=jax.ShapeDtypeStruct((n, d), BF16),
        compiler_params=_params("parallel"),
        name="rmsnorm_cast",
    )(x, g.reshape(1, d))


def _gelu_tanh(x):
    return (0.5 * x) * (1.0 + jnp.tanh(x * (GELU_K + (GELU_K * GELU_CUBIC) * (x * x))))


def _sigmoid_tanh(x):
    return 0.5 * (1.0 + jnp.tanh(0.5 * x))


def _proj_kernel(h_ref, w_ref, o_ref, *, sub, epilogue):
    h = h_ref[...]
    for s in range(o_ref.shape[1] // sub):
        cols = slice(s * sub, (s + 1) * sub)
        x = jnp.dot(h, w_ref[:, cols], preferred_element_type=F32)
        o_ref[:, cols] = epilogue(x).astype(o_ref.dtype)


def _proj_range(h, w, col_lo, col_hi, epilogue, tm, tn, out_dtype):
    n, d = h.shape
    tn = min(tn, col_hi - col_lo)
    assert col_lo % tn == 0 and col_hi % tn == 0
    first = col_lo // tn
    kern = functools.partial(_proj_kernel, sub=min(MXU_COLS, tn), epilogue=epilogue)
    return pl.pallas_call(
        kern,
        grid=(n // tm, (col_hi - col_lo) // tn),
        in_specs=[pl.BlockSpec((tm, d), lambda i, j: (i, 0)),
                  pl.BlockSpec((d, tn), lambda i, j: (0, first + j))],
        out_specs=pl.BlockSpec((tm, tn), lambda i, j: (i, j)),
        out_shape=jax.ShapeDtypeStruct((n, col_hi - col_lo), out_dtype),
        compiler_params=_params("parallel", "arbitrary"),
        name="in_proj",
    )(h, w)


def _proj(h, w, tm, tn, gelu_cols, sigmoid_from_col, out_dtype):
    cols = w.shape[1]
    return (_proj_range(h, w, 0, gelu_cols, _gelu_tanh, tm, tn, out_dtype),
            _proj_range(h, w, gelu_cols, sigmoid_from_col, lambda x: x, tm, tn, out_dtype),
            _proj_range(h, w, sigmoid_from_col, cols, _sigmoid_tanh, tm, tn, out_dtype))


def _branch_kernel(u_ref, v_ref, cb_ref, cc_ref, ch_ref, q_ref, ccp_ref, chp_ref, ccn_ref, chn_ref,
                   kv_ref, wsp_ref, bsp_ref, gg_ref, cw_ref, y_ref, *, seq, chunk, groups, heads):
    i = pl.program_id(0)
    ts = u_ref.shape[0]
    gw = u_ref.shape[1]
    gc = gw // groups
    cw = cb_ref.shape[1]
    xw = q_ref.shape[1]
    hd = xw // heads

    vn = (_rms_scale(v_ref[...].astype(F32)) * gg_ref[...]).astype(BF16)
    for c in range(ts // chunk):
        rows = slice(c * chunk, (c + 1) * chunk)
        for g in range(groups):
            cols = slice(g * gc, (g + 1) * gc)
            mixed = jnp.dot(wsp_ref[g], vn[rows, cols], preferred_element_type=F32) + bsp_ref[g]
            y_ref[rows, cols] = (u_ref[rows, cols].astype(F32) * mixed).astype(y_ref.dtype)

    pos = (i * ts) % seq
    z = cc_ref[...].astype(F32) * ch_ref[...].astype(F32)
    sub = ccp_ref.shape[0]
    z_before = jnp.where(pos == 0, 0.0, ccp_ref[sub - 1:sub, :].astype(F32)
                         * chp_ref[sub - 1:sub, :].astype(F32))
    z_after = jnp.where(pos + ts == seq, 0.0, ccn_ref[0:1, :].astype(F32)
                        * chn_ref[0:1, :].astype(F32))
    row = lax.broadcasted_iota(jnp.int32, z.shape, 0)
    z_prev = jnp.where(row == 0, z_before, pltpu.roll(z, 1, 0))
    z_next = jnp.where(row == ts - 1, z_after, pltpu.roll(z, ts - 1, 0))
    conv = z_prev * cw_ref[0:1, :] + z * cw_ref[1:2, :] + z_next * cw_ref[2:3, :]
    y_ref[:, gw:gw + cw] = (cb_ref[...].astype(F32) * conv).astype(y_ref.dtype)

    scale = hd ** -0.5
    for h in range(heads):
        cols = slice(h * hd, (h + 1) * hd)
        qh = q_ref[:, cols]
        kh = kv_ref[:, cols]
        vh = kv_ref[:, xw + h * hd: xw + (h + 1) * hd]
        s = lax.dot_general(qh, kh, (((1,), (1,)), ((), ())), preferred_element_type=F32) * scale
        e = jnp.exp(s - jnp.max(s, axis=-1, keepdims=True))
        p = (e / jnp.sum(e, axis=-1, keepdims=True)).astype(BF16)
        yh = jnp.dot(p, vh, preferred_element_type=F32)
        y_ref[:, gw + cw + h * hd: gw + cw + (h + 1) * hd] = yh.astype(y_ref.dtype)


def _branches(uv, mid, kv, wsp, bsp, gg, cw, *, seq, ts, gmlp_w, conv_w, xattn_w, n_mem):
    n = uv.shape[0]
    chunk = wsp.shape[1]
    groups = wsp.shape[0]
    sub = BF16_ROW_TILE
    cb_blk = 0
    q_blk = 3 * conv_w // xattn_w
    last = n // sub - 1
    tsb = ts // sub

    def prev_map(col):
        return lambda i: (jnp.maximum(i * tsb - 1, 0), col)

    def next_map(col):
        return lambda i: (jnp.minimum((i + 1) * tsb, last), col)

    kern = functools.partial(_branch_kernel, seq=seq, chunk=chunk, groups=groups, heads=XATTN_HEADS)
    return pl.pallas_call(
        kern,
        grid=(n // ts,),
        in_specs=[
            pl.BlockSpec((ts, gmlp_w), lambda i: (i, 0)),
            pl.BlockSpec((ts, gmlp_w), lambda i: (i, 1)),
            pl.BlockSpec((ts, conv_w), lambda i: (i, cb_blk)),
            pl.BlockSpec((ts, conv_w), lambda i: (i, cb_blk + 1)),
            pl.BlockSpec((ts, conv_w), lambda i: (i, cb_blk + 2)),
            pl.BlockSpec((ts, xattn_w), lambda i: (i, q_blk)),
            pl.BlockSpec((sub, conv_w), prev_map(cb_blk + 1)),
            pl.BlockSpec((sub, conv_w), prev_map(cb_blk + 2)),
            pl.BlockSpec((sub, conv_w), next_map(cb_blk + 1)),
            pl.BlockSpec((sub, conv_w), next_map(cb_blk + 2)),
            pl.BlockSpec((n_mem, 2 * xattn_w), lambda i: ((i * ts) // seq, 0)),
            pl.BlockSpec(wsp.shape, lambda i: (0, 0, 0)),
            pl.BlockSpec(bsp.shape, lambda i: (0, 0, 0)),
            pl.BlockSpec((1, gmlp_w), lambda i: (0, 0)),
            pl.BlockSpec(cw.shape, lambda i: (0, 0)),
        ],
        out_specs=pl.BlockSpec((ts, gmlp_w + conv_w + xattn_w), lambda i: (i, 0)),
        out_shape=jax.ShapeDtypeStruct((n, gmlp_w + conv_w + xattn_w), BF16),
        compiler_params=_params("parallel"),
        name="branches",
    )(uv, uv, mid, mid, mid, mid, mid, mid, mid, mid, kv, wsp, bsp, gg, cw)


def _merge_kernel(y_ref, w_ref, ga_ref, gb_ref, gx_ref, o_ref, *, gmlp_w, conv_w, sub):
    ya = y_ref[:, :gmlp_w]
    yb = y_ref[:, gmlp_w:gmlp_w + conv_w]
    yx = y_ref[:, gmlp_w + conv_w:]
    for s in range(o_ref.shape[1] // sub):
        cols = slice(s * sub, (s + 1) * sub)
        a = jnp.dot(ya, w_ref[:gmlp_w, cols], preferred_element_type=F32)
        b = jnp.dot(yb, w_ref[gmlp_w:gmlp_w + conv_w, cols], preferred_element_type=F32)
        x = jnp.dot(yx, w_ref[gmlp_w + conv_w:, cols], preferred_element_type=F32)
        o_ref[:, cols] = (ga_ref[:, cols].astype(F32) * a + gb_ref[:, cols].astype(F32) * b
                          + gx_ref[:, cols].astype(F32) * x).astype(o_ref.dtype)


def _merge(ycat, w_br, gates, *, tm, tn, d, gmlp_w, conv_w):
    n, k = ycat.shape
    g0 = 0
    gstep = d // tn
    kern = functools.partial(_merge_kernel, gmlp_w=gmlp_w, conv_w=conv_w, sub=min(MXU_COLS, tn))
    return pl.pallas_call(
        kern,
        grid=(n // tm, d // tn),
        in_specs=[pl.BlockSpec((tm, k), lambda i, j: (i, 0)),
                  pl.BlockSpec((k, tn), lambda i, j: (0, j)),
                  pl.BlockSpec((tm, tn), lambda i, j: (i, g0 + j)),
                  pl.BlockSpec((tm, tn), lambda i, j: (i, g0 + gstep + j)),
                  pl.BlockSpec((tm, tn), lambda i, j: (i, g0 + 2 * gstep + j))],
        out_specs=pl.BlockSpec((tm, tn), lambda i, j: (i, j)),
        out_shape=jax.ShapeDtypeStruct((n, d), BF16),
        compiler_params=_params("parallel", "arbitrary"),
        name="merge",
    )(ycat, w_br, gates, gates, gates)


def _outproj_kernel(m_ref, w_ref, x_ref, o_ref, *, sub):
    m = m_ref[...]
    for s in range(o_ref.shape[1] // sub):
        cols = slice(s * sub, (s + 1) * sub)
        o_ref[:, cols] = x_ref[:, cols] + jnp.dot(m, w_ref[:, cols], preferred_element_type=F32)


def _outproj(merged, w_out, x, tm, tn):
    n, d = x.shape
    return pl.pallas_call(
        functools.partial(_outproj_kernel, sub=min(MXU_COLS, tn)),
        grid=(n // tm, d // tn),
        in_specs=[pl.BlockSpec((tm, d), lambda i, j: (i, 0)),
                  pl.BlockSpec((d, tn), lambda i, j: (0, j)),
                  pl.BlockSpec((tm, tn), lambda i, j: (i, j))],
        out_specs=pl.BlockSpec((tm, tn), lambda i, j: (i, j)),
        out_shape=jax.ShapeDtypeStruct((n, d), F32),
        compiler_params=_params("parallel", "arbitrary"),
        name="out_proj",
    )(merged, w_out, x)


def _pack_bf16_pair(lo, hi):
    lo_bits = pltpu.bitcast(lo.astype(BF16).astype(F32), jnp.uint32)
    hi_bits = pltpu.bitcast(hi.astype(BF16).astype(F32), jnp.uint32)
    return lax.shift_right_logical(lo_bits, jnp.uint32(16)) | (hi_bits & jnp.uint32(0xFFFF0000))


def _unpack_bf16_pair(words):
    lo = pltpu.bitcast(lax.shift_left(words, jnp.uint32(16)), F32)
    hi = pltpu.bitcast(words & jnp.uint32(0xFFFF0000), F32)
    return lo.astype(BF16), hi.astype(BF16)


def _router_kernel(x_ref, g_ref, wrt_ref, hp_ref, afft_ref):
    hf = _rms_scale(x_ref[...]) * g_ref[...]
    half = hf.shape[1] // 2
    words = _pack_bf16_pair(hf[:, :half], hf[:, half:])
    pw = hp_ref.shape[2]
    for c in range(hp_ref.shape[0]):
        hp_ref[c] = words[:, c * pw:(c + 1) * pw]
    lt = lax.dot_general(wrt_ref[...], hf.astype(BF16), (((1,), (1,)), ((), ())),
                         preferred_element_type=F32)
    et = jnp.exp(lt - jnp.max(lt, axis=0, keepdims=True))
    afft_ref[...] = et / jnp.sum(et, axis=0, keepdims=True)


def _router(x2, g, wr, tm, pieces):
    n, d = x2.shape
    ne = wr.shape[1]
    pw = d // 2 // pieces
    return pl.pallas_call(
        _router_kernel,
        grid=(n // tm,),
        in_specs=[pl.BlockSpec((tm, d), lambda i: (i, 0)),
                  pl.BlockSpec((1, d), lambda i: (0, 0)),
                  pl.BlockSpec((ne, d), lambda i: (0, 0))],
        out_specs=[pl.BlockSpec((pieces, tm, pw), lambda i: (0, i, 0)),
                   pl.BlockSpec((ne, tm), lambda i: (0, i))],
        out_shape=[jax.ShapeDtypeStruct((pieces, n, pw), jnp.uint32),
                   jax.ShapeDtypeStruct((ne, n), F32)],
        compiler_params=_params("parallel"),
        name="ffn_norm_router",
    )(x2, g.reshape(1, d), wr.T)


def _select_kernel(aff_ref, slot_ref, base_ref, *, cap, tb):
    ne, n = aff_ref.shape
    bits = pltpu.bitcast(aff_ref[...], jnp.int32)
    capf = jnp.float32(cap)

    def search(k, t):
        cand = t | lax.shift_left(jnp.int32(1), 30 - k)
        cnt = jnp.sum(jnp.where(bits >= cand, 1.0, 0.0), axis=1, keepdims=True)
        return jnp.where(cnt >= capf, cand, t)

    thr = lax.fori_loop(0, 31, search, jnp.zeros((ne, 1), jnp.int32))
    n_gt = jnp.sum(jnp.where(bits > thr, 1.0, 0.0), axis=1, keepdims=True)
    need = capf - n_gt

    r = lax.broadcasted_iota(jnp.int32, (tb, tb), 0)
    c = lax.broadcasted_iota(jnp.int32, (tb, tb), 1)
    tri = jnp.where(r < c, 1.0, 0.0).astype(BF16)
    lane = lax.broadcasted_iota(jnp.int32, base_ref.shape, 1)
    eq_seen = jnp.zeros((ne, 1), F32)
    sel_seen = jnp.zeros((ne, 1), F32)
    base = jnp.zeros(base_ref.shape, F32)
    for b in range(n // tb):
        blk = pltpu.bitcast(aff_ref[:, b * tb:(b + 1) * tb], jnp.int32)
        eq = jnp.where(blk == thr, 1.0, 0.0)
        eq_rank = jnp.dot(eq.astype(BF16), tri, preferred_element_type=F32) + eq_seen
        sel = jnp.where(blk > thr, 1.0, jnp.where(eq_rank < need, eq, 0.0))
        pos = jnp.dot(sel.astype(BF16), tri, preferred_element_type=F32) + sel_seen
        slot_ref[:, b * tb:(b + 1) * tb] = jnp.where(sel > 0.0, pos, -1.0).astype(jnp.int32)
        base = jnp.where(lane == b, sel_seen, base)
        eq_seen = eq_seen + jnp.sum(eq, axis=1, keepdims=True)
        sel_seen = sel_seen + jnp.sum(sel, axis=1, keepdims=True)
    base = jnp.where(lane == n // tb, sel_seen, base)
    base_ref[...] = base.astype(jnp.int32)


def _select(afft, cap, tb):
    ne, n = afft.shape
    lanes = 128
    assert n // tb < lanes
    kern = functools.partial(_select_kernel, cap=cap, tb=tb)
    return pl.pallas_call(
        kern,
        out_shape=[jax.ShapeDtypeStruct((ne, n), jnp.int32),
                   jax.ShapeDtypeStruct((ne, lanes), jnp.int32)],
        compiler_params=pltpu.CompilerParams(vmem_limit_bytes=VMEM_LIMIT_BYTES),
        name="expert_choice_select",
    )(afft)


def _pick_list_kernel(base_ref, slot_ref, aff_ref, tok_ref, gw_ref, *, cap, nblk, chunk):
    eb = pl.program_id(0)
    b = pl.program_id(1)

    @pl.when(b == 0)
    def _():
        tok_ref[...] = jnp.zeros_like(tok_ref)
        gw_ref[...] = jnp.zeros_like(gw_ref)

    tb = slot_ref.shape[2]
    token = (lax.broadcasted_iota(jnp.int32, (1, tb), 1) + b * tb).astype(F32)
    for k in range(tok_ref.shape[0]):
        e = eb * tok_ref.shape[0] + k
        first = (base_ref[e * (nblk + 1) + b] // SUBLANES) * SUBLANES
        end = base_ref[e * (nblk + 1) + b + 1]
        slot = slot_ref[k]
        aff = aff_ref[k]

        def body(c, carry, k=k, first=first, slot=slot, aff=aff):
            lo = first + c * chunk
            st = pl.multiple_of(jnp.minimum(lo, cap - chunk), SUBLANES)
            rows = lax.broadcasted_iota(jnp.int32, (chunk, tb), 0) + st
            hit = jnp.where(rows == slot, jnp.where(rows >= lo, 1.0, 0.0), 0.0)
            tok_ref[k, pl.ds(st, chunk), :] += jnp.sum(hit * token, axis=1,
                                                       keepdims=True).astype(jnp.int32)
            gw_ref[k, pl.ds(st, chunk), :] += jnp.sum(hit * aff, axis=1, keepdims=True)
            return carry

        lax.fori_loop(0, (end - first + chunk - 1) // chunk, body, 0)


def _pick_list(base_flat, slot3, aff3, *, cap, tb, chunk, experts_per_step):
    ne, _, n = slot3.shape
    nblk = n // tb
    eps = experts_per_step
    assert cap >= chunk and (cap - chunk) % SUBLANES == 0 and chunk % SUBLANES == 0
    kern = functools.partial(_pick_list_kernel, cap=cap, nblk=nblk, chunk=chunk)
    return pl.pallas_call(
        kern,
        grid_spec=pltpu.PrefetchScalarGridSpec(
            num_scalar_prefetch=1,
            grid=(ne // eps, nblk),
            in_specs=[pl.BlockSpec((eps, 1, tb), lambda e, b, base: (e, 0, b)),
                      pl.BlockSpec((eps, 1, tb), lambda e, b, base: (e, 0, b))],
            out_specs=[pl.BlockSpec((eps, cap, 1), lambda e, b, base: (e, 0, 0)),
                       pl.BlockSpec((eps, cap, 1), lambda e, b, base: (e, 0, 0))],
        ),
        out_shape=[jax.ShapeDtypeStruct((ne, cap, 1), jnp.int32),
                   jax.ShapeDtypeStruct((ne, cap, 1), F32)],
        compiler_params=_params("parallel", "arbitrary"),
        name="expert_pick_list",
    )(base_flat, slot3, aff3)


def _row_gather_sparsecore(table, rows, *, window):
    num, width = rows.shape[0], table.shape[1]
    mesh = plsc.VectorSubcoreMesh(core_axis_name="c", subcore_axis_name="s")

    @pl.kernel(out_type=jax.ShapeDtypeStruct((num, width), table.dtype), mesh=mesh,
               name="expert_row_gather")
    def gather(table_hbm, rows_hbm, out_hbm):
        def body(rows_vmem, out_vmem):
            pltpu.sync_copy(table_hbm.at[rows_vmem.at[0]], out_vmem)

        pltpu.emit_pipeline(
            body,
            grid=(num // window,),
            in_specs=[pl.BlockSpec((1, window), index_map=lambda i: (0, i))],
            out_specs=[pl.BlockSpec((window, width), index_map=lambda i: (i, 0))],
            core_axis_name=("c", "s"),
            dimension_semantics=(pltpu.PARALLEL,),
        )(rows_hbm, out_hbm)

    return gather(table, rows.reshape(1, num))


def _gather(tok, hp, *, window):
    ne, cap, _ = tok.shape
    pieces, n, pw = hp.shape
    rows = (jnp.arange(pieces, dtype=jnp.int32)[:, None] * n + tok.reshape(1, ne * cap)).reshape(-1)
    out = _row_gather_sparsecore(hp.reshape(pieces * n, pw), rows, window=window)
    return out.reshape(pieces, ne, cap, pw)


def _ffn_up_kernel(*refs, groups, sub):
    xp_refs = refs[:groups]
    wg_ref, wu_ref = refs[groups:groups + 2]
    hid_refs = refs[groups + 2:2 * groups + 2]
    xe_refs = refs[2 * groups + 2:]

    @pl.when(pl.program_id(1) == 0)
    def _():
        for xp_ref, xe_ref in zip(xp_refs, xe_refs):
            pieces, _, pw = xp_ref.shape
            for c in range(pieces):
                lo, hi = _unpack_bf16_pair(xp_ref[c])
                xe_ref[:, c * pw:(c + 1) * pw] = lo
                xe_ref[:, (pieces + c) * pw:(pieces + c + 1) * pw] = hi

    for s in range(hid_refs[0].shape[1] // sub):
        cols = slice(s * sub, (s + 1) * sub)
        wg = wg_ref[:, cols].astype(BF16)
        wu = wu_ref[:, cols].astype(BF16)
        for xe_ref, hid_ref in zip(xe_refs, hid_refs):
            x = xe_ref[...]
            g = jnp.dot(x, wg, preferred_element_type=F32)
            u = jnp.dot(x, wu, preferred_element_type=F32)
            hid_ref[:, cols] = ((g * _sigmoid(g)) * u).astype(hid_ref.dtype)


def _ffn_down_kernel(*refs, groups, sub):
    hid_refs = refs[:groups]
    gw_refs = refs[groups:2 * groups]
    wd_ref = refs[2 * groups]
    ye_refs = refs[2 * groups + 1:]
    for s in range(ye_refs[0].shape[1] // sub):
        cols = slice(s * sub, (s + 1) * sub)
        wd = wd_ref[:, cols].astype(BF16)
        for hid_ref, gw_ref, ye_ref in zip(hid_refs, gw_refs, ye_refs):
            y = jnp.dot(hid_ref[...], wd, preferred_element_type=F32)
            ye_ref[:, cols] = (y * gw_ref[...]).astype(ye_ref.dtype)


def _ffn(xps, gws, wg, wu, wd, tf, tn):
    groups = len(xps)
    pieces, ne, _, pw = xps[0].shape
    caps = [xp.shape[2] for xp in xps]
    d, f = wg.shape[1], wg.shape[2]
    assert d == 2 * pieces * pw
    params = pltpu.CompilerParams(dimension_semantics=("parallel", "arbitrary"),
                                  vmem_limit_bytes=FFN_VMEM_LIMIT_BYTES)
    hids = pl.pallas_call(
        functools.partial(_ffn_up_kernel, groups=groups, sub=min(MXU_COLS, tf)),
        grid=(ne, f // tf),
        in_specs=[pl.BlockSpec((pieces, None, c, pw), lambda e, j: (0, e, 0, 0),
                               pipeline_mode=pl.Buffered(1)) for c in caps]
                 + [pl.BlockSpec((None, d, tf), lambda e, j: (e, 0, j)),
                    pl.BlockSpec((None, d, tf), lambda e, j: (e, 0, j))],
        out_specs=[pl.BlockSpec((None, c, tf), lambda e, j: (e, 0, j)) for c in caps],
        out_shape=[jax.ShapeDtypeStruct((ne, c, f), BF16) for c in caps],
        scratch_shapes=[pltpu.VMEM((c, d), BF16) for c in caps],
        compiler_params=params,
        name="expert_ffn_up",
    )(*xps, wg, wu)
    return pl.pallas_call(
        functools.partial(_ffn_down_kernel, groups=groups, sub=min(MXU_COLS, tn)),
        grid=(ne, d // tn),
        in_specs=[pl.BlockSpec((None, c, f), lambda e, j: (e, 0, 0)) for c in caps]
                 + [pl.BlockSpec((None, c, 1), lambda e, j: (e, 0, 0)) for c in caps]
                 + [pl.BlockSpec((None, f, tn), lambda e, j: (e, 0, j))],
        out_specs=[pl.BlockSpec((None, c, tn), lambda e, j: (e, 0, j)) for c in caps],
        out_shape=[jax.ShapeDtypeStruct((ne, c, d), BF16) for c in caps],
        compiler_params=params,
        name="expert_ffn_down",
    )(*hids, *gws, wd)


def _expert_column(ref, e):
    lane = lax.broadcasted_iota(jnp.int32, ref.shape, 1)
    return jnp.sum(jnp.where(lane == e, ref[...], 0), axis=1, keepdims=True)


def _combine_any_kernel(base_ref, slot_ref, x_ref, ya_ref, yb_ref, g_ref, o_ref, *, cap, nblk):
    b = pl.program_id(0)
    e = pl.program_id(1)
    ne = pl.num_programs(1)

    @pl.when(e == 0)
    def _():
        o_ref[...] = x_ref[...]

    tb = x_ref.shape[0]
    blk = ya_ref.shape[0]
    start = jnp.minimum(base_ref[e * (nblk + 1) + b] // blk, cap // blk - 1) * blk
    slot = _expert_column(slot_ref, e)
    cols = lax.broadcasted_iota(jnp.int32, (tb, blk), 1) + start
    oh_a = jnp.where(cols == slot, 1.0, 0.0).astype(BF16)
    oh_b = jnp.where(cols + blk == slot, 1.0, 0.0).astype(BF16)
    o_ref[...] += (jnp.dot(oh_a, ya_ref[...], preferred_element_type=F32)
                   + jnp.dot(oh_b, yb_ref[...], preferred_element_type=F32))

    @pl.when(e == ne - 1)
    def _():
        o_ref[...] = _rms_scale(o_ref[...]) * g_ref[...]


def _combine_any(base_flat, slot_tok, x2, ye, g_final, *, cap, tb):
    n, d = x2.shape
    ne = ye.shape[0]
    nblk = n // tb
    blk = tb
    last = cap // blk - 1

    def ya_map(b, e, base):
        return (e, jnp.minimum(base[e * (nblk + 1) + b] // blk, last), 0)

    def yb_map(b, e, base):
        return (e, jnp.minimum(jnp.minimum(base[e * (nblk + 1) + b] // blk, last) + 1, last), 0)

    kern = functools.partial(_combine_any_kernel, cap=cap, nblk=nblk)
    return pl.pallas_call(
        kern,
        grid_spec=pltpu.PrefetchScalarGridSpec(
            num_scalar_prefetch=1,
            grid=(nblk, ne),
            in_specs=[pl.BlockSpec((tb, ne), lambda b, e, base: (b, 0)),
                      pl.BlockSpec((tb, d), lambda b, e, base: (b, 0)),
                      pl.BlockSpec((None, blk, d), ya_map),
                      pl.BlockSpec((None, blk, d), yb_map),
                      pl.BlockSpec((1, d), lambda b, e, base: (0, 0))],
            out_specs=pl.BlockSpec((tb, d), lambda b, e, base: (b, 0)),
        ),
        out_shape=jax.ShapeDtypeStruct((n, d), F32),
        compiler_params=_params("parallel", "arbitrary"),
        name="expert_combine_any",
    )(base_flat, slot_tok, x2, ye, ye, g_final.reshape(1, d))


def _window_start(base, e, b, *, nblk, cap, win):
    tile = jnp.minimum(base[e * (nblk + 1) + b] // BF16_ROW_TILE, (cap - win) // BF16_ROW_TILE)
    return tile * BF16_ROW_TILE


def _combine_windows_kernel(base_ref, slot_ref, x_ref, g_ref, *rest, cap, nblk, win):
    y_refs, o_ref = rest[:-1], rest[-1]
    b = pl.program_id(0)
    s = pl.program_id(1)
    per = len(y_refs)

    @pl.when(s == 0)
    def _():
        o_ref[...] = x_ref[...]

    tb = x_ref.shape[0]
    col = lax.broadcasted_iota(jnp.int32, (tb, win), 1)
    onehots = []
    for k in range(per):
        e = s * per + k
        start = _window_start(base_ref, e, b, nblk=nblk, cap=cap, win=win)
        onehots.append(jnp.where(col + start == _expert_column(slot_ref, e), 1.0, 0.0).astype(BF16))
    y = jnp.concatenate([r[...] for r in y_refs], axis=0)
    o_ref[...] += jnp.dot(jnp.concatenate(onehots, axis=1), y, preferred_element_type=F32)

    @pl.when(s == pl.num_programs(1) - 1)
    def _():
        o_ref[...] = _rms_scale(o_ref[...]) * g_ref[...]


def _combine_windows(base_flat, slot_tok, x2, ye, g_final, *, cap, tb, win, experts_per_step):
    n, d = x2.shape
    ne = ye.shape[0]
    nblk = n // tb
    per = experts_per_step
    assert ne % per == 0

    def y_map(k):
        def index(b, s, base):
            e = s * per + k
            return (e, _window_start(base, e, b, nblk=nblk, cap=cap, win=win), 0)
        return index

    kern = functools.partial(_combine_windows_kernel, cap=cap, nblk=nblk, win=win)
    return pl.pallas_call(
        kern,
        grid_spec=pltpu.PrefetchScalarGridSpec(
            num_scalar_prefetch=1,
            grid=(nblk, ne // per),
            in_specs=[pl.BlockSpec((tb, ne), lambda b, s, base: (b, 0)),
                      pl.BlockSpec((tb, d), lambda b, s, base: (b, 0)),
                      pl.BlockSpec((1, d), lambda b, s, base: (0, 0))]
                     + [pl.BlockSpec((None, pl.Element(win), pl.Element(d)), y_map(k))
                        for k in range(per)],
            out_specs=pl.BlockSpec((tb, d), lambda b, s, base: (b, 0)),
        ),
        out_shape=jax.ShapeDtypeStruct((n, d), F32),
        compiler_params=_params("parallel", "arbitrary"),
        name="expert_combine_windows",
    )(base_flat, slot_tok, x2, g_final.reshape(1, d), *([ye] * per))


def _combine(base, slot_tok, x2, ye, g_final, *, cap, tb, win):
    nblk = x2.shape[0] // tb
    base_flat = base[:, :nblk + 1].reshape(-1)
    first = (base[:, :nblk] // BF16_ROW_TILE) * BF16_ROW_TILE
    fits = jnp.all(base[:, 1:nblk + 1] - jnp.minimum(first, cap - win) <= win)
    return lax.cond(
        fits,
        lambda: _combine_windows(base_flat, slot_tok, x2, ye, g_final, cap=cap, tb=tb, win=win,
                                 experts_per_step=8),
        lambda: _combine_any(base_flat, slot_tok, x2, ye, g_final, cap=cap, tb=tb))


def _mix_and_route(x3, mem3, p):
    bsz, seq, d = x3.shape
    n = bsz * seq
    n_mem = mem3.shape[1]
    x = x3.reshape(n, d)
    mem = mem3.reshape(bsz * n_mem, d)

    gmlp_w = p["gmlp_norm_g"].shape[0]
    conv_w = p["conv_w"].shape[1]
    xattn_w = p["w_mem_kv"].shape[1] // 2
    off_cb = 2 * gmlp_w
    off_g = off_cb + 3 * conv_w + xattn_w
    ne = p["w_router"].shape[1]
    cap = CAPACITY_FACTOR * n // ne

    def col_tile(cols):
        return next(t for t in (1024, 512, 256, 128) if cols % t == 0 and t <= cols)

    tm = min(1024, n)
    ts = min(512, seq)
    tb = 256 if cap >= 256 + BF16_ROW_TILE else 128

    h = _rmsnorm_cast(x, p["norm_mix_g"], ts)
    uv, mid, gates = _proj(h, p["w_in"], tm, col_tile(d), off_cb, off_g, BF16)
    m = _rmsnorm_cast(mem, p["mem_norm_g"], n_mem)
    kv = _proj_range(m, p["w_mem_kv"], 0, 2 * xattn_w, lambda x: x, n_mem, col_tile(2 * xattn_w), BF16)
    ycat = _branches(uv, mid, kv, p["w_spatial"], p["b_spatial"], p["gmlp_norm_g"].reshape(1, gmlp_w),
                     p["conv_w"], seq=seq, ts=ts, gmlp_w=gmlp_w, conv_w=conv_w, xattn_w=xattn_w,
                     n_mem=n_mem)
    merged = _merge(ycat, p["w_branch"], gates, tm=tm, tn=min(512, col_tile(d)), d=d,
                    gmlp_w=gmlp_w, conv_w=conv_w)
    x2 = _outproj(merged, p["w_out"], x, tm, col_tile(d))

    hp, afft = _router(x2, p["norm_ffn_g"], p["w_router"], ts, min(SC_ROW_PIECES, d // 256))
    slot, base = _select(afft, cap, tb)
    nblk = n // tb
    base_flat = base[:, :nblk + 1].reshape(-1)
    tok, gw = _pick_list(base_flat, slot.reshape(ne, 1, n), afft.reshape(ne, 1, n), cap=cap, tb=tb,
                         chunk=PICK_CHUNK_ROWS, experts_per_step=min(16, ne))
    xp = _gather(tok, hp, window=SC_GATHER_WINDOW)
    return {"xp": xp, "gw": gw, "x2": x2, "base": base, "slot_tok": slot.T, "cap": cap, "tb": tb}


def _combine_group(r, ye, g_final, shape):
    y = _combine(r["base"], r["slot_tok"], r["x2"], ye, g_final, cap=r["cap"], tb=r["tb"],
                 win=min(COMBINE_WINDOW_ROWS, r["tb"]))
    return y.reshape(shape)


def kernel(x_prompt, x_sample, mem_prompt, mem_sample, norm_mix_g, w_in, gmlp_norm_g, w_spatial,
           b_spatial, conv_w, mem_norm_g, w_mem_kv, w_branch_a, w_branch_b, w_branch_x, w_out,
           norm_ffn_g, w_router, w_gate, w_up, w_down, final_norm_g):
    assert w_in.shape[0] == 1, "one layer"
    groups, chunk = b_spatial.shape[1], b_spatial.shape[2]
    gc = gmlp_norm_g.shape[1] // groups
    p = {
        "norm_mix_g": norm_mix_g[0],
        "w_in": w_in[0].astype(BF16),
        "gmlp_norm_g": gmlp_norm_g[0],
        "w_spatial": w_spatial[0].astype(BF16),
        "b_spatial": jnp.broadcast_to(b_spatial[0][:, :, None], (groups, chunk, gc)),
        "conv_w": conv_w[0],
        "mem_norm_g": mem_norm_g[0],
        "w_mem_kv": w_mem_kv[0].astype(BF16),
        "w_branch": jnp.concatenate([w_branch_a[0], w_branch_b[0], w_branch_x[0]],
                                    axis=0).astype(BF16),
        "w_out": w_out[0].astype(BF16),
        "norm_ffn_g": norm_ffn_g[0],
        "w_router": w_router[0].astype(BF16),
        "w_gate": w_gate.reshape(w_gate.shape[1:]),
        "w_up": w_up.reshape(w_up.shape[1:]),
        "w_down": w_down.reshape(w_down.shape[1:]),
        "final_norm_g": final_norm_g,
    }
    routed = [_mix_and_route(x_prompt, mem_prompt, p), _mix_and_route(x_sample, mem_sample, p)]
    d = x_prompt.shape[-1]
    yes = _ffn([r["xp"] for r in routed], [r["gw"] for r in routed], p["w_gate"], p["w_up"],
               p["w_down"], min(FFN_UP_COLS, d), min(FFN_DOWN_COLS, d))
    return tuple(_combine_group(r, ye, final_norm_g, x.shape)
                 for r, ye, x in zip(routed, yes, (x_prompt, x_sample)))
```

```python
import functools

import jax
import jax.numpy as jnp
from jax import lax
from jax.experimental import pallas as pl
from jax.experimental.pallas import tpu as pltpu
from jax.experimental.pallas import tpu_sc as plsc

EPS = 1e-6
XATTN_HEADS = 4
CAPACITY_FACTOR = 2
BF16_ROW_TILE = 16
SUBLANES = 8
PICK_CHUNK_ROWS = 48
SC_ROW_PIECES = 8
SC_GATHER_WINDOW = 128
COMBINE_WINDOW_ROWS = 128
VMEM_LIMIT_BYTES = 56 * 1024 * 1024
FFN_VMEM_LIMIT_BYTES = 60 * 1024 * 1024
FFN_UP_COLS = 256
FFN_DOWN_COLS = 512

F32 = jnp.float32
BF16 = jnp.bfloat16


def _params(*semantics):
    return pltpu.CompilerParams(dimension_semantics=semantics,
                                vmem_limit_bytes=VMEM_LIMIT_BYTES)


GELU_K = 0.7978845608028654
GELU_CUBIC = 0.044715
MXU_COLS = 256


def _sigmoid(x):
    return 1.0 / (1.0 + jnp.exp(-x))


def _rms_scale(x):
    return x * lax.rsqrt(jnp.mean(x * x, axis=-1, keepdims=True) + EPS)


def _rmsnorm_cast_kernel(x_ref, g_ref, o_ref):
    o_ref[...] = (_rms_scale(x_ref[...]) * g_ref[...]).astype(o_ref.dtype)


def _rmsnorm_cast(x, g, tm):
    n, d = x.shape
    return pl.pallas_call(
        _rmsnorm_cast_kernel,
        grid=(n // tm,),
        in_specs=[pl.BlockSpec((tm, d), lambda i: (i, 0)),
                  pl.BlockSpec((1, d), lambda i: (0, 0))],
        out_specs=pl.BlockSpec((tm, d), lambda i: (i, 0)),
        out_shape=jax.ShapeDtypeStruct((n, d), BF16),
        compiler_params=_params("parallel"),
        name="rmsnorm_cast",
    )(x, g.reshape(1, d))


def _gelu_tanh(x):
    return (0.5 * x) * (1.0 + jnp.tanh(x * (GELU_K + (GELU_K * GELU_CUBIC) * (x * x))))


def _sigmoid_tanh(x):
    return 0.5 * (1.0 + jnp.tanh(0.5 * x))


def _proj_kernel(h_ref, w_ref, o_ref, *, sub, epilogue):
    h = h_ref[...]
    for s in range(o_ref.shape[1] // sub):
        cols = slice(s * sub, (s + 1) * sub)
        x = jnp.dot(h, w_ref[:, cols], preferred_element_type=F32)
        o_ref[:, cols] = epilogue(x).astype(o_ref.dtype)


def _proj_range(h, w, col_lo, col_hi, epilogue, tm, tn, out_dtype):
    n, d = h.shape
    tn = min(tn, col_hi - col_lo)
    assert col_lo % tn == 0 and col_hi % tn == 0
    first = col_lo // tn
    kern = functools.partial(_proj_kernel, sub=min(MXU_COLS, tn), epilogue=epilogue)
    return pl.pallas_call(
        kern,
        grid=(n // tm, (col_hi - col_lo) // tn),
        in_specs=[pl.BlockSpec((tm, d), lambda i, j: (i, 0)),
                  pl.BlockSpec((d, tn), lambda i, j: (0, first + j))],
        out_specs=pl.BlockSpec((tm, tn), lambda i, j: (i, j)),
        out_shape=jax.ShapeDtypeStruct((n, col_hi - col_lo), out_dtype),
        compiler_params=_params("parallel", "arbitrary"),
        name="in_proj",
    )(h, w)


def _proj(h, w, tm, tn, gelu_cols, sigmoid_from_col, out_dtype):
    cols = w.shape[1]
    return (_proj_range(h, w, 0, gelu_cols, _gelu_tanh, tm, tn, out_dtype),
            _proj_range(h, w, gelu_cols, sigmoid_from_col, lambda x: x, tm, tn, out_dtype),
            _proj_range(h, w, sigmoid_from_col, cols, _sigmoid_tanh, tm, tn, out_dtype))


def _branch_kernel(u_ref, v_ref, cb_ref, cc_ref, ch_ref, q_ref, ccp_ref, chp_ref, ccn_ref, chn_ref,
                   kv_ref, wsp_ref, bsp_ref, gg_ref, cw_ref, y_ref, *, seq, chunk, groups, heads):
    i = pl.program_id(0)
    ts = u_ref.shape[0]
    gw = u_ref.shape[1]
    gc = gw // groups
    cw = cb_ref.shape[1]
    xw = q_ref.shape[1]
    hd = xw // heads

    vn = (_rms_scale(v_ref[...].astype(F32)) * gg_ref[...]).astype(BF16)
    for c in range(ts // chunk):
        rows = slice(c * chunk, (c + 1) * chunk)
        for g in range(groups):
            cols = slice(g * gc, (g + 1) * gc)
            mixed = jnp.dot(wsp_ref[g], vn[rows, cols], preferred_element_type=F32) + bsp_ref[g]
            y_ref[rows, cols] = (u_ref[rows, cols].astype(F32) * mixed).astype(y_ref.dtype)

    pos = (i * ts) % seq
    z = cc_ref[...].astype(F32) * ch_ref[...].astype(F32)
    sub = ccp_ref.shape[0]
    z_before = jnp.where(pos == 0, 0.0, ccp_ref[sub - 1:sub, :].astype(F32)
                         * chp_ref[sub - 1:sub, :].astype(F32))
    z_after = jnp.where(pos + ts == seq, 0.0, ccn_ref[0:1, :].astype(F32)
                        * chn_ref[0:1, :].astype(F32))
    row = lax.broadcasted_iota(jnp.int32, z.shape, 0)
    z_prev = jnp.where(row == 0, z_before, pltpu.roll(z, 1, 0))
    z_next = jnp.where(row == ts - 1, z_after, pltpu.roll(z, ts - 1, 0))
    conv = z_prev * cw_ref[0:1, :] + z * cw_ref[1:2, :] + z_next * cw_ref[2:3, :]
    y_ref[:, gw:gw + cw] = (cb_ref[...].astype(F32) * conv).astype(y_ref.dtype)

    scale = hd ** -0.5
    for h in range(heads):
        cols = slice(h * hd, (h + 1) * hd)
        qh = q_ref[:, cols]
        kh = kv_ref[:, cols]
        vh = kv_ref[:, xw + h * hd: xw + (h + 1) * hd]
        s = lax.dot_general(qh, kh, (((1,), (1,)), ((), ())), preferred_element_type=F32) * scale
        e = jnp.exp(s - jnp.max(s, axis=-1, keepdims=True))
        p = (e / jnp.sum(e, axis=-1, keepdims=True)).astype(BF16)
        yh = jnp.dot(p, vh, preferred_element_type=F32)
        y_ref[:, gw + cw + h * hd: gw + cw + (h + 1) * hd] = yh.astype(y_ref.dtype)


def _branches(uv, mid, kv, wsp, bsp, gg, cw, *, seq, ts, gmlp_w, conv_w, xattn_w, n_mem):
    n = uv.shape[0]
    chunk = wsp.shape[1]
    groups = wsp.shape[0]
    sub = BF16_ROW_TILE
    cb_blk = 0
    q_blk = 3 * conv_w // xattn_w
    last = n // sub - 1
    tsb = ts // sub

    def prev_map(col):
        return lambda i: (jnp.maximum(i * tsb - 1, 0), col)

    def next_map(col):
        return lambda i: (jnp.minimum((i + 1) * tsb, last), col)

    kern = functools.partial(_branch_kernel, seq=seq, chunk=chunk, groups=groups, heads=XATTN_HEADS)
    return pl.pallas_call(
        kern,
        grid=(n // ts,),
        in_specs=[
            pl.BlockSpec((ts, gmlp_w), lambda i: (i, 0)),
            pl.BlockSpec((ts, gmlp_w), lambda i: (i, 1)),
            pl.BlockSpec((ts, conv_w), lambda i: (i, cb_blk)),
            pl.BlockSpec((ts, conv_w), lambda i: (i, cb_blk + 1)),
            pl.BlockSpec((ts, conv_w), lambda i: (i, cb_blk + 2)),
            pl.BlockSpec((ts, xattn_w), lambda i: (i, q_blk)),
            pl.BlockSpec((sub, conv_w), prev_map(cb_blk + 1)),
            pl.BlockSpec((sub, conv_w), prev_map(cb_blk + 2)),
            pl.BlockSpec((sub, conv_w), next_map(cb_blk + 1)),
            pl.BlockSpec((sub, conv_w), next_map(cb_blk + 2)),
            pl.BlockSpec((n_mem, 2 * xattn_w), lambda i: ((i * ts) // seq, 0)),
            pl.BlockSpec(wsp.shape, lambda i: (0, 0, 0)),
            pl.BlockSpec(bsp.shape, lambda i: (0, 0, 0)),
            pl.BlockSpec((1, gmlp_w), lambda i: (0, 0)),
            pl.BlockSpec(cw.shape, lambda i: (0, 0)),
        ],
        out_specs=pl.BlockSpec((ts, gmlp_w + conv_w + xattn_w), lambda i: (i, 0)),
        out_shape=jax.ShapeDtypeStruct((n, gmlp_w + conv_w + xattn_w), BF16),
        compiler_params=_params("parallel"),
        name="branches",
    )(uv, uv, mid, mid, mid, mid, mid, mid, mid, mid, kv, wsp, bsp, gg, cw)


def _merge_kernel(y_ref, w_ref, ga_ref, gb_ref, gx_ref, o_ref, *, gmlp_w, conv_w, sub):
    ya = y_ref[:, :gmlp_w]
    yb = y_ref[:, gmlp_w:gmlp_w + conv_w]
    yx = y_ref[:, gmlp_w + conv_w:]
    for s in range(o_ref.shape[1] // sub):
        cols = slice(s * sub, (s + 1) * sub)
        a = jnp.dot(ya, w_ref[:gmlp_w, cols], preferred_element_type=F32)
        b = jnp.dot(yb, w_ref[gmlp_w:gmlp_w + conv_w, cols], preferred_element_type=F32)
        x = jnp.dot(yx, w_ref[gmlp_w + conv_w:, cols], preferred_element_type=F32)
        o_ref[:, cols] = (ga_ref[:, cols].astype(F32) * a + gb_ref[:, cols].astype(F32) * b
                          + gx_ref[:, cols].astype(F32) * x).astype(o_ref.dtype)


def _merge(ycat, w_br, gates, *, tm, tn, d, gmlp_w, conv_w):
    n, k = ycat.shape
    g0 = 0
    gstep = d // tn
    kern = functools.partial(_merge_kernel, gmlp_w=gmlp_w, conv_w=conv_w, sub=min(MXU_COLS, tn))
    return pl.pallas_call(
        kern,
        grid=(n // tm, d // tn),
        in_specs=[pl.BlockSpec((tm, k), lambda i, j: (i, 0)),
                  pl.BlockSpec((k, tn), lambda i, j: (0, j)),
                  pl.BlockSpec((tm, tn), lambda i, j: (i, g0 + j)),
                  pl.BlockSpec((tm, tn), lambda i, j: (i, g0 + gstep + j)),
                  pl.BlockSpec((tm, tn), lambda i, j: (i, g0 + 2 * gstep + j))],
        out_specs=pl.BlockSpec((tm, tn), lambda i, j: (i, j)),
        out_shape=jax.ShapeDtypeStruct((n, d), BF16),
        compiler_params=pltpu.CompilerParams(dimension_semantics=("parallel", "arbitrary"),
                                             vmem_limit_bytes=FFN_VMEM_LIMIT_BYTES),
        name="merge",
    )(ycat, w_br, gates, gates, gates)


def _outproj_kernel(m_ref, w_ref, x_ref, o_ref, *, sub):
    m = m_ref[...]
    for s in range(o_ref.shape[1] // sub):
        cols = slice(s * sub, (s + 1) * sub)
        o_ref[:, cols] = x_ref[:, cols] + jnp.dot(m, w_ref[:, cols], preferred_element_type=F32)


def _outproj(merged, w_out, x, tm, tn):
    n, d = x.shape
    return pl.pallas_call(
        functools.partial(_outproj_kernel, sub=min(MXU_COLS, tn)),
        grid=(n // tm, d // tn),
        in_specs=[pl.BlockSpec((tm, d), lambda i, j: (i, 0)),
                  pl.BlockSpec((d, tn), lambda i, j: (0, j)),
                  pl.BlockSpec((tm, tn), lambda i, j: (i, j))],
        out_specs=pl.BlockSpec((tm, tn), lambda i, j: (i, j)),
        out_shape=jax.ShapeDtypeStruct((n, d), F32),
        compiler_params=_params("parallel", "arbitrary"),
        name="out_proj",
    )(merged, w_out, x)


def _pack_bf16_pair(lo, hi):
    lo_bits = pltpu.bitcast(lo.astype(BF16).astype(F32), jnp.uint32)
    hi_bits = pltpu.bitcast(hi.astype(BF16).astype(F32), jnp.uint32)
    return lax.shift_right_logical(lo_bits, jnp.uint32(16)) | (hi_bits & jnp.uint32(0xFFFF0000))


def _unpack_bf16_pair(words):
    lo = pltpu.bitcast(lax.shift_left(words, jnp.uint32(16)), F32)
    hi = pltpu.bitcast(words & jnp.uint32(0xFFFF0000), F32)
    return lo.astype(BF16), hi.astype(BF16)


def _router_kernel(x_ref, g_ref, wrt_ref, hp_ref, afft_ref):
    hf = _rms_scale(x_ref[...]) * g_ref[...]
    half = hf.shape[1] // 2
    words = _pack_bf16_pair(hf[:, :half], hf[:, half:])
    pw = hp_ref.shape[2]
    for c in range(hp_ref.shape[0]):
        hp_ref[c] = words[:, c * pw:(c + 1) * pw]
    lt = lax.dot_general(wrt_ref[...], hf.astype(BF16), (((1,), (1,)), ((), ())),
                         preferred_element_type=F32)
    et = jnp.exp(lt - jnp.max(lt, axis=0, keepdims=True))
    afft_ref[...] = et / jnp.sum(et, axis=0, keepdims=True)


def _router(x2, g, wr, tm, pieces):
    n, d = x2.shape
    ne = wr.shape[1]
    pw = d // 2 // pieces
    return pl.pallas_call(
        _router_kernel,
        grid=(n // tm,),
        in_specs=[pl.BlockSpec((tm, d), lambda i: (i, 0)),
                  pl.BlockSpec((1, d), lambda i: (0, 0)),
                  pl.BlockSpec((ne, d), lambda i: (0, 0))],
        out_specs=[pl.BlockSpec((pieces, tm, pw), lambda i: (0, i, 0)),
                   pl.BlockSpec((ne, tm), lambda i: (0, i))],
        out_shape=[jax.ShapeDtypeStruct((pieces, n, pw), jnp.uint32),
                   jax.ShapeDtypeStruct((ne, n), F32)],
        compiler_params=_params("parallel"),
        name="ffn_norm_router",
    )(x2, g.reshape(1, d), wr.T)


def _select_kernel(aff_ref, slot_ref, base_ref, *, cap, tb):
    ne, n = aff_ref.shape
    bits = pltpu.bitcast(aff_ref[...], jnp.int32)
    capf = jnp.float32(cap)

    def search(k, t):
        cand = t | lax.shift_left(jnp.int32(1), 30 - k)
        cnt = jnp.sum(jnp.where(bits >= cand, 1.0, 0.0), axis=1, keepdims=True)
        return jnp.where(cnt >= capf, cand, t)

    thr = lax.fori_loop(0, 31, search, jnp.zeros((ne, 1), jnp.int32))
    n_gt = jnp.sum(jnp.where(bits > thr, 1.0, 0.0), axis=1, keepdims=True)
    need = capf - n_gt

    r = lax.broadcasted_iota(jnp.int32, (tb, tb), 0)
    c = lax.broadcasted_iota(jnp.int32, (tb, tb), 1)
    tri = jnp.where(r < c, 1.0, 0.0).astype(BF16)
    lane = lax.broadcasted_iota(jnp.int32, base_ref.shape, 1)
    eq_seen = jnp.zeros((ne, 1), F32)
    sel_seen = jnp.zeros((ne, 1), F32)
    base = jnp.zeros(base_ref.shape, F32)
    for b in range(n // tb):
        blk = pltpu.bitcast(aff_ref[:, b * tb:(b + 1) * tb], jnp.int32)
        eq = jnp.where(blk == thr, 1.0, 0.0)
        eq_rank = jnp.dot(eq.astype(BF16), tri, preferred_element_type=F32) + eq_seen
        sel = jnp.where(blk > thr, 1.0, jnp.where(eq_rank < need, eq, 0.0))
        pos = jnp.dot(sel.astype(BF16), tri, preferred_element_type=F32) + sel_seen
        slot_ref[:, b * tb:(b + 1) * tb] = jnp.where(sel > 0.0, pos, -1.0).astype(jnp.int32)
        base = jnp.where(lane == b, sel_seen, base)
        eq_seen = eq_seen + jnp.sum(eq, axis=1, keepdims=True)
        sel_seen = sel_seen + jnp.sum(sel, axis=1, keepdims=True)
    base = jnp.where(lane == n // tb, sel_seen, base)
    base_ref[...] = base.astype(jnp.int32)


def _select(afft, cap, tb):
    ne, n = afft.shape
    lanes = 128
    assert n // tb < lanes
    kern = functools.partial(_select_kernel, cap=cap, tb=tb)
    return pl.pallas_call(
        kern,
        out_shape=[jax.ShapeDtypeStruct((ne, n), jnp.int32),
                   jax.ShapeDtypeStruct((ne, lanes), jnp.int32)],
        compiler_params=pltpu.CompilerParams(vmem_limit_bytes=VMEM_LIMIT_BYTES),
        name="expert_choice_select",
    )(afft)


def _pick_list_kernel(base_ref, slot_ref, aff_ref, tok_ref, gw_ref, *, cap, nblk, chunk):
    eb = pl.program_id(0)
    b = pl.program_id(1)

    @pl.when(b == 0)
    def _():
        tok_ref[...] = jnp.zeros_like(tok_ref)
        gw_ref[...] = jnp.zeros_like(gw_ref)

    tb = slot_ref.shape[2]
    token = (lax.broadcasted_iota(jnp.int32, (1, tb), 1) + b * tb).astype(F32)
    for k in range(tok_ref.shape[0]):
        e = eb * tok_ref.shape[0] + k
        first = (base_ref[e * (nblk + 1) + b] // SUBLANES) * SUBLANES
        end = base_ref[e * (nblk + 1) + b + 1]
        slot = slot_ref[k]
        aff = aff_ref[k]

        def body(c, carry, k=k, first=first, slot=slot, aff=aff):
            lo = first + c * chunk
            st = pl.multiple_of(jnp.minimum(lo, cap - chunk), SUBLANES)
            rows = lax.broadcasted_iota(jnp.int32, (chunk, tb), 0) + st
            hit = jnp.where(rows == slot, jnp.where(rows >= lo, 1.0, 0.0), 0.0)
            tok_ref[k, pl.ds(st, chunk), :] += jnp.sum(hit * token, axis=1,
                                                       keepdims=True).astype(jnp.int32)
            gw_ref[k, pl.ds(st, chunk), :] += jnp.sum(hit * aff, axis=1, keepdims=True)
            return carry

        lax.fori_loop(0, (end - first + chunk - 1) // chunk, body, 0)


def _pick_list(base_flat, slot3, aff3, *, cap, tb, chunk, experts_per_step):
    ne, _, n = slot3.shape
    nblk = n // tb
    eps = experts_per_step
    assert cap >= chunk and (cap - chunk) % SUBLANES == 0 and chunk % SUBLANES == 0
    kern = functools.partial(_pick_list_kernel, cap=cap, nblk=nblk, chunk=chunk)
    return pl.pallas_call(
        kern,
        grid_spec=pltpu.PrefetchScalarGridSpec(
            num_scalar_prefetch=1,
            grid=(ne // eps, nblk),
            in_specs=[pl.BlockSpec((eps, 1, tb), lambda e, b, base: (e, 0, b)),
                      pl.BlockSpec((eps, 1, tb), lambda e, b, base: (e, 0, b))],
            out_specs=[pl.BlockSpec((eps, cap, 1), lambda e, b, base: (e, 0, 0)),
                       pl.BlockSpec((eps, cap, 1), lambda e, b, base: (e, 0, 0))],
        ),
        out_shape=[jax.ShapeDtypeStruct((ne, cap, 1), jnp.int32),
                   jax.ShapeDtypeStruct((ne, cap, 1), F32)],
        compiler_params=_params("parallel", "arbitrary"),
        name="expert_pick_list",
    )(base_flat, slot3, aff3)


def _row_gather_sparsecore(table, rows, *, window):
    num, width = rows.shape[0], table.shape[1]
    mesh = plsc.VectorSubcoreMesh(core_axis_name="c", subcore_axis_name="s")

    @pl.kernel(out_type=jax.ShapeDtypeStruct((num, width), table.dtype), mesh=mesh,
               name="expert_row_gather")
    def gather(table_hbm, rows_hbm, out_hbm):
        def body(rows_vmem, out_vmem):
            pltpu.sync_copy(table_hbm.at[rows_vmem.at[0]], out_vmem)

        pltpu.emit_pipeline(
            body,
            grid=(num // window,),
            in_specs=[pl.BlockSpec((1, window), index_map=lambda i: (0, i))],
            out_specs=[pl.BlockSpec((window, width), index_map=lambda i: (i, 0))],
            core_axis_name=("c", "s"),
            dimension_semantics=(pltpu.PARALLEL,),
        )(rows_hbm, out_hbm)

    return gather(table, rows.reshape(1, num))


def _gather(tok, hp, *, window):
    ne, cap, _ = tok.shape
    pieces, n, pw = hp.shape
    rows = (jnp.arange(pieces, dtype=jnp.int32)[:, None] * n + tok.reshape(1, ne * cap)).reshape(-1)
    out = _row_gather_sparsecore(hp.reshape(pieces * n, pw), rows, window=window)
    return out.reshape(pieces, ne, cap, pw)


def _ffn_up_kernel(*refs, groups, sub):
    xp_refs = refs[:groups]
    wg_ref, wu_ref = refs[groups:groups + 2]
    hid_refs = refs[groups + 2:2 * groups + 2]
    xe_refs = refs[2 * groups + 2:]

    @pl.when(pl.program_id(1) == 0)
    def _():
        for xp_ref, xe_ref in zip(xp_refs, xe_refs):
            pieces, _, pw = xp_ref.shape
            for c in range(pieces):
                lo, hi = _unpack_bf16_pair(xp_ref[c])
                xe_ref[:, c * pw:(c + 1) * pw] = lo
                xe_ref[:, (pieces + c) * pw:(pieces + c + 1) * pw] = hi

    for s in range(hid_refs[0].shape[1] // sub):
        cols = slice(s * sub, (s + 1) * sub)
        wg = wg_ref[:, cols].astype(BF16)
        wu = wu_ref[:, cols].astype(BF16)
        for xe_ref, hid_ref in zip(xe_refs, hid_refs):
            x = xe_ref[...]
            g = jnp.dot(x, wg, preferred_element_type=F32)
            u = jnp.dot(x, wu, preferred_element_type=F32)
            hid_ref[:, cols] = ((g * _sigmoid(g)) * u).astype(hid_ref.dtype)


def _ffn_down_kernel(*refs, groups, sub):
    hid_refs = refs[:groups]
    gw_refs = refs[groups:2 * groups]
    wd_ref = refs[2 * groups]
    ye_refs = refs[2 * groups + 1:]
    for s in range(ye_refs[0].shape[1] // sub):
        cols = slice(s * sub, (s + 1) * sub)
        wd = wd_ref[:, cols].astype(BF16)
        for hid_ref, gw_ref, ye_ref in zip(hid_refs, gw_refs, ye_refs):
            y = jnp.dot(hid_ref[...], wd, preferred_element_type=F32)
            ye_ref[:, cols] = (y * gw_ref[...]).astype(ye_ref.dtype)


def _ffn(xps, gws, wg, wu, wd, tf, tn):
    groups = len(xps)
    pieces, ne, _, pw = xps[0].shape
    caps = [xp.shape[2] for xp in xps]
    d, f = wg.shape[1], wg.shape[2]
    assert d == 2 * pieces * pw
    params = pltpu.CompilerParams(dimension_semantics=("parallel", "arbitrary"),
                                  vmem_limit_bytes=FFN_VMEM_LIMIT_BYTES)
    hids = pl.pallas_call(
        functools.partial(_ffn_up_kernel, groups=groups, sub=min(MXU_COLS, tf)),
        grid=(ne, f // tf),
        in_specs=[pl.BlockSpec((pieces, None, c, pw), lambda e, j: (0, e, 0, 0),
                               pipeline_mode=pl.Buffered(1)) for c in caps]
                 + [pl.BlockSpec((None, d, tf), lambda e, j: (e, 0, j)),
                    pl.BlockSpec((None, d, tf), lambda e, j: (e, 0, j))],
        out_specs=[pl.BlockSpec((None, c, tf), lambda e, j: (e, 0, j)) for c in caps],
        out_shape=[jax.ShapeDtypeStruct((ne, c, f), BF16) for c in caps],
        scratch_shapes=[pltpu.VMEM((c, d), BF16) for c in caps],
        compiler_params=params,
        name="expert_ffn_up",
    )(*xps, wg, wu)
    return pl.pallas_call(
        functools.partial(_ffn_down_kernel, groups=groups, sub=min(MXU_COLS, tn)),
        grid=(ne, d // tn),
        in_specs=[pl.BlockSpec((None, c, f), lambda e, j: (e, 0, 0)) for c in caps]
                 + [pl.BlockSpec((None, c, 1), lambda e, j: (e, 0, 0)) for c in caps]
                 + [pl.BlockSpec((None, f, tn), lambda e, j: (e, 0, j))],
        out_specs=[pl.BlockSpec((None, c, tn), lambda e, j: (e, 0, j)) for c in caps],
        out_shape=[jax.ShapeDtypeStruct((ne, c, d), BF16) for c in caps],
        compiler_params=params,
        name="expert_ffn_down",
    )(*hids, *gws, wd)


def _expert_column(ref, e):
    lane = lax.broadcasted_iota(jnp.int32, ref.shape, 1)
    return jnp.sum(jnp.where(lane == e, ref[...], 0), axis=1, keepdims=True)


def _combine_any_kernel(base_ref, slot_ref, x_ref, ya_ref, yb_ref, g_ref, o_ref, *, cap, nblk):
    b = pl.program_id(0)
    e = pl.program_id(1)
    ne = pl.num_programs(1)

    @pl.when(e == 0)
    def _():
        o_ref[...] = x_ref[...]

    tb = x_ref.shape[0]
    blk = ya_ref.shape[0]
    start = jnp.minimum(base_ref[e * (nblk + 1) + b] // blk, cap // blk - 1) * blk
    slot = _expert_column(slot_ref, e)
    cols = lax.broadcasted_iota(jnp.int32, (tb, blk), 1) + start
    oh_a = jnp.where(cols == slot, 1.0, 0.0).astype(BF16)
    oh_b = jnp.where(cols + blk == slot, 1.0, 0.0).astype(BF16)
    o_ref[...] += (jnp.dot(oh_a, ya_ref[...], preferred_element_type=F32)
                   + jnp.dot(oh_b, yb_ref[...], preferred_element_type=F32))

    @pl.when(e == ne - 1)
    def _():
        o_ref[...] = _rms_scale(o_ref[...]) * g_ref[...]


def _combine_any(base_flat, slot_tok, x2, ye, g_final, *, cap, tb):
    n, d = x2.shape
    ne = ye.shape[0]
    nblk = n // tb
    blk = tb
    last = cap // blk - 1

    def ya_map(b, e, base):
        return (e, jnp.minimum(base[e * (nblk + 1) + b] // blk, last), 0)

    def yb_map(b, e, base):
        return (e, jnp.minimum(jnp.minimum(base[e * (nblk + 1) + b] // blk, last) + 1, last), 0)

    kern = functools.partial(_combine_any_kernel, cap=cap, nblk=nblk)
    return pl.pallas_call(
        kern,
        grid_spec=pltpu.PrefetchScalarGridSpec(
            num_scalar_prefetch=1,
            grid=(nblk, ne),
            in_specs=[pl.BlockSpec((tb, ne), lambda b, e, base: (b, 0)),
                      pl.BlockSpec((tb, d), lambda b, e, base: (b, 0)),
                      pl.BlockSpec((None, blk, d), ya_map),
                      pl.BlockSpec((None, blk, d), yb_map),
                      pl.BlockSpec((1, d), lambda b, e, base: (0, 0))],
            out_specs=pl.BlockSpec((tb, d), lambda b, e, base: (b, 0)),
        ),
        out_shape=jax.ShapeDtypeStruct((n, d), F32),
        compiler_params=_params("parallel", "arbitrary"),
        name="expert_combine_any",
    )(base_flat, slot_tok, x2, ye, ye, g_final.reshape(1, d))


def _window_start(base, e, b, *, nblk, cap, win):
    tile = jnp.minimum(base[e * (nblk + 1) + b] // BF16_ROW_TILE, (cap - win) // BF16_ROW_TILE)
    return tile * BF16_ROW_TILE


def _combine_windows_kernel(base_ref, slot_ref, x_ref, g_ref, *rest, cap, nblk, win):
    y_refs, o_ref = rest[:-1], rest[-1]
    b = pl.program_id(0)
    s = pl.program_id(1)
    per = len(y_refs)

    @pl.when(s == 0)
    def _():
        o_ref[...] = x_ref[...]

    tb = x_ref.shape[0]
    col = lax.broadcasted_iota(jnp.int32, (tb, win), 1)
    onehots = []
    for k in range(per):
        e = s * per + k
        start = _window_start(base_ref, e, b, nblk=nblk, cap=cap, win=win)
        onehots.append(jnp.where(col + start == _expert_column(slot_ref, e), 1.0, 0.0).astype(BF16))
    y = jnp.concatenate([r[...] for r in y_refs], axis=0)
    o_ref[...] += jnp.dot(jnp.concatenate(onehots, axis=1), y, preferred_element_type=F32)

    @pl.when(s == pl.num_programs(1) - 1)
    def _():
        o_ref[...] = _rms_scale(o_ref[...]) * g_ref[...]


def _combine_windows(base_flat, slot_tok, x2, ye, g_final, *, cap, tb, win, experts_per_step):
    n, d = x2.shape
    ne = ye.shape[0]
    nblk = n // tb
    per = experts_per_step
    assert ne % per == 0

    def y_map(k):
        def index(b, s, base):
            e = s * per + k
            return (e, _window_start(base, e, b, nblk=nblk, cap=cap, win=win), 0)
        return index

    kern = functools.partial(_combine_windows_kernel, cap=cap, nblk=nblk, win=win)
    return pl.pallas_call(
        kern,
        grid_spec=pltpu.PrefetchScalarGridSpec(
            num_scalar_prefetch=1,
            grid=(nblk, ne // per),
            in_specs=[pl.BlockSpec((tb, ne), lambda b, s, base: (b, 0)),
                      pl.BlockSpec((tb, d), lambda b, s, base: (b, 0)),
                      pl.BlockSpec((1, d), lambda b, s, base: (0, 0))]
                     + [pl.BlockSpec((None, pl.Element(win), pl.Element(d)), y_map(k))
                        for k in range(per)],
            out_specs=pl.BlockSpec((tb, d), lambda b, s, base: (b, 0)),
        ),
        out_shape=jax.ShapeDtypeStruct((n, d), F32),
        compiler_params=_params("parallel", "arbitrary"),
        name="expert_combine_windows",
    )(base_flat, slot_tok, x2, g_final.reshape(1, d), *([ye] * per))


def _combine(base, slot_tok, x2, ye, g_final, *, cap, tb, win):
    nblk = x2.shape[0] // tb
    base_flat = base[:, :nblk + 1].reshape(-1)
    first = (base[:, :nblk] // BF16_ROW_TILE) * BF16_ROW_TILE
    fits = jnp.all(base[:, 1:nblk + 1] - jnp.minimum(first, cap - win) <= win)
    return lax.cond(
        fits,
        lambda: _combine_windows(base_flat, slot_tok, x2, ye, g_final, cap=cap, tb=tb, win=win,
                                 experts_per_step=8),
        lambda: _combine_any(base_flat, slot_tok, x2, ye, g_final, cap=cap, tb=tb))


def _mix_and_route(x3, mem3, p):
    bsz, seq, d = x3.shape
    n = bsz * seq
    n_mem = mem3.shape[1]
    x = x3.reshape(n, d)
    mem = mem3.reshape(bsz * n_mem, d)

    gmlp_w = p["gmlp_norm_g"].shape[0]
    conv_w = p["conv_w"].shape[1]
    xattn_w = p["w_mem_kv"].shape[1] // 2
    off_cb = 2 * gmlp_w
    off_g = off_cb + 3 * conv_w + xattn_w
    ne = p["w_router"].shape[1]
    cap = CAPACITY_FACTOR * n // ne

    def col_tile(cols):
        return next(t for t in (1024, 512, 256, 128) if cols % t == 0 and t <= cols)

    tm = min(1024, n)
    ts = min(512, seq)
    tb = 256 if cap >= 256 + BF16_ROW_TILE else 128

    h = _rmsnorm_cast(x, p["norm_mix_g"], ts)
    uv, mid, gates = _proj(h, p["w_in"], tm, col_tile(d), off_cb, off_g, BF16)
    m = _rmsnorm_cast(mem, p["mem_norm_g"], n_mem)
    kv = _proj_range(m, p["w_mem_kv"], 0, 2 * xattn_w, lambda x: x, n_mem, col_tile(2 * xattn_w), BF16)
    ycat = _branches(uv, mid, kv, p["w_spatial"], p["b_spatial"], p["gmlp_norm_g"].reshape(1, gmlp_w),
                     p["conv_w"], seq=seq, ts=ts, gmlp_w=gmlp_w, conv_w=conv_w, xattn_w=xattn_w,
                     n_mem=n_mem)
    merged = _merge(ycat, p["w_branch"], gates, tm=tm, tn=col_tile(d), d=d,
                    gmlp_w=gmlp_w, conv_w=conv_w)
    x2 = _outproj(merged, p["w_out"], x, tm, col_tile(d))

    hp, afft = _router(x2, p["norm_ffn_g"], p["w_router"], ts, min(SC_ROW_PIECES, d // 256))
    slot, base = _select(afft, cap, tb)
    nblk = n // tb
    base_flat = base[:, :nblk + 1].reshape(-1)
    tok, gw = _pick_list(base_flat, slot.reshape(ne, 1, n), afft.reshape(ne, 1, n), cap=cap, tb=tb,
                         chunk=PICK_CHUNK_ROWS, experts_per_step=min(16, ne))
    xp = _gather(tok, hp, window=SC_GATHER_WINDOW)
    return {"xp": xp, "gw": gw, "x2": x2, "base": base, "slot_tok": slot.T, "cap": cap, "tb": tb}


def _combine_group(r, ye, g_final, shape):
    y = _combine(r["base"], r["slot_tok"], r["x2"], ye, g_final, cap=r["cap"], tb=r["tb"],
                 win=min(COMBINE_WINDOW_ROWS, r["tb"]))
    return y.reshape(shape)


def kernel(x_prompt, x_sample, mem_prompt, mem_sample, norm_mix_g, w_in, gmlp_norm_g, w_spatial,
           b_spatial, conv_w, mem_norm_g, w_mem_kv, w_branch_a, w_branch_b, w_branch_x, w_out,
           norm_ffn_g, w_router, w_gate, w_up, w_down, final_norm_g):
    assert w_in.shape[0] == 1, "one layer"
    groups, chunk = b_spatial.shape[1], b_spatial.shape[2]
    gc = gmlp_norm_g.shape[1] // groups
    p = {
        "norm_mix_g": norm_mix_g[0],
        "w_in": w_in[0].astype(BF16),
        "gmlp_norm_g": gmlp_norm_g[0],
        "w_spatial": w_spatial[0].astype(BF16),
        "b_spatial": jnp.broadcast_to(b_spatial[0][:, :, None], (groups, chunk, gc)),
        "conv_w": conv_w[0],
        "mem_norm_g": mem_norm_g[0],
        "w_mem_kv": w_mem_kv[0].astype(BF16),
        "w_branch": jnp.concatenate([w_branch_a[0], w_branch_b[0], w_branch_x[0]],
                                    axis=0).astype(BF16),
        "w_out": w_out[0].astype(BF16),
        "norm_ffn_g": norm_ffn_g[0],
        "w_router": w_router[0].astype(BF16),
        "w_gate": w_gate.reshape(w_gate.shape[1:]),
        "w_up": w_up.reshape(w_up.shape[1:]),
        "w_down": w_down.reshape(w_down.shape[1:]),
        "final_norm_g": final_norm_g,
    }
    routed = [_mix_and_route(x_prompt, mem_prompt, p), _mix_and_route(x_sample, mem_sample, p)]
    d = x_prompt.shape[-1]
    yes = _ffn([r["xp"] for r in routed], [r["gw"] for r in routed], p["w_gate"], p["w_up"],
               p["w_down"], min(FFN_UP_COLS, d), min(FFN_DOWN_COLS, d))
    return tuple(_combine_group(r, ye, final_norm_g, x.shape)
                 for r, ye, x in zip(routed, yes, (x_prompt, x_sample)))
```
